```python
import math
import jax, jax.numpy as jnp
from jax import lax
import numpy as np


D_MODEL = 2048
BATCH = 2
SEQ = 8192
DEPTH = 2

HEAD_DIM = 128
N_HEADS_SB = 8
N_HEADS_DW = 8
DW_PATTERNS = ((128, 1), (512, 4), (2048, 16))
Q_BLOCK = 128
SSM_WIDTH = 1024
SSM_GROUP = 16
SSM_GROUPS = SSM_WIDTH // SSM_GROUP
SSM_STATE = 64
N_HEADS_GDN = 8
GDN_DK = 128
GDN_DV = 128
GDN_CONV = 4
GDN_CHUNK = 64
D_FF = 5632
N_EXPERTS = 8
TOP_K = 2
D_FF_EXPERT = 7168
DEEPNORM_ALPHA = (2 * DEPTH) ** 0.25
DEEPNORM_BETA = (8 * DEPTH) ** -0.25
LN_EPS = 1e-5
RMS_EPS = 1e-6

W_SB = N_HEADS_SB * HEAD_DIM
W_DW = N_HEADS_DW * HEAD_DIM
IN0 = 3 * W_SB + 3 * W_DW
MIX0 = W_SB + W_DW
GDN_QKV = N_HEADS_GDN * (2 * GDN_DK + GDN_DV)
IN1 = SSM_WIDTH + GDN_QKV + N_HEADS_GDN * GDN_DV + 2 * N_HEADS_GDN
MIX1 = SSM_WIDTH + N_HEADS_GDN * GDN_DV
N_EVEN = (DEPTH + 1) // 2
N_ODD = DEPTH // 2

kernel_name = 'hybrid_sb_dilated_s5_gdn_moe_deepnorm'

F32 = jnp.float32


def layer_norm(x, g, b):
    xf = x.astype(F32)
    mu = jnp.mean(xf, -1, keepdims=True)
    var = jnp.mean(jnp.square(xf - mu), -1, keepdims=True)
    return ((xf - mu) * lax.rsqrt(var + LN_EPS) * g + b).astype(x.dtype)


def split_heads(t, head_dim):
    b, s, _ = t.shape
    return t.reshape(b, s, -1, head_dim).transpose(0, 2, 1, 3)


def stick_breaking_attention(q, k, v):
    b, h, s, hd = q.shape
    nb = s // Q_BLOCK
    scale = hd ** -0.5
    kf = k.astype(F32)
    vf = v.astype(F32)
    qb = q.astype(F32).reshape(b, h, nb, Q_BLOCK, hd).transpose(2, 0, 1, 3, 4)
    starts = jnp.arange(nb, dtype=jnp.int32) * Q_BLOCK
    kpos = jnp.arange(s, dtype=jnp.int32)

    def block(args):
        qblk, q0 = args
        z = jnp.einsum('bhqd,bhkd->bhqk', qblk, kf) * scale
        qpos = q0 + jnp.arange(Q_BLOCK, dtype=jnp.int32)
        past = kpos[None, :] < qpos[:, None]
        log_keep = jnp.where(past, jax.nn.log_sigmoid(-z), 0.0)
        later = lax.cumsum(log_keep, axis=3, reverse=True) - log_keep
        w = jnp.where(past, jnp.exp(jax.nn.log_sigmoid(z) + later), 0.0)
        return jnp.einsum('bhqk,bhkd->bhqd', w, vf)

    out = lax.map(block, (qb, starts))
    return out.transpose(1, 2, 0, 3, 4).reshape(b, h, s, hd)


def dilated_window_attention(q, k, v, window, dilation):
    b, h, s, hd = q.shape
    n_keys = window // dilation
    sub_len = s // dilation
    nb = -(-sub_len // n_keys)
    padded = nb * n_keys
    scale = hd ** -0.5

    def to_sub(t):
        return t.astype(F32).reshape(b, h, sub_len, dilation, hd).transpose(0, 1, 3, 2, 4)

    qs = jnp.pad(to_sub(q), ((0, 0),) * 3 + ((0, padded - sub_len), (0, 0)))
    qs = qs.reshape(b, h, dilation, nb, n_keys, hd)

    def key_blocks(t):
        tp = jnp.pad(to_sub(t), ((0, 0),) * 3 + ((n_keys, padded - sub_len), (0, 0)))
        tp = tp.reshape(b, h, dilation, nb + 1, n_keys, hd)
        return jnp.concatenate([tp[:, :, :, :-1], tp[:, :, :, 1:]], axis=4)

    ks = key_blocks(k)
    vs = key_blocks(v)
    i = jnp.arange(n_keys)[:, None]
    j = jnp.arange(2 * n_keys)[None, :]
    dist = n_keys + i - j
    band = (dist >= 0) & (dist <= n_keys)
    blk = jnp.arange(nb)[:, None, None]
    mask = band[None] & ~((blk == 0) & (j[None] < n_keys))
    sc = jnp.einsum('bhrnqd,bhrnkd->bhrnqk', qs, ks) * scale
    sc = jnp.where(mask, sc, -jnp.inf)
    m = jnp.max(sc, -1, keepdims=True)
    p = jnp.exp(sc - m)
    den = jnp.sum(p, -1, keepdims=True)
    o = jnp.einsum('bhrnqk,bhrnkd->bhrnqd', p, vs) / den
    lse = (m + jnp.log(den))[..., 0]
    o = o.reshape(b, h, dilation, padded, hd)[:, :, :, :sub_len]
    o = o.transpose(0, 1, 3, 2, 4).reshape(b, h, s, hd)
    lse = lse.reshape(b, h, dilation, padded)[..., :sub_len].transpose(0, 1, 3, 2).reshape(b, h, s)
    return o, lse


def dilated_mixture(q, k, v):
    outs, lses = [], []
    for window, dilation in DW_PATTERNS:
        o, l = dilated_window_attention(q, k, v, window, dilation)
        outs.append(o)
        lses.append(l)
    wts = jax.nn.softmax(jnp.stack(lses), axis=0)
    return jnp.einsum('pbhs,pbhsd->bhsd', wts, jnp.stack(outs))


def s5_mixer(u, lam_re, lam_im, log_dt, b_re, b_im, c_re, c_im, d_skip, glu_w, glu_b):
    bsz, s, _ = u.shape
    uf = u.astype(F32)
    lam = lax.complex(lam_re.astype(F32), lam_im.astype(F32))
    dt = jnp.exp(log_dt.astype(F32))[:, None]
    lam_bar = jnp.exp(lam * dt)
    b_mat = lax.complex(b_re.astype(F32), b_im.astype(F32))
    b_bar = ((lam_bar - 1.0) / lam)[..., None] * b_mat
    c_mat = lax.complex(c_re.astype(F32), c_im.astype(F32))
    ug = uf.reshape(bsz, s, SSM_GROUPS, SSM_GROUP).astype(jnp.complex64)
    bu = jnp.einsum('gph,bsgh->bsgp', b_bar, ug)
    a = jnp.broadcast_to(lam_bar, bu.shape)

    def combine(e1, e2):
        a1, x1 = e1
        a2, x2 = e2
        return a1 * a2, a2 * x1 + x2

    _, states = lax.associative_scan(combine, (a, bu), axis=1)
    y = jnp.einsum('ghp,bsgp->bsgh', c_mat, states).real.reshape(bsz, s, SSM_WIDTH)
    y = y + d_skip.astype(F32) * uf
    z = jax.nn.gelu(y)
    return z * jax.nn.sigmoid(z @ glu_w.astype(F32) + glu_b.astype(F32))


def causal_depthwise_conv(x, w):
    return lax.conv_general_dilated(
        x, w[:, None, :].astype(x.dtype), window_strides=(1,),
        padding=((GDN_CONV - 1, 0),), dimension_numbers=('NWC', 'WIO', 'NWC'),
        feature_group_count=x.shape[-1])


def l2norm(t):
    return t * lax.rsqrt(jnp.sum(t * t, -1, keepdims=True) + RMS_EPS)


def gated_delta_rule(q, k, v, g, beta):
    b, s, h, dk = q.shape
    dv = v.shape[-1]
    c = GDN_CHUNK
    nc = s // c

    def chunk(t):
        return jnp.moveaxis(t.reshape((b, nc, c) + t.shape[2:]), 3, 2)

    qc = chunk(q) * dk ** -0.5
    kc = chunk(k)
    vc = chunk(v)
    bc = chunk(beta)
    gc = lax.cumsum(chunk(g), axis=3)
    tri = jnp.tril(jnp.ones((c, c), bool))
    strict = jnp.tril(jnp.ones((c, c), bool), -1)
    decay = jnp.exp(jnp.where(tri, gc[..., :, None] - gc[..., None, :], -jnp.inf))
    kb = kc * bc[..., None]
    lower = jnp.where(strict, jnp.einsum('bnhid,bnhjd->bnhij', kb, kc) * decay, 0.0)
    mat = jnp.eye(c, dtype=F32) + lower

    def solve(rhs):
        return lax.linalg.triangular_solve(mat, rhs, left_side=True, lower=True, unit_diagonal=True)

    u_val = solve(vc * bc[..., None])
    w_key = solve(kb * jnp.exp(gc)[..., None])
    intra = jnp.einsum('bnhid,bnhjd->bnhij', qc, kc) * decay
    q_dec = qc * jnp.exp(gc)[..., None]
    k_dec = kc * jnp.exp(gc[..., -1:] - gc)[..., None]
    g_last = jnp.exp(gc[..., -1])

    def step(state, xs):
        u_c, w_c, qd, kd, att, gl = xs
        v_new = u_c - jnp.einsum('bhck,bhkv->bhcv', w_c, state)
        o = jnp.einsum('bhck,bhkv->bhcv', qd, state) + jnp.einsum('bhij,bhjv->bhiv', att, v_new)
        state = state * gl[..., None, None] + jnp.einsum('bhck,bhcv->bhkv', kd, v_new)
        return state, o

    xs = tuple(jnp.moveaxis(t, 1, 0) for t in (u_val, w_key, q_dec, k_dec, intra, g_last))
    state0 = jnp.zeros((b, h, dk, dv), F32)
    _, o = lax.scan(step, state0, xs)
    return jnp.moveaxis(o, 0, 1).transpose(0, 1, 3, 2, 4).reshape(b, s, h, dv)


def gated_deltanet(qkv, gate, beta_logit, a_logit, conv_w, a_log, dt_bias, norm_w):
    b, s, _ = qkv.shape
    qkv = jax.nn.silu(causal_depthwise_conv(qkv.astype(F32), conv_w.astype(F32)))
    q, k, v = jnp.split(qkv, [N_HEADS_GDN * GDN_DK, 2 * N_HEADS_GDN * GDN_DK], axis=-1)
    q = l2norm(q.reshape(b, s, N_HEADS_GDN, GDN_DK))
    k = l2norm(k.reshape(b, s, N_HEADS_GDN, GDN_DK))
    v = v.reshape(b, s, N_HEADS_GDN, GDN_DV)
    beta = jax.nn.sigmoid(beta_logit.astype(F32))
    g = -jnp.exp(a_log.astype(F32)) * jax.nn.softplus(a_logit.astype(F32) + dt_bias.astype(F32))
    o = gated_delta_rule(q, k, v, g, beta)
    o = o * lax.rsqrt(jnp.mean(o * o, -1, keepdims=True) + RMS_EPS) * norm_w.astype(F32)
    o = o * jax.nn.silu(gate.astype(F32).reshape(b, s, N_HEADS_GDN, GDN_DV))
    return o.reshape(b, s, N_HEADS_GDN * GDN_DV)


def even_mixer(x, w_in, w_out):
    b, s, _ = x.shape
    h = x @ w_in
    cuts = [W_SB, 2 * W_SB, 3 * W_SB, 3 * W_SB + W_DW, 3 * W_SB + 2 * W_DW]
    qa, ka, va, qb, kb, vb = jnp.split(h, cuts, axis=-1)
    oa = stick_breaking_attention(split_heads(qa, HEAD_DIM), split_heads(ka, HEAD_DIM), split_heads(va, HEAD_DIM))
    ob = dilated_mixture(split_heads(qb, HEAD_DIM), split_heads(kb, HEAD_DIM), split_heads(vb, HEAD_DIM))
    o = jnp.concatenate([oa, ob], axis=1).transpose(0, 2, 1, 3).reshape(b, s, MIX0)
    return o.astype(x.dtype) @ w_out


def odd_mixer(x, w_in, lam_re, lam_im, log_dt, b_re, b_im, c_re, c_im, d_skip, glu_w, glu_b,
              conv_w, a_log, dt_bias, norm_w, w_out):
    h = x @ w_in
    c1 = SSM_WIDTH
    c2 = c1 + GDN_QKV
    c3 = c2 + N_HEADS_GDN * GDN_DV
    c4 = c3 + N_HEADS_GDN
    u, qkv, gate, beta_logit, a_logit = jnp.split(h, [c1, c2, c3, c4], axis=-1)
    oc = s5_mixer(u, lam_re, lam_im, log_dt, b_re, b_im, c_re, c_im, d_skip, glu_w, glu_b)
    od = gated_deltanet(qkv, gate, beta_logit, a_logit, conv_w, a_log, dt_bias, norm_w)
    o = jnp.concatenate([oc, od], axis=-1)
    return o.astype(x.dtype) @ w_out


def swiglu(x, w1, w3, w2):
    return (jax.nn.silu(x @ w1) * (x @ w3)) @ w2


def moe_swiglu(x, router_w, w1, w3, w2):
    logits = jnp.einsum('bsd,de->bse', x, router_w).astype(F32)
    top_val, top_idx = lax.top_k(logits, TOP_K)
    top_w = jax.nn.softmax(top_val, axis=-1)
    gates = jnp.sum(jax.nn.one_hot(top_idx, N_EXPERTS, dtype=F32) * top_w[..., None], axis=-2)
    y = jnp.zeros_like(x)
    for e in range(N_EXPERTS):
        y = y + gates[..., e:e + 1].astype(x.dtype) * swiglu(x, w1[e], w3[e], w2[e])
    return y


def setup_inputs(seed: int = 0) -> dict:
    key = jax.random.key(seed)
    ks = list(jax.random.split(key, 40))

    def nrm(shape, std):
        return jax.random.normal(ks.pop(), shape, F32) * std

    def unif(shape, lo, hi):
        return jax.random.uniform(ks.pop(), shape, F32, lo, hi)

    ne, no = N_EVEN, N_ODD
    d = D_MODEL
    inp = {}
    inp['x'] = nrm((BATCH, SEQ, d), 1.0)
    inp['even_w_in'] = nrm((ne, d, IN0), d ** -0.5)
    inp['even_w_out'] = nrm((ne, MIX0, d), MIX0 ** -0.5 * DEEPNORM_BETA)
    inp['even_ln_mix_g'] = 1.0 + nrm((ne, d), 0.02)
    inp['even_ln_mix_b'] = nrm((ne, d), 0.02)
    inp['even_ffn_w1'] = nrm((ne, d, D_FF), d ** -0.5)
    inp['even_ffn_w3'] = nrm((ne, d, D_FF), d ** -0.5)
    inp['even_ffn_w2'] = nrm((ne, D_FF, d), D_FF ** -0.5 * DEEPNORM_BETA)
    inp['even_ln_ffn_g'] = 1.0 + nrm((ne, d), 0.02)
    inp['even_ln_ffn_b'] = nrm((ne, d), 0.02)
    inp['odd_w_in'] = nrm((no, d, IN1), d ** -0.5)
    inp['odd_ssm_lam_re'] = -0.5 + nrm((no, SSM_GROUPS, SSM_STATE), 0.01)
    inp['odd_ssm_lam_im'] = math.pi * jnp.arange(SSM_STATE, dtype=F32) + nrm((no, SSM_GROUPS, SSM_STATE), 0.01)
    inp['odd_ssm_log_dt'] = unif((no, SSM_GROUPS), math.log(1e-3), math.log(1e-1))
    inp['odd_ssm_b_re'] = nrm((no, SSM_GROUPS, SSM_STATE, SSM_GROUP), (2 * SSM_GROUP) ** -0.5)
    inp['odd_ssm_b_im'] = nrm((no, SSM_GROUPS, SSM_STATE, SSM_GROUP), (2 * SSM_GROUP) ** -0.5)
    inp['odd_ssm_c_re'] = nrm((no, SSM_GROUPS, SSM_GROUP, SSM_STATE), SSM_STATE ** -0.5)
    inp['odd_ssm_c_im'] = nrm((no, SSM_GROUPS, SSM_GROUP, SSM_STATE), SSM_STATE ** -0.5)
    inp['odd_ssm_d'] = nrm((no, SSM_WIDTH), 1.0)
    inp['odd_glu_w'] = nrm((no, SSM_WIDTH, SSM_WIDTH), SSM_WIDTH ** -0.5)
    inp['odd_glu_b'] = nrm((no, SSM_WIDTH), 0.02)
    inp['odd_gdn_conv_w'] = nrm((no, GDN_CONV, GDN_QKV), GDN_CONV ** -0.5)
    inp['odd_gdn_a_log'] = jnp.log(unif((no, N_HEADS_GDN), 1.0, 16.0))
    dt0 = jnp.exp(unif((no, N_HEADS_GDN), math.log(1e-3), math.log(1e-1)))
    inp['odd_gdn_dt_bias'] = dt0 + jnp.log(-jnp.expm1(-dt0))
    inp['odd_gdn_norm_w'] = 1.0 + nrm((no, GDN_DV), 0.02)
    inp['odd_w_out'] = nrm((no, MIX1, d), MIX1 ** -0.5 * DEEPNORM_BETA)
    inp['odd_ln_mix_g'] = 1.0 + nrm((no, d), 0.02)
    inp['odd_ln_mix_b'] = nrm((no, d), 0.02)
    inp['odd_router_w'] = nrm((no, d, N_EXPERTS), d ** -0.5)
    inp['odd_moe_w1'] = nrm((no, N_EXPERTS, d, D_FF_EXPERT), d ** -0.5)
    inp['odd_moe_w3'] = nrm((no, N_EXPERTS, d, D_FF_EXPERT), d ** -0.5)
    inp['odd_moe_w2'] = nrm((no, N_EXPERTS, D_FF_EXPERT, d), D_FF_EXPERT ** -0.5 * DEEPNORM_BETA)
    inp['odd_ln_ffn_g'] = 1.0 + nrm((no, d), 0.02)
    inp['odd_ln_ffn_b'] = nrm((no, d), 0.02)
    return inp


def reference(x, even_w_in, even_w_out, even_ln_mix_g, even_ln_mix_b, even_ffn_w1, even_ffn_w3,
              even_ffn_w2, even_ln_ffn_g, even_ln_ffn_b, odd_w_in, odd_ssm_lam_re, odd_ssm_lam_im,
              odd_ssm_log_dt, odd_ssm_b_re, odd_ssm_b_im, odd_ssm_c_re, odd_ssm_c_im, odd_ssm_d,
              odd_glu_w, odd_glu_b, odd_gdn_conv_w, odd_gdn_a_log, odd_gdn_dt_bias, odd_gdn_norm_w,
              odd_w_out, odd_ln_mix_g, odd_ln_mix_b, odd_router_w, odd_moe_w1, odd_moe_w3,
              odd_moe_w2, odd_ln_ffn_g, odd_ln_ffn_b):
    for layer in range(DEPTH):
        i = layer // 2
        if layer % 2 == 0:
            mix = even_mixer(x, even_w_in[i], even_w_out[i])
            x = layer_norm(DEEPNORM_ALPHA * x + mix, even_ln_mix_g[i], even_ln_mix_b[i])
            ffn = swiglu(x, even_ffn_w1[i], even_ffn_w3[i], even_ffn_w2[i])
            x = layer_norm(DEEPNORM_ALPHA * x + ffn, even_ln_ffn_g[i], even_ln_ffn_b[i])
        else:
            mix = odd_mixer(x, odd_w_in[i], odd_ssm_lam_re[i], odd_ssm_lam_im[i], odd_ssm_log_dt[i],
                            odd_ssm_b_re[i], odd_ssm_b_im[i], odd_ssm_c_re[i], odd_ssm_c_im[i],
                            odd_ssm_d[i], odd_glu_w[i], odd_glu_b[i], odd_gdn_conv_w[i],
                            odd_gdn_a_log[i], odd_gdn_dt_bias[i], odd_gdn_norm_w[i], odd_w_out[i])
            x = layer_norm(DEEPNORM_ALPHA * x + mix, odd_ln_mix_g[i], odd_ln_mix_b[i])
            ffn = moe_swiglu(x, odd_router_w[i], odd_moe_w1[i], odd_moe_w3[i], odd_moe_w2[i])
            x = layer_norm(DEEPNORM_ALPHA * x + ffn, odd_ln_ffn_g[i], odd_ln_ffn_b[i])
    return x
```

```python
import functools
import math

import jax
import jax.numpy as jnp
from jax import lax
from jax.experimental import pallas as pl
from jax.experimental.pallas import tpu as pltpu

F32 = jnp.float32
BF16 = jnp.bfloat16
HIGHEST = lax.Precision.HIGHEST

HEAD_DIM = 128
N_HEADS_SB = 8
N_HEADS_DW = 8
DW_PATTERNS = ((128, 1), (512, 4), (2048, 16))
DW_KEYS = 128
DW_TILE = 2048
SSM_WIDTH = 1024
SSM_GROUP = 16
SSM_GROUPS = 64
SSM_STATE = 64
SSM_CHUNK = 16
N_HEADS_GDN = 8
GDN_D = 128
GDN_CONV = 4
GDN_CHUNK = 64
N_EXPERTS = 8
DEPTH = 2
DEEPNORM_ALPHA = (2 * DEPTH) ** 0.25
LN_EPS = 1e-5
RMS_EPS = 1e-6

LANES = 128
SUBLANES = 8
VMEM_LIMIT = 56 * 1024 * 1024
SB_SKIP_LOG = -104.0


def _cp(*sem):
    return pltpu.CompilerParams(dimension_semantics=sem, vmem_limit_bytes=VMEM_LIMIT)


def _layer_norm(y, g, b):
    mu = jnp.mean(y, axis=-1, keepdims=True)
    yc = y - mu
    var = jnp.mean(yc * yc, axis=-1, keepdims=True)
    return yc * lax.rsqrt(var + LN_EPS) * g + b


def _softplus(x):
    return jnp.maximum(x, 0.0) + jnp.log1p(jnp.exp(-jnp.abs(x)))


def _dot(a, b, precision=None):
    return jnp.dot(a, b, preferred_element_type=F32, precision=precision)


def _dot_nt(a, b, precision=None):
    return lax.dot_general(a, b, (((1,), (1,)), ((), ())),
                           preferred_element_type=F32, precision=precision)


def _dot_tn(a, b, precision=None):
    return lax.dot_general(a, b, (((0,), (0,)), ((), ())),
                           preferred_element_type=F32, precision=precision)


def _mm_kernel(a_ref, b_ref, o_ref, *, precision):
    o_ref[...] = _dot(a_ref[...], b_ref[...], precision).astype(o_ref.dtype)


def _matmul(a, b, out_dtype, tm, tn, name, precision=None):
    m, k = a.shape
    n = b.shape[1]
    return pl.pallas_call(
        functools.partial(_mm_kernel, precision=precision),
        grid=(m // tm, n // tn),
        in_specs=[pl.BlockSpec((tm, k), lambda i, j: (i, 0)),
                  pl.BlockSpec((k, tn), lambda i, j: (0, j))],
        out_specs=pl.BlockSpec((tm, tn), lambda i, j: (i, j)),
        out_shape=jax.ShapeDtypeStruct((m, n), out_dtype),
        compiler_params=_cp("parallel", "parallel"),
        name=name,
    )(a, b)


def _proj_ln_kernel(a0_ref, a1_ref, w0_ref, w1_ref, res_ref, g_ref, b_ref, *rest, with_router):
    if with_router:
        rw_ref, of_ref, ob_ref, rt_ref = rest
    else:
        of_ref, ob_ref = rest
    mix = _dot(a0_ref[...], w0_ref[...]) + _dot(a1_ref[...], w1_ref[...])
    xn = _layer_norm(DEEPNORM_ALPHA * res_ref[...] + mix, g_ref[...], b_ref[...])
    of_ref[...] = xn
    ob_ref[...] = xn.astype(BF16)
    if with_router:
        logits = _dot(xn, rw_ref[...], HIGHEST)
        lane = lax.broadcasted_iota(jnp.int32, logits.shape, 1).astype(F32)
        neg = jnp.float32(-jnp.inf)
        lg = jnp.where(lane < N_EXPERTS, logits, neg)
        m1 = jnp.max(lg, axis=1, keepdims=True)
        i1 = jnp.min(jnp.where(lg == m1, lane, float(LANES)), axis=1, keepdims=True)
        lg2 = jnp.where(lane == i1, neg, lg)
        m2 = jnp.max(lg2, axis=1, keepdims=True)
        i2 = jnp.min(jnp.where(lg2 == m2, lane, float(LANES)), axis=1, keepdims=True)
        e2 = jnp.exp(m2 - m1)
        p1 = 1.0 / (1.0 + e2)
        p2 = e2 * p1
        rt = jnp.where(lane == 0.0, i1, jnp.where(lane == 1.0, i2,
                       jnp.where(lane == 2.0, p1, jnp.where(lane == 3.0, p2, 0.0))))
        rt_ref[...] = rt


def _proj_ln(a0, a1, w0, w1, res, g, b, router_w=None, tm=256):
    t, d = res.shape
    k0, k1 = a0.shape[1], a1.shape[1]
    with_router = router_w is not None
    row = lambda i: (i, 0)
    fixed = lambda i: (0, 0)
    in_specs = [pl.BlockSpec((tm, k0), row), pl.BlockSpec((tm, k1), row),
                pl.BlockSpec((k0, d), fixed), pl.BlockSpec((k1, d), fixed),
                pl.BlockSpec((tm, d), row), pl.BlockSpec((1, d), fixed), pl.BlockSpec((1, d), fixed)]
    out_specs = [pl.BlockSpec((tm, d), row), pl.BlockSpec((tm, d), row)]
    out_shape = [jax.ShapeDtypeStruct((t, d), F32), jax.ShapeDtypeStruct((t, d), BF16)]
    args = [a0, a1, w0, w1, res, g.reshape(1, d), b.reshape(1, d)]
    if with_router:
        in_specs.append(pl.BlockSpec((d, LANES), fixed))
        out_specs.append(pl.BlockSpec((tm, LANES), row))
        out_shape.append(jax.ShapeDtypeStruct((t, LANES), F32))
        args.append(router_w)
    return pl.pallas_call(
        functools.partial(_proj_ln_kernel, with_router=with_router),
        grid=(t // tm,), in_specs=in_specs, out_specs=out_specs, out_shape=out_shape,
        compiler_params=_cp("parallel"),
        name="proj_ln_router" if with_router else "proj_ln",
    )(*args)


def _swiglu_acc(x, w1_ref, w3_ref, w2_ref, acc_ref):
    h1 = _dot(x, w1_ref[...])
    h3 = _dot(x, w3_ref[...])
    act = (h1 * jax.nn.sigmoid(h1) * h3).astype(BF16)
    acc_ref[...] += _dot(act, w2_ref[...])


def _ffn_kernel(x_ref, w1_ref, w3_ref, w2_ref, res_ref, g_ref, b_ref, of_ref, ob_ref, acc_ref, *, nj):
    j = pl.program_id(1)

    @pl.when(j == 0)
    def _():
        acc_ref[...] = jnp.zeros_like(acc_ref)

    _swiglu_acc(x_ref[...], w1_ref, w3_ref, w2_ref, acc_ref)

    @pl.when(j == nj - 1)
    def _():
        xn = _layer_norm(DEEPNORM_ALPHA * res_ref[...] + acc_ref[...], g_ref[...], b_ref[...])
        of_ref[...] = xn
        ob_ref[...] = xn.astype(BF16)


def _ffn_ln(xb, res, w1, w3, w2, g, b, tm=512, tf=512):
    t, d = res.shape
    dff = w1.shape[1]
    nj = dff // tf
    row = lambda i, j: (i, 0)
    fixed = lambda i, j: (0, 0)
    return pl.pallas_call(
        functools.partial(_ffn_kernel, nj=nj),
        grid=(t // tm, nj),
        in_specs=[pl.BlockSpec((tm, d), row),
                  pl.BlockSpec((d, tf), lambda i, j: (0, j)),
                  pl.BlockSpec((d, tf), lambda i, j: (0, j)),
                  pl.BlockSpec((tf, d), lambda i, j: (j, 0)),
                  pl.BlockSpec((tm, d), row), pl.BlockSpec((1, d), fixed), pl.BlockSpec((1, d), fixed)],
        out_specs=[pl.BlockSpec((tm, d), row), pl.BlockSpec((tm, d), row)],
        out_shape=[jax.ShapeDtypeStruct((t, d), F32), jax.ShapeDtypeStruct((t, d), BF16)],
        scratch_shapes=[pltpu.VMEM((tm, d), F32)],
        compiler_params=_cp("parallel", "arbitrary"),
        name="ffn_ln",
    )(xb, w1, w3, w2, res, g.reshape(1, d), b.reshape(1, d))


def _sb_kernel(q_ref, k_ref, v_ref, o_ref, acc_ref, carry_ref, *, tq, scale):
    i = pl.program_id(1)
    q = q_ref[...]
    row = lax.broadcasted_iota(jnp.int32, (tq, tq), 0)
    col = lax.broadcasted_iota(jnp.int32, (tq, tq), 1)
    later_sum = (row > col).astype(BF16)
    past = col < row

    def block(kb, diagonal):
        start = pl.multiple_of(kb * tq, tq)
        k = k_ref[pl.ds(start, tq), :]
        v = v_ref[pl.ds(start, tq), :]
        z = _dot_nt(q, k) * scale
        lk = -_softplus(z)
        if diagonal:
            lk = jnp.where(past, lk, 0.0)
        hi = lk.astype(BF16)
        lo = (lk - hi.astype(F32)).astype(BF16)
        later = _dot(hi, later_sum) + _dot(lo, later_sum)
        logw = z + lk + later + carry_ref[...]
        w = jnp.exp(logw)
        if diagonal:
            w = jnp.where(past, w, 0.0)
        acc_ref[...] += _dot(w.astype(BF16), v)
        carry_ref[...] += jnp.sum(lk, axis=1, keepdims=True)

    acc_ref[...] = jnp.zeros_like(acc_ref)
    carry_ref[...] = jnp.zeros_like(carry_ref)
    block(i, True)

    def cond(kb):
        return jnp.logical_and(kb >= 0, jnp.max(carry_ref[...]) > SB_SKIP_LOG)

    def body(kb):
        block(kb, False)
        return kb - 1

    lax.while_loop(cond, body, i - 1)
    o_ref[...] = acc_ref[...].astype(o_ref.dtype)


def _sb_attention(h, batch, seq, tq=256):
    nh = N_HEADS_SB
    nq = seq // tq
    return pl.pallas_call(
        functools.partial(_sb_kernel, tq=tq, scale=HEAD_DIM ** -0.5),
        grid=(batch * nh, nq),
        in_specs=[pl.BlockSpec((None, tq, HEAD_DIM), lambda bh, i: (bh // nh, i, bh % nh)),
                  pl.BlockSpec((None, seq, HEAD_DIM), lambda bh, i: (bh // nh, 0, nh + bh % nh)),
                  pl.BlockSpec((None, seq, HEAD_DIM), lambda bh, i: (bh // nh, 0, 2 * nh + bh % nh))],
        out_specs=pl.BlockSpec((None, tq, HEAD_DIM), lambda bh, i: (bh // nh, i, bh % nh)),
        out_shape=jax.ShapeDtypeStruct((batch, seq, nh * HEAD_DIM), BF16),
        scratch_shapes=[pltpu.VMEM((tq, HEAD_DIM), F32), pltpu.VMEM((tq, 1), F32)],
        compiler_params=_cp("parallel", "arbitrary"),
        name="sb_attention",
    )(h, h, h)


def _dw_kernel(q_ref, k_ref, v_ref, o_ref, m_ref, l_ref, acc_ref, *, scale):
    nk = DW_KEYS
    t0 = pl.program_id(1) * DW_TILE
    ii = lax.broadcasted_iota(jnp.int32, (nk, nk), 0)
    jj = lax.broadcasted_iota(jnp.int32, (nk, nk), 1)
    cur_mask = jj <= ii
    prev_mask = jj >= ii
    neg = jnp.float32(-jnp.inf)

    for p_idx, (window, dil) in enumerate(DW_PATTERNS):
        assert window // dil == nk
        span = nk * dil
        n_steps = DW_TILE // nk

        def step(s, _, dil=dil, span=span, first=(p_idx == 0)):
            blk = s // dil
            r = s % dil
            off = blk * span + r
            rows = pl.ds(off, nk, stride=dil)
            cur = pl.ds(t0 + off, nk, stride=dil)
            has_prev = (t0 + blk * span) > 0
            prev = pl.ds(jnp.maximum(t0 + blk * span - span, 0) + r, nk, stride=dil)
            qs = q_ref[rows, :].astype(BF16)
            s_c = _dot_nt(qs, k_ref[cur, :].astype(BF16)) * scale
            s_p = _dot_nt(qs, k_ref[prev, :].astype(BF16)) * scale
            s_c = jnp.where(cur_mask, s_c, neg)
            s_p = jnp.where(jnp.logical_and(prev_mask, has_prev), s_p, neg)
            m_new = jnp.maximum(jnp.max(s_c, axis=1, keepdims=True), jnp.max(s_p, axis=1, keepdims=True))
            p_c = jnp.exp(s_c - m_new)
            p_p = jnp.exp(s_p - m_new)
            l_new = jnp.sum(p_c, axis=1, keepdims=True) + jnp.sum(p_p, axis=1, keepdims=True)
            num = (_dot(p_c.astype(BF16), v_ref[cur, :].astype(BF16))
                   + _dot(p_p.astype(BF16), v_ref[prev, :].astype(BF16)))
            m_b = jnp.broadcast_to(m_new, (nk, HEAD_DIM))
            l_b = jnp.broadcast_to(l_new, (nk, HEAD_DIM))
            if first:
                m_ref[rows, :] = m_b
                l_ref[rows, :] = l_b
                acc_ref[rows, :] = num
            else:
                m_old = m_ref[rows, :]
                m_tot = jnp.maximum(m_old, m_b)
                a_old = jnp.exp(m_old - m_tot)
                a_new = jnp.exp(m_b - m_tot)
                m_ref[rows, :] = m_tot
                l_ref[rows, :] = a_old * l_ref[rows, :] + a_new * l_b
                acc_ref[rows, :] = a_old * acc_ref[rows, :] + a_new * num
            return 0

        lax.fori_loop(0, n_steps, step, 0)

    o_ref[...] = (acc_ref[...] / l_ref[...]).astype(o_ref.dtype)


def _dw_attention(h, batch, seq):
    nh = N_HEADS_DW
    return pl.pallas_call(
        functools.partial(_dw_kernel, scale=HEAD_DIM ** -0.5),
        grid=(batch * nh, seq // DW_TILE),
        in_specs=[pl.BlockSpec((None, DW_TILE, HEAD_DIM), lambda bh, i: (bh // nh, i, bh % nh)),
                  pl.BlockSpec((None, seq, HEAD_DIM), lambda bh, i: (bh // nh, 0, nh + bh % nh)),
                  pl.BlockSpec((None, seq, HEAD_DIM), lambda bh, i: (bh // nh, 0, 2 * nh + bh % nh))],
        out_specs=pl.BlockSpec((None, DW_TILE, HEAD_DIM), lambda bh, i: (bh // nh, i, bh % nh)),
        out_shape=jax.ShapeDtypeStruct((batch, seq, nh * HEAD_DIM), BF16),
        scratch_shapes=[pltpu.VMEM((DW_TILE, HEAD_DIM), F32)] * 3,
        compiler_params=_cp("parallel", "arbitrary"),
        name="dw_attention",
    )(h, h, h)


def _s5_params(lam_re, lam_im, log_dt, b_re, b_im, c_re, c_im, n_scan):
    L = SSM_CHUNK
    dt = jnp.exp(log_dt)[:, None]
    mag_log = lam_re * dt
    ang = lam_im * dt

    def power(n):
        n = jnp.asarray(n, F32)[..., None, None]
        mag = jnp.exp(mag_log * n)
        return mag * jnp.cos(ang * n), mag * jnp.sin(ang * n)

    lr, li = power(jnp.ones(()))
    den = lam_re * lam_re + lam_im * lam_im
    cr = ((lr - 1.0) * lam_re + li * lam_im) / den
    ci = (li * lam_re - (lr - 1.0) * lam_im) / den
    bbr = cr[..., None] * b_re - ci[..., None] * b_im
    bbi = cr[..., None] * b_im + ci[..., None] * b_re

    pr, pi = power(jnp.arange(L + 1, dtype=F32))
    mr = pr[:L, :, :, None] * bbr - pi[:L, :, :, None] * bbi
    mi = pr[:L, :, :, None] * bbi + pi[:L, :, :, None] * bbr
    kk = (jnp.einsum('ghp,tgpk->tghk', c_re, mr, precision=HIGHEST)
          - jnp.einsum('ghp,tgpk->tghk', c_im, mi, precision=HIGHEST))
    jj = jnp.arange(L)[:, None]
    ii = jnp.arange(L)[None, :]
    lag = ii - jj
    toe = jnp.where((lag >= 0)[:, :, None, None, None], kk[jnp.clip(lag, 0, L - 1)], 0.0)
    g = SSM_GROUPS
    toe = toe.transpose(2, 0, 4, 1, 3).reshape(g, L * SSM_GROUP, L * SSM_GROUP)
    qr = pr[L - 1 - jnp.arange(L)]
    qi = pi[L - 1 - jnp.arange(L)]
    inr = qr[..., None] * bbr - qi[..., None] * bbi
    ini = qr[..., None] * bbi + qi[..., None] * bbr
    inr = inr.transpose(1, 0, 3, 2).reshape(g, L * SSM_GROUP, SSM_STATE)
    ini = ini.transpose(1, 0, 3, 2).reshape(g, L * SSM_GROUP, SSM_STATE)
    orr = c_re[None] * pr[1:, :, None, :] - c_im[None] * pi[1:, :, None, :]
    oii = c_re[None] * pi[1:, :, None, :] + c_im[None] * pr[1:, :, None, :]
    outr = orr.transpose(1, 3, 0, 2).reshape(g, SSM_STATE, L * SSM_GROUP)
    outi = (-oii).transpose(1, 3, 0, 2).reshape(g, SSM_STATE, L * SSM_GROUP)
    ar, ai = power(float(L) * (2.0 ** jnp.arange(n_scan, dtype=F32)))
    return toe, inr, ini, outr, outi, ar.transpose(1, 0, 2), ai.transpose(1, 0, 2)


def _s5_kernel(u_ref, toe_ref, inr_ref, ini_ref, outr_ref, outi_ref, ar_ref, ai_ref, y_ref,
               *, n_scan, chunks_per_seq):
    u = u_ref[...]
    y = _dot(u, toe_ref[...], HIGHEST)
    zr = _dot(u, inr_ref[...], HIGHEST)
    zi = _dot(u, ini_ref[...], HIGHEST)
    seg = lax.broadcasted_iota(jnp.int32, zr.shape, 0) % chunks_per_seq
    for k in range(n_scan):
        s = 1 << k
        ar = ar_ref[k:k + 1, :]
        ai = ai_ref[k:k + 1, :]
        keep = seg >= s
        sr = jnp.where(keep, pltpu.roll(zr, s, 0), 0.0)
        si = jnp.where(keep, pltpu.roll(zi, s, 0), 0.0)
        zr, zi = zr + ar * sr - ai * si, zi + ar * si + ai * sr
    keep = seg >= 1
    sr = jnp.where(keep, pltpu.roll(zr, 1, 0), 0.0)
    si = jnp.where(keep, pltpu.roll(zi, 1, 0), 0.0)
    y_ref[...] = y + _dot(sr, outr_ref[...], HIGHEST) + _dot(si, outi_ref[...], HIGHEST)


def _s5_scan(u_g, ops, chunks_per_seq, n_scan):
    g, rows, width = u_g.shape
    toe, inr, ini, outr, outi, ar, ai = ops
    p = SSM_STATE
    per_g = lambda *blk: pl.BlockSpec((None,) + blk, lambda i: (i, 0, 0))
    return pl.pallas_call(
        functools.partial(_s5_kernel, n_scan=n_scan, chunks_per_seq=chunks_per_seq),
        grid=(g,),
        in_specs=[per_g(rows, width), per_g(width, width), per_g(width, p), per_g(width, p),
                  per_g(p, width), per_g(p, width), per_g(n_scan, p), per_g(n_scan, p)],
        out_specs=per_g(rows, width),
        out_shape=jax.ShapeDtypeStruct((g, rows, width), F32),
        compiler_params=_cp("parallel"),
        name="s5_scan",
    )(u_g, toe, inr, ini, outr, outi, ar, ai)


def _s5_glu_kernel(y_ref, u_ref, d_ref, w_ref, b_ref, o_ref):
    y = y_ref[...] + d_ref[...] * u_ref[...]
    c = math.sqrt(2.0 / math.pi)
    z = 0.5 * y * (1.0 + jnp.tanh(c * (y + 0.044715 * (y * y * y))))
    gate = jax.nn.sigmoid(_dot(z.astype(BF16), w_ref[...]) + b_ref[...])
    o_ref[...] = (z * gate).astype(o_ref.dtype)


def _s5_glu(y, h1, d_skip, glu_w, glu_b, tm=512):
    t, w = y.shape
    row = lambda i: (i, 0)
    fixed = lambda i: (0, 0)
    return pl.pallas_call(
        _s5_glu_kernel,
        grid=(t // tm,),
        in_specs=[pl.BlockSpec((tm, w), row), pl.BlockSpec((tm, w), row),
                  pl.BlockSpec((1, w), fixed), pl.BlockSpec((w, w), fixed), pl.BlockSpec((1, w), fixed)],
        out_specs=pl.BlockSpec((tm, w), row),
        out_shape=jax.ShapeDtypeStruct((t, w), BF16),
        compiler_params=_cp("parallel"),
        name="s5_glu",
    )(y, h1, d_skip.reshape(1, w), glu_w, glu_b.reshape(1, w))


def _gdn_kernel(q_ref, k_ref, v_ref, gate_ref, ba_ref, wq_ref, wk_ref, wv_ref, alog_ref, dtb_ref,
                nw_ref, o_ref, state_ref, hq_ref, hk_ref, hv_ref, *, lt, nh):
    c = GDN_CHUNK
    d = GDN_D
    head = pl.program_id(0) % nh

    @pl.when(pl.program_id(1) == 0)
    def _():
        state_ref[...] = jnp.zeros_like(state_ref)
        hq_ref[...] = jnp.zeros_like(hq_ref)
        hk_ref[...] = jnp.zeros_like(hk_ref)
        hv_ref[...] = jnp.zeros_like(hv_ref)

    sub = lax.broadcasted_iota(jnp.int32, (SUBLANES, d), 0)

    def conv_silu(x_ref, w_ref, halo_ref):
        x = x_ref[...]
        w = w_ref[...]
        halo = halo_ref[...]
        y = x * w[GDN_CONV - 1:GDN_CONV, :]
        for back in range(1, GDN_CONV):
            rolled = pltpu.roll(x, back, 0)
            head_rows = jnp.where(sub < back, pltpu.roll(halo, back, 0), rolled[:SUBLANES, :])
            shifted = jnp.concatenate([head_rows, rolled[SUBLANES:, :]], axis=0)
            y = y + shifted * w[GDN_CONV - 1 - back:GDN_CONV - back, :]
        halo_ref[...] = x[lt - SUBLANES:, :]
        return y * jax.nn.sigmoid(y)

    def l2norm(t):
        return t * lax.rsqrt(jnp.sum(t * t, axis=-1, keepdims=True) + RMS_EPS)

    q_all = l2norm(conv_silu(q_ref, wq_ref, hq_ref)) * (d ** -0.5)
    k_all = l2norm(conv_silu(k_ref, wk_ref, hk_ref))
    v_all = conv_silu(v_ref, wv_ref, hv_ref)

    ba = ba_ref[...]
    lane = lax.broadcasted_iota(jnp.int32, ba.shape, 1)
    beta_all = jnp.sum(jnp.where(lane == head, jax.nn.sigmoid(ba), 0.0), axis=1, keepdims=True)
    g_full = -jnp.exp(alog_ref[...]) * _softplus(ba + dtb_ref[...])
    g_all = jnp.sum(jnp.where(lane == nh + head, g_full, 0.0), axis=1, keepdims=True)

    ri = lax.broadcasted_iota(jnp.int32, (c, c), 0)
    ci = lax.broadcasted_iota(jnp.int32, (c, c), 1)
    tri = ri >= ci
    strict = ri > ci
    tril_f = tri.astype(F32)
    eye = (ri == ci).astype(F32)
    ones = jnp.ones((c, c), F32)

    for n in range(lt // c):
        sl = slice(n * c, (n + 1) * c)
        q, k, v = q_all[sl], k_all[sl], v_all[sl]
        beta, g = beta_all[sl], g_all[sl]
        g_b = jnp.broadcast_to(g, (c, c))
        gc_col = _dot(tril_f, g_b, HIGHEST)
        gc_row = _dot(ones, eye * gc_col, HIGHEST)
        decay = jnp.exp(jnp.where(tri, gc_col - gc_row, -jnp.inf))
        gc = gc_col[:, 0:1]
        gc_last = gc_row[:, c - 1:c]
        kb = k * beta
        lower = jnp.where(strict, _dot_nt(kb, k, HIGHEST) * decay, 0.0)
        neg = -lower
        inv = eye + neg
        pw = neg
        for _ in range(int(math.log2(c)) - 1):
            pw = _dot(pw, pw, HIGHEST)
            inv = inv + _dot(inv, pw, HIGHEST)
        e_gc = jnp.exp(gc)
        u_val = _dot(inv, v * beta, HIGHEST)
        w_key = _dot(inv, kb * e_gc, HIGHEST)
        intra = _dot_nt(q, k, HIGHEST) * decay
        q_dec = q * e_gc
        k_dec = k * jnp.exp(gc_last - gc)
        state = state_ref[...]
        v_new = u_val - _dot(w_key, state, HIGHEST)
        o = _dot(q_dec, state, HIGHEST) + _dot(intra, v_new, HIGHEST)
        state_ref[...] = state * jnp.exp(gc_last[0:1, :]) + _dot_tn(k_dec, v_new, HIGHEST)
        o = o * lax.rsqrt(jnp.mean(o * o, axis=-1, keepdims=True) + RMS_EPS) * nw_ref[...]
        gt = gate_ref[sl, :]
        o_ref[sl, :] = (o * (gt * jax.nn.sigmoid(gt))).astype(o_ref.dtype)


def _gated_deltanet(h1, ba, conv_w, a_log_pad, dt_bias_pad, norm_w, batch, seq, lt=256):
    nh = N_HEADS_GDN
    d = GDN_D
    t = batch * seq
    nt = seq // lt
    base = SSM_WIDTH // d
    rows = lambda off: (lambda bh, i: ((bh // nh) * nt + i, off + bh % nh))
    cw = lambda off: (lambda bh, i: (0, off + bh % nh))
    fixed = lambda bh, i: (0, 0)
    return pl.pallas_call(
        functools.partial(_gdn_kernel, lt=lt, nh=nh),
        grid=(batch * nh, nt),
        in_specs=[pl.BlockSpec((lt, d), rows(base)), pl.BlockSpec((lt, d), rows(base + nh)),
                  pl.BlockSpec((lt, d), rows(base + 2 * nh)), pl.BlockSpec((lt, d), rows(base + 3 * nh)),
                  pl.BlockSpec((lt, LANES), lambda bh, i: ((bh // nh) * nt + i, 0)),
                  pl.BlockSpec((GDN_CONV, d), cw(0)), pl.BlockSpec((GDN_CONV, d), cw(nh)),
                  pl.BlockSpec((GDN_CONV, d), cw(2 * nh)),
                  pl.BlockSpec((1, LANES), fixed), pl.BlockSpec((1, LANES), fixed),
                  pl.BlockSpec((1, d), fixed)],
        out_specs=pl.BlockSpec((lt, d), rows(0)),
        out_shape=jax.ShapeDtypeStruct((t, nh * d), BF16),
        scratch_shapes=[pltpu.VMEM((d, d), F32)] + [pltpu.VMEM((SUBLANES, d), F32)] * 3,
        compiler_params=_cp("parallel", "arbitrary"),
        name="gated_deltanet",
    )(h1, h1, h1, h1, ba, conv_w, conv_w, conv_w, a_log_pad, dt_bias_pad, norm_w.reshape(1, d))


def _row_gather_kernel(src_ref, x_hbm, o_ref, buf_ref, sem, *, tm):
    base = pl.program_id(0) * tm

    def issue(r, _):
        tok = src_ref[base + r]
        pltpu.make_async_copy(x_hbm.at[pl.ds(tok, 1), :], buf_ref.at[pl.ds(r, 1), :], sem).start()
        return 0

    lax.fori_loop(0, tm, issue, 0)
    pltpu.make_async_copy(x_hbm.at[pl.ds(0, tm), :], buf_ref, sem).wait()
    o_ref[...] = buf_ref[...].astype(o_ref.dtype)


def _row_gather(x, src, tm=256):
    p = src.shape[0]
    d = x.shape[1]
    return pl.pallas_call(
        functools.partial(_row_gather_kernel, tm=tm),
        grid_spec=pltpu.PrefetchScalarGridSpec(
            num_scalar_prefetch=1, grid=(p // tm,),
            in_specs=[pl.BlockSpec(memory_space=pl.ANY)],
            out_specs=pl.BlockSpec((tm, d), lambda i, src: (i, 0)),
            scratch_shapes=[pltpu.VMEM((tm, d), x.dtype), pltpu.SemaphoreType.DMA(())]),
        out_shape=jax.ShapeDtypeStruct((p, d), BF16),
        compiler_params=_cp("arbitrary"),
        name="moe_gather",
    )(src, x)


def _moe_kernel(te_ref, nv_ref, x_ref, w1_ref, w3_ref, w2_ref, gate_ref, o_ref, acc_ref, *, nj):
    i = pl.program_id(0)
    j = pl.program_id(1)
    valid = i < nv_ref[0]

    @pl.when(j == 0)
    def _():
        acc_ref[...] = jnp.zeros_like(acc_ref)

    @pl.when(valid)
    def _():
        _swiglu_acc(x_ref[...], w1_ref, w3_ref, w2_ref, acc_ref)

    @pl.when(j == nj - 1)
    def _():
        o_ref[...] = acc_ref[...] * gate_ref[...]


def _moe_ffn(xs, gates, tile_expert, n_valid, w1, w3, w2, tm, tf=512):
    p, d = xs.shape
    dff = w1.shape[2]
    nj = dff // tf
    jsel = lambda i, j, nv: jnp.where(i < nv[0], j, nj - 1)
    return pl.pallas_call(
        functools.partial(_moe_kernel, nj=nj),
        grid_spec=pltpu.PrefetchScalarGridSpec(
            num_scalar_prefetch=2, grid=(p // tm, nj),
            in_specs=[pl.BlockSpec((tm, d), lambda i, j, te, nv: (i, 0)),
                      pl.BlockSpec((None, d, tf), lambda i, j, te, nv: (te[i], 0, jsel(i, j, nv))),
                      pl.BlockSpec((None, d, tf), lambda i, j, te, nv: (te[i], 0, jsel(i, j, nv))),
                      pl.BlockSpec((None, tf, d), lambda i, j, te, nv: (te[i], jsel(i, j, nv), 0)),
                      pl.BlockSpec((tm, 1), lambda i, j, te, nv: (i, 0))],
            out_specs=pl.BlockSpec((tm, d), lambda i, j, te, nv: (i, 0)),
            scratch_shapes=[pltpu.VMEM((tm, d), F32)]),
        out_shape=jax.ShapeDtypeStruct((p, d), F32),
        compiler_params=_cp("parallel", "arbitrary"),
        name="moe_ffn",
    )(tile_expert, n_valid, xs, w1, w3, w2, gates)


def _combine_kernel(pos_ref, ys_hbm, res_ref, g_ref, b_ref, o_ref, buf0_ref, buf1_ref, sem, *, tm, t):
    base = pl.program_id(0) * tm

    def issue(r, _):
        p0 = pos_ref[base + r]
        p1 = pos_ref[t + base + r]
        pltpu.make_async_copy(ys_hbm.at[pl.ds(p0, 1), :], buf0_ref.at[pl.ds(r, 1), :], sem.at[0]).start()
        pltpu.make_async_copy(ys_hbm.at[pl.ds(p1, 1), :], buf1_ref.at[pl.ds(r, 1), :], sem.at[1]).start()
        return 0

    lax.fori_loop(0, tm, issue, 0)
    pltpu.make_async_copy(ys_hbm.at[pl.ds(0, tm), :], buf0_ref, sem.at[0]).wait()
    pltpu.make_async_copy(ys_hbm.at[pl.ds(0, tm), :], buf1_ref, sem.at[1]).wait()
    y = buf0_ref[...] + buf1_ref[...]
    o_ref[...] = _layer_norm(DEEPNORM_ALPHA * res_ref[...] + y, g_ref[...], b_ref[...])


def _moe_combine_ln(ys, pos, res, g, b, tm=256):
    t, d = res.shape
    return pl.pallas_call(
        functools.partial(_combine_kernel, tm=tm, t=t),
        grid_spec=pltpu.PrefetchScalarGridSpec(
            num_scalar_prefetch=1, grid=(t // tm,),
            in_specs=[pl.BlockSpec(memory_space=pl.ANY),
                      pl.BlockSpec((tm, d), lambda i, pos: (i, 0)),
                      pl.BlockSpec((1, d), lambda i, pos: (0, 0)),
                      pl.BlockSpec((1, d), lambda i, pos: (0, 0))],
            out_specs=pl.BlockSpec((tm, d), lambda i, pos: (i, 0)),
            scratch_shapes=[pltpu.VMEM((tm, d), F32), pltpu.VMEM((tm, d), F32),
                            pltpu.SemaphoreType.DMA((2,))]),
        out_shape=jax.ShapeDtypeStruct((t, d), F32),
        compiler_params=_cp("arbitrary"),
        name="moe_combine_ln",
    )(pos, ys, res, g.reshape(1, d), b.reshape(1, d))


def _moe_routing(route, tm):
    t = route.shape[0]
    e = N_EXPERTS
    idx = route[:, 0:2].astype(jnp.int32)
    wts = route[:, 2:4]
    flat_e = idx.T.reshape(-1)
    onehot = (flat_e[:, None] == jnp.arange(e, dtype=jnp.int32)[None, :]).astype(jnp.int32)
    rank = jnp.cumsum(onehot, axis=0) - onehot
    counts = jnp.sum(onehot, axis=0)
    tiles = (counts + tm - 1) // tm
    tile_end = jnp.cumsum(tiles)
    start = (tile_end - tiles) * tm
    pos = jnp.sum(onehot * (start[None, :] + rank), axis=1)
    n_slots = 2 * t + e * tm
    n_tiles = n_slots // tm
    tok = jnp.tile(jnp.arange(t, dtype=jnp.int32), 2)
    src = jnp.zeros((n_slots,), jnp.int32).at[pos].set(tok)
    gates = jnp.zeros((n_slots,), F32).at[pos].set(wts.T.reshape(-1))
    n_valid = tile_end[-1]
    tile_ids = jnp.arange(n_tiles, dtype=jnp.int32)
    tile_expert = jnp.sum((tile_ids[:, None] >= tile_end[None, :]).astype(jnp.int32), axis=1)
    last_expert = jnp.sum((n_valid - 1 >= tile_end).astype(jnp.int32))
    tile_expert = jnp.where(tile_ids < n_valid, tile_expert, last_expert).astype(jnp.int32)
    return src, gates.reshape(n_slots, 1), pos.astype(jnp.int32), tile_expert, n_valid.reshape(1).astype(jnp.int32)


def _even_layer(x, xb, batch, seq, w_in, w_out, ln_mix_g, ln_mix_b, w1, w3, w2, ln_ffn_g, ln_ffn_b):
    t, d = x.shape
    w_sb = N_HEADS_SB * HEAD_DIM
    w_in_b = w_in.astype(BF16)
    ha = _matmul(xb, w_in_b[:, :3 * w_sb], BF16, 512, 512, "in_proj_sb")
    hb = _matmul(xb, w_in_b[:, 3 * w_sb:], F32, 512, 512, "in_proj_dw")
    oa = _sb_attention(ha.reshape(batch, seq, -1), batch, seq).reshape(t, -1)
    ob = _dw_attention(hb.reshape(batch, seq, -1), batch, seq).reshape(t, -1)
    w_out_b = w_out.astype(BF16)
    x, xb = _proj_ln(oa, ob, w_out_b[:w_sb], w_out_b[w_sb:], x, ln_mix_g, ln_mix_b)
    return _ffn_ln(xb, x, w1.astype(BF16), w3.astype(BF16), w2.astype(BF16), ln_ffn_g, ln_ffn_b)


def _odd_layer(x, xb, batch, seq, w_in, lam_re, lam_im, log_dt, b_re, b_im, c_re, c_im, d_skip,
               glu_w, glu_b, conv_w, a_log, dt_bias, norm_w, w_out, ln_mix_g, ln_mix_b,
               router_w, w1, w3, w2, ln_ffn_g, ln_ffn_b):
    t, d = x.shape
    nh = N_HEADS_GDN
    wide = SSM_WIDTH + 4 * nh * GDN_D
    h1 = _matmul(xb, w_in[:, :wide].astype(BF16), F32, 512, 512, "in_proj_odd")
    w_small = jnp.pad(w_in[:, wide:], ((0, 0), (0, LANES - 2 * nh)))
    ba = _matmul(x, w_small, F32, 512, LANES, "in_proj_gates", precision=HIGHEST)

    L = SSM_CHUNK
    chunks_per_seq = seq // L
    n_scan = max(1, math.ceil(math.log2(chunks_per_seq)))
    ops = _s5_params(lam_re, lam_im, log_dt, b_re, b_im, c_re, c_im, n_scan)
    u_g = (h1[:, :SSM_WIDTH].reshape(batch * chunks_per_seq, L, SSM_GROUPS, SSM_GROUP)
           .transpose(2, 0, 1, 3).reshape(SSM_GROUPS, batch * chunks_per_seq, L * SSM_GROUP))
    y_g = _s5_scan(u_g, ops, chunks_per_seq, n_scan)
    y = (y_g.reshape(SSM_GROUPS, batch * chunks_per_seq, L, SSM_GROUP)
         .transpose(1, 2, 0, 3).reshape(t, SSM_WIDTH))
    oc = _s5_glu(y, h1, d_skip, glu_w.astype(BF16), glu_b)

    pad_hi = LANES - 2 * nh
    a_log_pad = jnp.pad(a_log, (nh, pad_hi)).reshape(1, LANES)
    dt_bias_pad = jnp.pad(dt_bias, (nh, pad_hi)).reshape(1, LANES)
    od = _gated_deltanet(h1, ba, conv_w, a_log_pad, dt_bias_pad, norm_w, batch, seq)

    w_out_b = w_out.astype(BF16)
    rw = jnp.pad(router_w, ((0, 0), (0, LANES - N_EXPERTS)))
    x, xb, route = _proj_ln(oc, od, w_out_b[:SSM_WIDTH], w_out_b[SSM_WIDTH:], x, ln_mix_g, ln_mix_b,
                            router_w=rw)

    tm = 512
    src, gates, pos, tile_expert, n_valid = _moe_routing(route, tm)
    xs = _row_gather(x, src)
    ys = _moe_ffn(xs, gates, tile_expert, n_valid, w1.astype(BF16), w3.astype(BF16), w2.astype(BF16), tm)
    return _moe_combine_ln(ys, pos, x, ln_ffn_g, ln_ffn_b)


def kernel(x, even_w_in, even_w_out, even_ln_mix_g, even_ln_mix_b, even_ffn_w1, even_ffn_w3, even_ffn_w2, even_ln_ffn_g, even_ln_ffn_b, odd_w_in, odd_ssm_lam_re, odd_ssm_lam_im, odd_ssm_log_dt, odd_ssm_b_re, odd_ssm_b_im, odd_ssm_c_re, odd_ssm_c_im, odd_ssm_d, odd_glu_w, odd_glu_b, odd_gdn_conv_w, odd_gdn_a_log, odd_gdn_dt_bias, odd_gdn_norm_w, odd_w_out, odd_ln_mix_g, odd_ln_mix_b, odd_router_w, odd_moe_w1, odd_moe_w3, odd_moe_w2, odd_ln_ffn_g, odd_ln_ffn_b):
    batch, seq, d = x.shape
    xf = x.reshape(batch * seq, d)
    xf, xb = _even_layer(xf, xf.astype(BF16), batch, seq, even_w_in[0], even_w_out[0],
                         even_ln_mix_g[0], even_ln_mix_b[0], even_ffn_w1[0], even_ffn_w3[0],
                         even_ffn_w2[0], even_ln_ffn_g[0], even_ln_ffn_b[0])
    out = _odd_layer(xf, xb, batch, seq, odd_w_in[0], odd_ssm_lam_re[0], odd_ssm_lam_im[0],
                     odd_ssm_log_dt[0], odd_ssm_b_re[0], odd_ssm_b_im[0], odd_ssm_c_re[0],
                     odd_ssm_c_im[0], odd_ssm_d[0], odd_glu_w[0], odd_glu_b[0], odd_gdn_conv_w[0],
                     odd_gdn_a_log[0], odd_gdn_dt_bias[0], odd_gdn_norm_w[0], odd_w_out[0],
                     odd_ln_mix_g[0], odd_ln_mix_b[0], odd_router_w[0], odd_moe_w1[0],
                     odd_moe_w3[0], odd_moe_w2[0], odd_ln_ffn_g[0], odd_ln_ffn_b[0])
    return out.reshape(batch, seq, d)
```

```python
import functools
import math

import jax
import jax.numpy as jnp
from jax import lax
from jax.experimental import pallas as pl
from jax.experimental.pallas import tpu as pltpu

F32 = jnp.float32
BF16 = jnp.bfloat16
HIGHEST = lax.Precision.HIGHEST

HEAD_DIM = 128
N_HEADS_SB = 8
N_HEADS_DW = 8
DW_PATTERNS = ((128, 1), (512, 4), (2048, 16))
DW_KEYS = 128
DW_TILE = 2048
SSM_WIDTH = 1024
SSM_GROUP = 16
SSM_GROUPS = 64
SSM_STATE = 64
SSM_CHUNK = 8
N_HEADS_GDN = 8
GDN_D = 128
GDN_CONV = 4
GDN_CHUNK = 64
N_EXPERTS = 8
DEPTH = 2
DEEPNORM_ALPHA = (2 * DEPTH) ** 0.25
LN_EPS = 1e-5
RMS_EPS = 1e-6

LANES = 128
SUBLANES = 8
VMEM_LIMIT = 56 * 1024 * 1024
SB_SKIP_LOG = -104.0


def _cp(*sem):
    return pltpu.CompilerParams(dimension_semantics=sem, vmem_limit_bytes=VMEM_LIMIT)


def _layer_norm(y, g, b):
    mu = jnp.mean(y, axis=-1, keepdims=True)
    yc = y - mu
    var = jnp.mean(yc * yc, axis=-1, keepdims=True)
    return yc * lax.rsqrt(var + LN_EPS) * g + b


def _softplus(x):
    return jnp.maximum(x, 0.0) + jnp.log1p(jnp.exp(-jnp.abs(x)))


def _dot(a, b, precision=None):
    return jnp.dot(a, b, preferred_element_type=F32, precision=precision)


def _dot_nt(a, b, precision=None):
    return lax.dot_general(a, b, (((1,), (1,)), ((), ())),
                           preferred_element_type=F32, precision=precision)


def _dot_tn(a, b, precision=None):
    return lax.dot_general(a, b, (((0,), (0,)), ((), ())),
                           preferred_element_type=F32, precision=precision)


def _split_hi_lo(w):
    hi = w.astype(BF16)
    return jnp.stack([hi, (w - hi.astype(F32)).astype(BF16)])


def _dot_split(x, w_ref):
    xh = x.astype(BF16)
    xl = (x - xh.astype(F32)).astype(BF16)
    wh = w_ref[0]
    return _dot(xh, wh) + _dot(xl, wh) + _dot(xh, w_ref[1])


def _mm_kernel(a_ref, b_ref, o_ref):
    o_ref[...] = _dot(a_ref[...], b_ref[...]).astype(o_ref.dtype)


def _mm_split_kernel(a_ref, b_ref, o_ref):
    o_ref[...] = _dot_split(a_ref[...], b_ref).astype(o_ref.dtype)


def _matmul(a, b, out_dtype, tm, tn, name):
    m, k = a.shape
    n = b.shape[-1]
    if b.ndim == 3:
        body, b_spec = _mm_split_kernel, pl.BlockSpec((2, k, tn), lambda i, j: (0, 0, j))
    else:
        body, b_spec = _mm_kernel, pl.BlockSpec((k, tn), lambda i, j: (0, j))
    return pl.pallas_call(
        body,
        grid=(m // tm, n // tn),
        in_specs=[pl.BlockSpec((tm, k), lambda i, j: (i, 0)), b_spec],
        out_specs=pl.BlockSpec((tm, tn), lambda i, j: (i, j)),
        out_shape=jax.ShapeDtypeStruct((m, n), out_dtype),
        compiler_params=_cp("parallel", "parallel"),
        name=name,
    )(a, b)


def _proj_ln_kernel(a0_ref, a1_ref, w0_ref, w1_ref, res_ref, g_ref, b_ref, *rest, with_router):
    if with_router:
        rw_ref, of_ref, ob_ref, rt_ref = rest
    else:
        of_ref, ob_ref = rest
    mix = _dot(a0_ref[...], w0_ref[...]) + _dot(a1_ref[...], w1_ref[...])
    xn = _layer_norm(DEEPNORM_ALPHA * res_ref[...] + mix, g_ref[...], b_ref[...])
    of_ref[...] = xn
    ob_ref[...] = xn.astype(BF16)
    if with_router:
        logits = _dot_split(xn, rw_ref)
        lane = lax.broadcasted_iota(jnp.int32, logits.shape, 1).astype(F32)
        neg = jnp.float32(-jnp.inf)
        lg = jnp.where(lane < N_EXPERTS, logits, neg)
        m1 = jnp.max(lg, axis=1, keepdims=True)
        i1 = jnp.min(jnp.where(lg == m1, lane, float(LANES)), axis=1, keepdims=True)
        lg2 = jnp.where(lane == i1, neg, lg)
        m2 = jnp.max(lg2, axis=1, keepdims=True)
        i2 = jnp.min(jnp.where(lg2 == m2, lane, float(LANES)), axis=1, keepdims=True)
        e2 = jnp.exp(m2 - m1)
        p1 = 1.0 / (1.0 + e2)
        p2 = e2 * p1
        rt = jnp.where(lane == 0.0, i1, jnp.where(lane == 1.0, i2,
                       jnp.where(lane == 2.0, p1, jnp.where(lane == 3.0, p2, 0.0))))
        rt_ref[...] = rt


def _proj_ln(a0, a1, w0, w1, res, g, b, router_w=None, tm=256):
    t, d = res.shape
    k0, k1 = a0.shape[1], a1.shape[1]
    with_router = router_w is not None
    row = lambda i: (i, 0)
    fixed = lambda i: (0, 0)
    in_specs = [pl.BlockSpec((tm, k0), row), pl.BlockSpec((tm, k1), row),
                pl.BlockSpec((k0, d), fixed), pl.BlockSpec((k1, d), fixed),
                pl.BlockSpec((tm, d), row), pl.BlockSpec((1, d), fixed), pl.BlockSpec((1, d), fixed)]
    out_specs = [pl.BlockSpec((tm, d), row), pl.BlockSpec((tm, d), row)]
    out_shape = [jax.ShapeDtypeStruct((t, d), F32), jax.ShapeDtypeStruct((t, d), BF16)]
    args = [a0, a1, w0, w1, res, g.reshape(1, d), b.reshape(1, d)]
    if with_router:
        in_specs.append(pl.BlockSpec((2, d, LANES), lambda i: (0, 0, 0)))
        out_specs.append(pl.BlockSpec((tm, LANES), row))
        out_shape.append(jax.ShapeDtypeStruct((t, LANES), F32))
        args.append(router_w)
    return pl.pallas_call(
        functools.partial(_proj_ln_kernel, with_router=with_router),
        grid=(t // tm,), in_specs=in_specs, out_specs=out_specs, out_shape=out_shape,
        compiler_params=_cp("parallel"),
        name="proj_ln_router" if with_router else "proj_ln",
    )(*args)


def _swiglu_acc(x, w1_ref, w3_ref, w2_ref, acc_ref):
    h1 = _dot(x, w1_ref[...])
    h3 = _dot(x, w3_ref[...])
    act = (h1 * jax.nn.sigmoid(h1) * h3).astype(BF16)
    acc_ref[...] += _dot(act, w2_ref[...])


def _ffn_kernel(x_ref, w1_ref, w3_ref, w2_ref, res_ref, g_ref, b_ref, of_ref, ob_ref, acc_ref, *, nj):
    j = pl.program_id(1)

    @pl.when(j == 0)
    def _():
        acc_ref[...] = jnp.zeros_like(acc_ref)

    _swiglu_acc(x_ref[...], w1_ref, w3_ref, w2_ref, acc_ref)

    @pl.when(j == nj - 1)
    def _():
        xn = _layer_norm(DEEPNORM_ALPHA * res_ref[...] + acc_ref[...], g_ref[...], b_ref[...])
        of_ref[...] = xn
        ob_ref[...] = xn.astype(BF16)


def _ffn_ln(xb, res, w1, w3, w2, g, b, tm=512, tf=512):
    t, d = res.shape
    dff = w1.shape[1]
    nj = dff // tf
    row = lambda i, j: (i, 0)
    fixed = lambda i, j: (0, 0)
    return pl.pallas_call(
        functools.partial(_ffn_kernel, nj=nj),
        grid=(t // tm, nj),
        in_specs=[pl.BlockSpec((tm, d), row),
                  pl.BlockSpec((d, tf), lambda i, j: (0, j)),
                  pl.BlockSpec((d, tf), lambda i, j: (0, j)),
                  pl.BlockSpec((tf, d), lambda i, j: (j, 0)),
                  pl.BlockSpec((tm, d), row), pl.BlockSpec((1, d), fixed), pl.BlockSpec((1, d), fixed)],
        out_specs=[pl.BlockSpec((tm, d), row), pl.BlockSpec((tm, d), row)],
        out_shape=[jax.ShapeDtypeStruct((t, d), F32), jax.ShapeDtypeStruct((t, d), BF16)],
        scratch_shapes=[pltpu.VMEM((tm, d), F32)],
        compiler_params=_cp("parallel", "arbitrary"),
        name="ffn_ln",
    )(xb, w1, w3, w2, res, g.reshape(1, d), b.reshape(1, d))


def _sb_kernel(q_ref, k_ref, v_ref, o_ref, acc_ref, carry_ref, *, tq, scale):
    i = pl.program_id(1)
    q = q_ref[...]
    row = lax.broadcasted_iota(jnp.int32, (tq, tq), 0)
    col = lax.broadcasted_iota(jnp.int32, (tq, tq), 1)
    later_sum = (row > col).astype(BF16)
    past = col < row

    def block(kb, diagonal):
        start = pl.multiple_of(kb * tq, tq)
        k = k_ref[pl.ds(start, tq), :]
        v = v_ref[pl.ds(start, tq), :]
        z = _dot_nt(q, k) * scale
        lk = -_softplus(z)
        if diagonal:
            lk = jnp.where(past, lk, 0.0)
        hi = lk.astype(BF16)
        lo = (lk - hi.astype(F32)).astype(BF16)
        later = _dot(hi, later_sum) + _dot(lo, later_sum)
        logw = z + lk + later + carry_ref[...]
        w = jnp.exp(logw)
        if diagonal:
            w = jnp.where(past, w, 0.0)
        acc_ref[...] += _dot(w.astype(BF16), v)
        carry_ref[...] += jnp.sum(lk, axis=1, keepdims=True)

    acc_ref[...] = jnp.zeros_like(acc_ref)
    carry_ref[...] = jnp.zeros_like(carry_ref)
    block(i, True)

    def cond(kb):
        return jnp.logical_and(kb >= 0, jnp.max(carry_ref[...]) > SB_SKIP_LOG)

    def body(kb):
        block(kb, False)
        return kb - 1

    lax.while_loop(cond, body, i - 1)
    o_ref[...] = acc_ref[...].astype(o_ref.dtype)


def _sb_attention(h, batch, seq, tq=256):
    nh = N_HEADS_SB
    nq = seq // tq
    return pl.pallas_call(
        functools.partial(_sb_kernel, tq=tq, scale=HEAD_DIM ** -0.5),
        grid=(batch * nh, nq),
        in_specs=[pl.BlockSpec((None, tq, HEAD_DIM), lambda bh, i: (bh // nh, i, bh % nh)),
                  pl.BlockSpec((None, seq, HEAD_DIM), lambda bh, i: (bh // nh, 0, nh + bh % nh)),
                  pl.BlockSpec((None, seq, HEAD_DIM), lambda bh, i: (bh // nh, 0, 2 * nh + bh % nh))],
        out_specs=pl.BlockSpec((None, tq, HEAD_DIM), lambda bh, i: (bh // nh, i, bh % nh)),
        out_shape=jax.ShapeDtypeStruct((batch, seq, nh * HEAD_DIM), BF16),
        scratch_shapes=[pltpu.VMEM((tq, HEAD_DIM), F32), pltpu.VMEM((tq, 1), F32)],
        compiler_params=_cp("parallel", "arbitrary"),
        name="sb_attention",
    )(h, h, h)


DW_UNROLL = 4


def _dw_kernel(q_ref, k_ref, v_ref, o_ref, m_ref, l_ref, acc_ref, *, scale):
    nk = DW_KEYS
    t0 = pl.program_id(1) * DW_TILE
    ii = lax.broadcasted_iota(jnp.int32, (nk, 2 * nk), 0)
    jj = lax.broadcasted_iota(jnp.int32, (nk, 2 * nk), 1)
    band = jnp.logical_and(jj >= ii, jj <= ii + nk)
    band_prev = jnp.logical_and(band, jj < nk)
    band_own = jnp.logical_and(band, jj >= nk)
    neg = jnp.float32(-jnp.inf)

    for p_idx, (window, dil) in enumerate(DW_PATTERNS):
        assert window // dil == nk
        span = nk * dil
        n_steps = DW_TILE // nk

        def step(s, _, dil=dil, span=span, p_idx=p_idx):
            blk = s // dil
            r = s % dil
            off = blk * span + r
            rows = pl.ds(off, nk, stride=dil)
            cur = pl.ds(t0 + off, nk, stride=dil)
            has_prev = (t0 + blk * span) > 0
            prev = pl.ds(jnp.maximum(t0 + blk * span - span, 0) + r, nk, stride=dil)
            qs = q_ref[rows, :].astype(BF16)
            keys = jnp.concatenate([k_ref[prev, :], k_ref[cur, :]], axis=0).astype(BF16)
            vals = jnp.concatenate([v_ref[prev, :], v_ref[cur, :]], axis=0).astype(BF16)
            sc = _dot_nt(qs, keys) * scale
            sc = jnp.where(jnp.logical_or(band_own, jnp.logical_and(band_prev, has_prev)), sc, neg)
            m_new = jnp.max(sc, axis=1, keepdims=True)
            p = jnp.exp(sc - m_new)
            l_new = jnp.sum(p, axis=1, keepdims=True)
            m_ref[p_idx, rows, :] = jnp.broadcast_to(m_new, (nk, HEAD_DIM))
            l_ref[p_idx, rows, :] = jnp.broadcast_to(l_new, (nk, HEAD_DIM))
            acc_ref[p_idx, rows, :] = _dot(p.astype(BF16), vals)
            return 0

        lax.fori_loop(0, n_steps, step, 0, unroll=DW_UNROLL)

    m_all = m_ref[...]
    m_tot = jnp.max(m_all, axis=0)
    w = jnp.exp(m_all - m_tot[None])
    den = jnp.sum(w * l_ref[...], axis=0)
    num = jnp.sum(w * acc_ref[...], axis=0)
    o_ref[...] = (num / den).astype(o_ref.dtype)


def _dw_attention(h, batch, seq):
    nh = N_HEADS_DW
    return pl.pallas_call(
        functools.partial(_dw_kernel, scale=HEAD_DIM ** -0.5),
        grid=(batch * nh, seq // DW_TILE),
        in_specs=[pl.BlockSpec((None, DW_TILE, HEAD_DIM), lambda bh, i: (bh // nh, i, bh % nh)),
                  pl.BlockSpec((None, seq, HEAD_DIM), lambda bh, i: (bh // nh, 0, nh + bh % nh)),
                  pl.BlockSpec((None, seq, HEAD_DIM), lambda bh, i: (bh // nh, 0, 2 * nh + bh % nh))],
        out_specs=pl.BlockSpec((None, DW_TILE, HEAD_DIM), lambda bh, i: (bh // nh, i, bh % nh)),
        out_shape=jax.ShapeDtypeStruct((batch, seq, nh * HEAD_DIM), BF16),
        scratch_shapes=[pltpu.VMEM((len(DW_PATTERNS), DW_TILE, HEAD_DIM), F32)] * 3,
        compiler_params=_cp("parallel", "arbitrary"),
        name="dw_attention",
    )(h, h, h)


def _s5_params(lam_re, lam_im, log_dt, b_re, b_im, c_re, c_im, n_scan):
    L = SSM_CHUNK
    gpl = LANES // SSM_GROUP
    nlb = SSM_GROUPS // gpl
    same = jnp.eye(gpl, dtype=F32)
    dt = jnp.exp(log_dt)[:, None]
    mag_log = lam_re * dt
    ang = lam_im * dt

    def power(n):
        n = jnp.asarray(n, F32)[..., None, None]
        mag = jnp.exp(mag_log * n)
        return mag * jnp.cos(ang * n), mag * jnp.sin(ang * n)

    lr, li = power(jnp.ones(()))
    den = lam_re * lam_re + lam_im * lam_im
    cr = ((lr - 1.0) * lam_re + li * lam_im) / den
    ci = (li * lam_re - (lr - 1.0) * lam_im) / den
    bbr = cr[..., None] * b_re - ci[..., None] * b_im
    bbi = cr[..., None] * b_im + ci[..., None] * b_re

    pr, pi = power(jnp.arange(L + 1, dtype=F32))
    mr = pr[:L, :, :, None] * bbr - pi[:L, :, :, None] * bbi
    mi = pr[:L, :, :, None] * bbi + pi[:L, :, :, None] * bbr
    kk = (jnp.einsum('ghp,tgpk->tghk', c_re, mr, precision=HIGHEST)
          - jnp.einsum('ghp,tgpk->tghk', c_im, mi, precision=HIGHEST))
    jj = jnp.arange(L)[:, None]
    ii = jnp.arange(L)[None, :]
    lag = ii - jj
    toe = jnp.where((lag >= 0)[:, :, None, None, None], kk[jnp.clip(lag, 0, L - 1)], 0.0)
    toe = toe.reshape(L, L, nlb, gpl, SSM_GROUP, SSM_GROUP)
    w_toe = jnp.einsum('jilghk,gd->ljgkidh', toe, same).reshape(nlb, L * LANES, L * LANES)
    qr = pr[L - 1 - jnp.arange(L)]
    qi = pi[L - 1 - jnp.arange(L)]
    inr = (qr[..., None] * bbr - qi[..., None] * bbi).reshape(L, nlb, gpl, SSM_STATE, SSM_GROUP)
    ini = (qr[..., None] * bbi + qi[..., None] * bbr).reshape(L, nlb, gpl, SSM_STATE, SSM_GROUP)
    half = gpl * SSM_STATE
    w_in = jnp.concatenate(
        [jnp.einsum('jlgpk,gd->ljgkdp', part, same).reshape(nlb, L * LANES, half) for part in (inr, ini)],
        axis=2)
    orr = c_re[None] * pr[1:, :, None, :] - c_im[None] * pi[1:, :, None, :]
    oii = c_re[None] * pi[1:, :, None, :] + c_im[None] * pr[1:, :, None, :]
    w_out = jnp.concatenate(
        [jnp.einsum('ilghp,gd->lgpidh', part.reshape(L, nlb, gpl, SSM_GROUP, SSM_STATE), same)
         .reshape(nlb, half, L * LANES) for part in (orr, -oii)], axis=1)
    ar, ai = power(float(L) * (2.0 ** jnp.arange(n_scan, dtype=F32)))
    ar = ar.reshape(n_scan, nlb, half).transpose(1, 0, 2)
    ai = ai.reshape(n_scan, nlb, half).transpose(1, 0, 2)
    return w_toe.astype(BF16), w_in.astype(BF16), w_out.astype(BF16), ar, ai


def _s5_kernel(x_ref, wt_ref, wi_ref, wo_ref, ar_ref, ai_ref, y_ref, sr_ref, si_ref, *, n_scan, m):
    L = SSM_CHUNK

    @pl.when(pl.program_id(2) == 0)
    def _():
        sr_ref[...] = jnp.zeros_like(sr_ref)
        si_ref[...] = jnp.zeros_like(si_ref)

    xc = jnp.concatenate([x_ref[pl.ds(j, m, stride=L), :].astype(BF16) for j in range(L)], axis=1)
    yc = _dot(xc, wt_ref[...])
    z = _dot(xc, wi_ref[...])
    half = z.shape[1] // 2
    zr, zi = z[:, :half], z[:, half:]
    row = lax.broadcasted_iota(jnp.int32, (m, half), 0)
    pr, pi = sr_ref[0:1, :], si_ref[0:1, :]
    a1r, a1i = ar_ref[0:1, :], ai_ref[0:1, :]
    first = row == 0
    zr, zi = (zr + jnp.where(first, a1r * pr - a1i * pi, 0.0),
              zi + jnp.where(first, a1r * pi + a1i * pr, 0.0))
    for k in range(n_scan):
        s = 1 << k
        ar = ar_ref[k:k + 1, :]
        ai = ai_ref[k:k + 1, :]
        keep = row >= s
        tr = jnp.where(keep, pltpu.roll(zr, s, 0), 0.0)
        ti = jnp.where(keep, pltpu.roll(zi, s, 0), 0.0)
        zr, zi = zr + ar * tr - ai * ti, zi + ar * ti + ai * tr
    keep = row >= 1
    s_in = jnp.concatenate([jnp.where(keep, pltpu.roll(zr, 1, 0), pr),
                            jnp.where(keep, pltpu.roll(zi, 1, 0), pi)], axis=1)
    sr_ref[...] = jnp.broadcast_to(zr[m - 1:m, :], sr_ref.shape)
    si_ref[...] = jnp.broadcast_to(zi[m - 1:m, :], si_ref.shape)
    yc = yc + _dot(s_in.astype(BF16), wo_ref[...])
    for i in range(L):
        y_ref[pl.ds(i, m, stride=L), :] = yc[:, i * LANES:(i + 1) * LANES]


def _s5_scan(h1, lam_re, lam_im, log_dt, b_re, b_im, c_re, c_im, batch, seq, rows=2048):
    L = SSM_CHUNK
    rows = min(rows, seq)
    m = rows // L
    nt = seq // rows
    n_scan = max(1, math.ceil(math.log2(m)))
    w_toe, w_in, w_out, ar, ai = _s5_params(lam_re, lam_im, log_dt, b_re, b_im, c_re, c_im, n_scan)
    nlb, half = ar.shape[0], ar.shape[2]
    per_lb = lambda *blk: pl.BlockSpec((None,) + blk, lambda l, b, i: (l, 0, 0))
    tile = pl.BlockSpec((rows, LANES), lambda l, b, i: (b * nt + i, l))
    return pl.pallas_call(
        functools.partial(_s5_kernel, n_scan=n_scan, m=m),
        grid=(nlb, batch, nt),
        in_specs=[tile, per_lb(L * LANES, L * LANES), per_lb(L * LANES, 2 * half),
                  per_lb(2 * half, L * LANES), per_lb(n_scan, half), per_lb(n_scan, half)],
        out_specs=tile,
        out_shape=jax.ShapeDtypeStruct((batch * seq, SSM_WIDTH), F32),
        scratch_shapes=[pltpu.VMEM((SUBLANES, half), F32)] * 2,
        compiler_params=_cp("parallel", "parallel", "arbitrary"),
        name="s5_scan",
    )(h1, w_toe, w_in, w_out, ar, ai)


def _s5_glu_kernel(y_ref, u_ref, d_ref, w_ref, b_ref, o_ref):
    y = y_ref[...] + d_ref[...] * u_ref[...]
    c = math.sqrt(2.0 / math.pi)
    z = 0.5 * y * (1.0 + jnp.tanh(c * (y + 0.044715 * (y * y * y))))
    gate = jax.nn.sigmoid(_dot(z.astype(BF16), w_ref[...]) + b_ref[...])
    o_ref[...] = (z * gate).astype(o_ref.dtype)


def _s5_glu(y, h1, d_skip, glu_w, glu_b, tm=512):
    t, w = y.shape
    row = lambda i: (i, 0)
    fixed = lambda i: (0, 0)
    return pl.pallas_call(
        _s5_glu_kernel,
        grid=(t // tm,),
        in_specs=[pl.BlockSpec((tm, w), row), pl.BlockSpec((tm, w), row),
                  pl.BlockSpec((1, w), fixed), pl.BlockSpec((w, w), fixed), pl.BlockSpec((1, w), fixed)],
        out_specs=pl.BlockSpec((tm, w), row),
        out_shape=jax.ShapeDtypeStruct((t, w), BF16),
        compiler_params=_cp("parallel"),
        name="s5_glu",
    )(y, h1, d_skip.reshape(1, w), glu_w, glu_b.reshape(1, w))


def _bdot(a, b):
    return _dot(a.astype(BF16), b.astype(BF16))


def _bdot_nt(a, b):
    return _dot_nt(a.astype(BF16), b.astype(BF16))


def _exact_dot(sel, x):
    sel = sel.astype(BF16)
    x1 = x.astype(BF16)
    r1 = x - x1.astype(F32)
    x2 = r1.astype(BF16)
    x3 = (r1 - x2.astype(F32)).astype(BF16)
    return _dot(sel, x1) + _dot(sel, x2) + _dot(sel, x3)


def _gdn_kernel(q_ref, k_ref, v_ref, gate_ref, ba_ref, wq_ref, wk_ref, wv_ref, alog_ref, dtb_ref,
                nw_ref, o_ref, state_ref, hq_ref, hk_ref, hv_ref, gct_ref, *, lt, nh, hp):
    d = GDN_D
    c = lt
    head0 = (pl.program_id(0) % (nh // hp)) * hp

    @pl.when(pl.program_id(1) == 0)
    def _():
        state_ref[...] = jnp.zeros_like(state_ref)
        hq_ref[...] = jnp.zeros_like(hq_ref)
        hk_ref[...] = jnp.zeros_like(hk_ref)
        hv_ref[...] = jnp.zeros_like(hv_ref)

    sub = lax.broadcasted_iota(jnp.int32, (SUBLANES, hp * d), 0)

    def conv_silu(x_ref, w_ref, halo_ref):
        x = x_ref[...]
        w = w_ref[...]
        halo = halo_ref[...]
        y = x * w[GDN_CONV - 1:GDN_CONV, :]
        for back in range(1, GDN_CONV):
            rolled = pltpu.roll(x, back, 0)
            head_rows = jnp.where(sub < back, pltpu.roll(halo, back, 0), rolled[:SUBLANES, :])
            shifted = jnp.concatenate([head_rows, rolled[SUBLANES:, :]], axis=0)
            y = y + shifted * w[GDN_CONV - 1 - back:GDN_CONV - back, :]
        halo_ref[...] = x[lt - SUBLANES:, :]
        return y * jax.nn.sigmoid(y)

    def l2norm(t):
        return t * lax.rsqrt(jnp.sum(t * t, axis=-1, keepdims=True) + RMS_EPS)

    q_cs = conv_silu(q_ref, wq_ref, hq_ref)
    k_cs = conv_silu(k_ref, wk_ref, hk_ref)
    v_cs = conv_silu(v_ref, wv_ref, hv_ref)

    ba = ba_ref[...]
    lane = lax.broadcasted_iota(jnp.int32, ba.shape, 1)
    sig_ba = jax.nn.sigmoid(ba)
    g_full = -jnp.exp(alog_ref[...]) * _softplus(ba + dtb_ref[...])
    gc_full = g_full
    rows_i = lax.broadcasted_iota(jnp.int32, gc_full.shape, 0)
    step = 1
    while step < c:
        gc_full = gc_full + jnp.where(rows_i >= step, pltpu.roll(gc_full, step, 0), 0.0)
        step *= 2
    gct_ref[...] = gc_full.T

    ri = lax.broadcasted_iota(jnp.int32, (c, c), 0)
    ci = lax.broadcasted_iota(jnp.int32, (c, c), 1)
    tri = ri >= ci
    strict = ri > ci
    eye = (ri == ci).astype(F32)
    in16 = (ri // 16) == (ci // 16)
    merges = []
    width = 16
    while width < c:
        inner = (ri // width) == (ci // width)
        outer = (ri // (2 * width)) == (ci // (2 * width))
        merges.append(jnp.logical_and(outer, jnp.logical_not(inner)))
        width *= 2

    for hh in range(hp):
        cols = slice(hh * d, (hh + 1) * d)
        q = l2norm(q_cs[:, cols]) * (d ** -0.5)
        k = l2norm(k_cs[:, cols])
        v = v_cs[:, cols]
        beta = jnp.sum(jnp.where(lane == head0 + hh, sig_ba, 0.0), axis=1, keepdims=True)
        gc = jnp.sum(jnp.where(lane == nh + head0 + hh, gc_full, 0.0), axis=1, keepdims=True)
        gc_row = gct_ref[pl.ds(nh + head0 + hh, 1), :]
        gc_last = gc_row[:, c - 1:c]
        decay = jnp.exp(jnp.where(tri, gc - gc_row, -jnp.inf))
        kb = k * beta
        kq_kt = _bdot_nt(jnp.concatenate([kb, q], axis=0), k)
        lower = jnp.where(strict, kq_kt[:c] * decay, 0.0)
        neg = jnp.where(in16, -lower, 0.0)
        inv = eye + neg
        pw = neg
        for _ in range(3):
            pw = _bdot(pw, pw)
            inv = inv + _bdot(inv, pw)
        for off in merges:
            inv = inv - _bdot(_bdot(inv, jnp.where(off, lower, 0.0)), inv)
        e_gc = jnp.exp(gc)
        uw = _bdot(inv, jnp.concatenate([v * beta, kb * e_gc], axis=1))
        u_val, w_key = uw[:, :d], uw[:, d:]
        intra = kq_kt[c:] * decay
        q_dec = q * e_gc
        k_dec = k * jnp.exp(gc_last - gc)
        state = state_ref[hh]
        ws_qs = _bdot(jnp.concatenate([w_key, q_dec], axis=0), state)
        v_new = u_val - ws_qs[:c]
        o = ws_qs[c:] + _bdot(intra, v_new)
        state_ref[hh] = (state * jnp.exp(gc_last)
                         + _dot_tn(k_dec.astype(BF16), v_new.astype(BF16)))
        o = o * lax.rsqrt(jnp.mean(o * o, axis=-1, keepdims=True) + RMS_EPS) * nw_ref[...]
        gt = gate_ref[:, cols]
        o_ref[:, cols] = (o * (gt * jax.nn.sigmoid(gt))).astype(o_ref.dtype)


def _gated_deltanet(h1, ba, conv_w, a_log_pad, dt_bias_pad, norm_w, batch, seq, lt=256, hp=2):
    nh = N_HEADS_GDN
    d = GDN_D
    t = batch * seq
    nt = seq // lt
    ng = nh // hp
    wide = hp * d
    base = SSM_WIDTH // wide
    rows = lambda off: (lambda bg, i: ((bg // ng) * nt + i, off + bg % ng))
    cw = lambda off: (lambda bg, i: (0, off + bg % ng))
    fixed = lambda bg, i: (0, 0)
    return pl.pallas_call(
        functools.partial(_gdn_kernel, lt=lt, nh=nh, hp=hp),
        grid=(batch * ng, nt),
        in_specs=[pl.BlockSpec((lt, wide), rows(base)), pl.BlockSpec((lt, wide), rows(base + ng)),
                  pl.BlockSpec((lt, wide), rows(base + 2 * ng)), pl.BlockSpec((lt, wide), rows(base + 3 * ng)),
                  pl.BlockSpec((lt, LANES), lambda bg, i: ((bg // ng) * nt + i, 0)),
                  pl.BlockSpec((GDN_CONV, wide), cw(0)), pl.BlockSpec((GDN_CONV, wide), cw(ng)),
                  pl.BlockSpec((GDN_CONV, wide), cw(2 * ng)),
                  pl.BlockSpec((1, LANES), fixed), pl.BlockSpec((1, LANES), fixed),
                  pl.BlockSpec((1, d), fixed)],
        out_specs=pl.BlockSpec((lt, wide), rows(0)),
        out_shape=jax.ShapeDtypeStruct((t, nh * d), BF16),
        scratch_shapes=[pltpu.VMEM((hp, d, d), F32)] + [pltpu.VMEM((SUBLANES, wide), F32)] * 3
                       + [pltpu.VMEM((LANES, lt), F32)],
        compiler_params=_cp("parallel", "arbitrary"),
        name="gated_deltanet",
    )(h1, h1, h1, h1, ba, conv_w, conv_w, conv_w, a_log_pad, dt_bias_pad, norm_w.reshape(1, d))


DMA_ISSUE_UNROLL = 8


def _row_gather_kernel(src_ref, x_hbm, o_ref, buf_ref, sem, *, tm):
    i = pl.program_id(0)

    def issue(step, slot):
        def body(r, _):
            tok = src_ref[step * tm + r]
            pltpu.make_async_copy(x_hbm.at[pl.ds(tok, 1), :], buf_ref.at[slot, pl.ds(r, 1), :],
                                  sem.at[slot]).start()
            return 0
        lax.fori_loop(0, tm, body, 0, unroll=DMA_ISSUE_UNROLL)

    @pl.when(i == 0)
    def _():
        issue(0, 0)

    @pl.when(i + 1 < pl.num_programs(0))
    def _():
        issue(i + 1, (i + 1) % 2)

    slot = i % 2
    pltpu.make_async_copy(x_hbm.at[pl.ds(0, tm), :], buf_ref.at[slot], sem.at[slot]).wait()
    o_ref[...] = buf_ref[slot].astype(o_ref.dtype)


def _row_gather(x, src, tm=512):
    p = src.shape[0]
    d = x.shape[1]
    return pl.pallas_call(
        functools.partial(_row_gather_kernel, tm=tm),
        grid_spec=pltpu.PrefetchScalarGridSpec(
            num_scalar_prefetch=1, grid=(p // tm,),
            in_specs=[pl.BlockSpec(memory_space=pl.ANY)],
            out_specs=pl.BlockSpec((tm, d), lambda i, src: (i, 0)),
            scratch_shapes=[pltpu.VMEM((2, tm, d), x.dtype), pltpu.SemaphoreType.DMA((2,))]),
        out_shape=jax.ShapeDtypeStruct((p, d), BF16),
        compiler_params=_cp("arbitrary"),
        name="moe_gather",
    )(src, x)


def _moe_kernel(te_ref, nv_ref, x_ref, w1_ref, w3_ref, w2_ref, gate_ref, o_ref, acc_ref, *, nj):
    i = pl.program_id(0)
    j = pl.program_id(1)
    valid = i < nv_ref[0]

    @pl.when(j == 0)
    def _():
        acc_ref[...] = jnp.zeros_like(acc_ref)

    @pl.when(valid)
    def _():
        _swiglu_acc(x_ref[...], w1_ref, w3_ref, w2_ref, acc_ref)

    @pl.when(j == nj - 1)
    def _():
        o_ref[...] = acc_ref[...] * gate_ref[...]


def _moe_ffn(xs, gates, tile_expert, n_valid, w1, w3, w2, tm, tf=512):
    p, d = xs.shape
    dff = w1.shape[2]
    nj = dff // tf
    jsel = lambda i, j, nv: jnp.where(i < nv[0], j, nj - 1)
    return pl.pallas_call(
        functools.partial(_moe_kernel, nj=nj),
        grid_spec=pltpu.PrefetchScalarGridSpec(
            num_scalar_prefetch=2, grid=(p // tm, nj),
            in_specs=[pl.BlockSpec((tm, d), lambda i, j, te, nv: (i, 0)),
                      pl.BlockSpec((None, d, tf), lambda i, j, te, nv: (te[i], 0, jsel(i, j, nv))),
                      pl.BlockSpec((None, d, tf), lambda i, j, te, nv: (te[i], 0, jsel(i, j, nv))),
                      pl.BlockSpec((None, tf, d), lambda i, j, te, nv: (te[i], jsel(i, j, nv), 0)),
                      pl.BlockSpec((tm, 1), lambda i, j, te, nv: (i, 0))],
            out_specs=pl.BlockSpec((tm, d), lambda i, j, te, nv: (i, 0)),
            scratch_shapes=[pltpu.VMEM((tm, d), F32)]),
        out_shape=jax.ShapeDtypeStruct((p, d), F32),
        compiler_params=_cp("parallel", "arbitrary"),
        name="moe_ffn",
    )(tile_expert, n_valid, xs, w1, w3, w2, gates)


def _combine_kernel(pos_ref, ys_hbm, res_ref, g_ref, b_ref, o_ref, buf_ref, sem, *, tm, t):
    i = pl.program_id(0)

    def issue(step, slot):
        def body(r, _):
            for choice in range(2):
                p = pos_ref[choice * t + step * tm + r]
                pltpu.make_async_copy(ys_hbm.at[pl.ds(p, 1), :], buf_ref.at[slot, choice, pl.ds(r, 1), :],
                                      sem.at[slot, choice]).start()
            return 0
        lax.fori_loop(0, tm, body, 0, unroll=DMA_ISSUE_UNROLL)

    @pl.when(i == 0)
    def _():
        issue(0, 0)

    @pl.when(i + 1 < pl.num_programs(0))
    def _():
        issue(i + 1, (i + 1) % 2)

    slot = i % 2
    for choice in range(2):
        pltpu.make_async_copy(ys_hbm.at[pl.ds(0, tm), :], buf_ref.at[slot, choice], sem.at[slot, choice]).wait()
    y = buf_ref[slot, 0] + buf_ref[slot, 1]
    o_ref[...] = _layer_norm(DEEPNORM_ALPHA * res_ref[...] + y, g_ref[...], b_ref[...])


def _moe_combine_ln(ys, pos, res, g, b, tm=256):
    t, d = res.shape
    return pl.pallas_call(
        functools.partial(_combine_kernel, tm=tm, t=t),
        grid_spec=pltpu.PrefetchScalarGridSpec(
            num_scalar_prefetch=1, grid=(t // tm,),
            in_specs=[pl.BlockSpec(memory_space=pl.ANY),
                      pl.BlockSpec((tm, d), lambda i, pos: (i, 0)),
                      pl.BlockSpec((1, d), lambda i, pos: (0, 0)),
                      pl.BlockSpec((1, d), lambda i, pos: (0, 0))],
            out_specs=pl.BlockSpec((tm, d), lambda i, pos: (i, 0)),
            scratch_shapes=[pltpu.VMEM((2, 2, tm, d), F32), pltpu.SemaphoreType.DMA((2, 2))]),
        out_shape=jax.ShapeDtypeStruct((t, d), F32),
        compiler_params=_cp("arbitrary"),
        name="moe_combine_ln",
    )(pos, ys, res, g.reshape(1, d), b.reshape(1, d))


def _moe_routing(route, tm):
    t = route.shape[0]
    e = N_EXPERTS
    idx = route[:, 0:2].astype(jnp.int32)
    wts = route[:, 2:4]
    flat_e = idx.T.reshape(-1)
    onehot = (flat_e[:, None] == jnp.arange(e, dtype=jnp.int32)[None, :]).astype(jnp.int32)
    rank = jnp.cumsum(onehot, axis=0) - onehot
    counts = jnp.sum(onehot, axis=0)
    tiles = (counts + tm - 1) // tm
    tile_end = jnp.cumsum(tiles)
    start = (tile_end - tiles) * tm
    pos = jnp.sum(onehot * (start[None, :] + rank), axis=1)
    n_slots = 2 * t + e * tm
    n_tiles = n_slots // tm
    owner = jnp.full((n_slots,), -1, jnp.int32).at[pos].set(jnp.arange(2 * t, dtype=jnp.int32))
    used = owner >= 0
    src = jnp.where(used, owner % t, 0)
    gates = jnp.where(used, wts.T.reshape(-1)[jnp.maximum(owner, 0)], 0.0)
    n_valid = tile_end[-1]
    tile_ids = jnp.arange(n_tiles, dtype=jnp.int32)
    tile_expert = jnp.sum((tile_ids[:, None] >= tile_end[None, :]).astype(jnp.int32), axis=1)
    last_expert = jnp.sum((n_valid - 1 >= tile_end).astype(jnp.int32))
    tile_expert = jnp.where(tile_ids < n_valid, tile_expert, last_expert).astype(jnp.int32)
    return src, gates.reshape(n_slots, 1), pos.astype(jnp.int32), tile_expert, n_valid.reshape(1).astype(jnp.int32)


def _even_layer(x, xb, batch, seq, w_in, w_out, ln_mix_g, ln_mix_b, w1, w3, w2, ln_ffn_g, ln_ffn_b):
    t, d = x.shape
    w_sb = N_HEADS_SB * HEAD_DIM
    w_in_b = w_in.astype(BF16)
    ha = _matmul(xb, w_in_b[:, :3 * w_sb], BF16, 512, 512, "in_proj_sb")
    hb = _matmul(xb, w_in_b[:, 3 * w_sb:], F32, 512, 512, "in_proj_dw")
    oa = _sb_attention(ha.reshape(batch, seq, -1), batch, seq).reshape(t, -1)
    ob = _dw_attention(hb.reshape(batch, seq, -1), batch, seq).reshape(t, -1)
    w_out_b = w_out.astype(BF16)
    x, xb = _proj_ln(oa, ob, w_out_b[:w_sb], w_out_b[w_sb:], x, ln_mix_g, ln_mix_b)
    return _ffn_ln(xb, x, w1.astype(BF16), w3.astype(BF16), w2.astype(BF16), ln_ffn_g, ln_ffn_b)


def _odd_layer(x, xb, batch, seq, w_in, lam_re, lam_im, log_dt, b_re, b_im, c_re, c_im, d_skip,
               glu_w, glu_b, conv_w, a_log, dt_bias, norm_w, w_out, ln_mix_g, ln_mix_b,
               router_w, w1, w3, w2, ln_ffn_g, ln_ffn_b):
    t, d = x.shape
    nh = N_HEADS_GDN
    wide = SSM_WIDTH + 4 * nh * GDN_D
    h1 = _matmul(xb, w_in[:, :wide].astype(BF16), F32, 512, 512, "in_proj_odd")
    w_small = jnp.pad(w_in[:, wide:], ((0, 0), (0, LANES - 2 * nh)))
    ba = _matmul(x, _split_hi_lo(w_small), F32, 512, LANES, "in_proj_gates")

    y = _s5_scan(h1, lam_re, lam_im, log_dt, b_re, b_im, c_re, c_im, batch, seq)
    oc = _s5_glu(y, h1, d_skip, glu_w.astype(BF16), glu_b)

    pad_hi = LANES - 2 * nh
    a_log_pad = jnp.pad(a_log, (nh, pad_hi)).reshape(1, LANES)
    dt_bias_pad = jnp.pad(dt_bias, (nh, pad_hi)).reshape(1, LANES)
    od = _gated_deltanet(h1, ba, conv_w, a_log_pad, dt_bias_pad, norm_w, batch, seq)

    w_out_b = w_out.astype(BF16)
    rw = _split_hi_lo(jnp.pad(router_w, ((0, 0), (0, LANES - N_EXPERTS))))
    x, xb, route = _proj_ln(oc, od, w_out_b[:SSM_WIDTH], w_out_b[SSM_WIDTH:], x, ln_mix_g, ln_mix_b,
                            router_w=rw)

    tm = 512
    src, gates, pos, tile_expert, n_valid = _moe_routing(route, tm)
    xs = _row_gather(x, src)
    ys = _moe_ffn(xs, gates, tile_expert, n_valid, w1.astype(BF16), w3.astype(BF16), w2.astype(BF16), tm)
    return _moe_combine_ln(ys, pos, x, ln_ffn_g, ln_ffn_b)


def kernel(x, even_w_in, even_w_out, even_ln_mix_g, even_ln_mix_b, even_ffn_w1, even_ffn_w3, even_ffn_w2, even_ln_ffn_g, even_ln_ffn_b, odd_w_in, odd_ssm_lam_re, odd_ssm_lam_im, odd_ssm_log_dt, odd_ssm_b_re, odd_ssm_b_im, odd_ssm_c_re, odd_ssm_c_im, odd_ssm_d, odd_glu_w, odd_glu_b, odd_gdn_conv_w, odd_gdn_a_log, odd_gdn_dt_bias, odd_gdn_norm_w, odd_w_out, odd_ln_mix_g, odd_ln_mix_b, odd_router_w, odd_moe_w1, odd_moe_w3, odd_moe_w2, odd_ln_ffn_g, odd_ln_ffn_b):
    batch, seq, d = x.shape
    xf = x.reshape(batch * seq, d)
    xf, xb = _even_layer(xf, xf.astype(BF16), batch, seq, even_w_in[0], even_w_out[0],
                         even_ln_mix_g[0], even_ln_mix_b[0], even_ffn_w1[0], even_ffn_w3[0],
                         even_ffn_w2[0], even_ln_ffn_g[0], even_ln_ffn_b[0])
    out = _odd_layer(xf, xb, batch, seq, odd_w_in[0], odd_ssm_lam_re[0], odd_ssm_lam_im[0],
                     odd_ssm_log_dt[0], odd_ssm_b_re[0], odd_ssm_b_im[0], odd_ssm_c_re[0],
                     odd_ssm_c_im[0], odd_ssm_d[0], odd_glu_w[0], odd_glu_b[0], odd_gdn_conv_w[0],
                     odd_gdn_a_log[0], odd_gdn_dt_bias[0], odd_gdn_norm_w[0], odd_w_out[0],
                     odd_ln_mix_g[0], odd_ln_mix_b[0], odd_router_w[0], odd_moe_w1[0],
                     odd_moe_w3[0], odd_moe_w2[0], odd_ln_ffn_g[0], odd_ln_ffn_b[0])
    return out.reshape(batch, seq, d)
```

```python
import functools
import math

import jax
import jax.numpy as jnp
from jax import lax
from jax.experimental import pallas as pl
from jax.experimental.pallas import tpu as pltpu

F32 = jnp.float32
BF16 = jnp.bfloat16
HIGHEST = lax.Precision.HIGHEST

HEAD_DIM = 128
N_HEADS_SB = 8
N_HEADS_DW = 8
DW_PATTERNS = ((128, 1), (512, 4), (2048, 16))
DW_KEYS = 128
DW_TILE = 2048
SSM_WIDTH = 1024
SSM_GROUP = 16
SSM_GROUPS = 64
SSM_STATE = 64
SSM_CHUNK = 8
N_HEADS_GDN = 8
GDN_D = 128
GDN_CONV = 4
GDN_CHUNK = 64
N_EXPERTS = 8
DEPTH = 2
DEEPNORM_ALPHA = (2 * DEPTH) ** 0.25
LN_EPS = 1e-5
RMS_EPS = 1e-6

LANES = 128
SUBLANES = 8
VMEM_LIMIT = 56 * 1024 * 1024
SB_SKIP_LOG = -104.0


def _cp(*sem):
    return pltpu.CompilerParams(dimension_semantics=sem, vmem_limit_bytes=VMEM_LIMIT)


def _layer_norm(y, g, b):
    mu = jnp.mean(y, axis=-1, keepdims=True)
    yc = y - mu
    var = jnp.mean(yc * yc, axis=-1, keepdims=True)
    return yc * lax.rsqrt(var + LN_EPS) * g + b


def _store_row_slabs(ref, val):
    rows, d = val.shape
    n = d // LANES
    for c in range(n):
        ref[pl.ds(c, rows, stride=n), :] = val[:, c * LANES:(c + 1) * LANES]


def _load_row_slabs(ref, rows, n):
    return jnp.concatenate([ref[pl.ds(c, rows, stride=n), :] for c in range(n)], axis=1)


def _softplus(x):
    return jnp.maximum(x, 0.0) + jnp.log1p(jnp.exp(-jnp.abs(x)))


def _dot(a, b, precision=None):
    return jnp.dot(a, b, preferred_element_type=F32, precision=precision)


def _dot_nt(a, b, precision=None):
    return lax.dot_general(a, b, (((1,), (1,)), ((), ())),
                           preferred_element_type=F32, precision=precision)


def _dot_tn(a, b, precision=None):
    return lax.dot_general(a, b, (((0,), (0,)), ((), ())),
                           preferred_element_type=F32, precision=precision)


def _split_hi_lo(w):
    hi = w.astype(BF16)
    return jnp.stack([hi, (w - hi.astype(F32)).astype(BF16)])


def _dot_split(x, w_ref):
    xh = x.astype(BF16)
    xl = (x - xh.astype(F32)).astype(BF16)
    wh = w_ref[0]
    return _dot(xh, wh) + _dot(xl, wh) + _dot(xh, w_ref[1])


def _mm_kernel(a_ref, b_ref, o_ref):
    o_ref[...] = _dot(a_ref[...], b_ref[...]).astype(o_ref.dtype)


def _mm_split_kernel(a_ref, b_ref, o_ref):
    o_ref[...] = _dot_split(a_ref[...], b_ref).astype(o_ref.dtype)


def _matmul(a, b, out_dtype, tm, tn, name):
    m, k = a.shape
    n = b.shape[-1]
    if b.ndim == 3:
        body, b_spec = _mm_split_kernel, pl.BlockSpec((2, k, tn), lambda i, j: (0, 0, j))
    else:
        body, b_spec = _mm_kernel, pl.BlockSpec((k, tn), lambda i, j: (0, j))
    return pl.pallas_call(
        body,
        grid=(m // tm, n // tn),
        in_specs=[pl.BlockSpec((tm, k), lambda i, j: (i, 0)), b_spec],
        out_specs=pl.BlockSpec((tm, tn), lambda i, j: (i, j)),
        out_shape=jax.ShapeDtypeStruct((m, n), out_dtype),
        compiler_params=_cp("parallel", "parallel"),
        name=name,
    )(a, b)


def _proj_ln_kernel(a0_ref, a1_ref, w0_ref, w1_ref, res_ref, g_ref, b_ref, *rest, with_router):
    if with_router:
        rw_ref, of_ref, slab_ref, rt_ref = rest
    else:
        of_ref, ob_ref = rest
    mix = _dot(a0_ref[...], w0_ref[...]) + _dot(a1_ref[...], w1_ref[...])
    xn = _layer_norm(DEEPNORM_ALPHA * res_ref[...] + mix, g_ref[...], b_ref[...])
    of_ref[...] = xn
    if with_router:
        _store_row_slabs(slab_ref, xn)
    else:
        ob_ref[...] = xn.astype(BF16)
    if with_router:
        logits = _dot_split(xn, rw_ref)
        lane = lax.broadcasted_iota(jnp.int32, logits.shape, 1).astype(F32)
        neg = jnp.float32(-jnp.inf)
        lg = jnp.where(lane < N_EXPERTS, logits, neg)
        m1 = jnp.max(lg, axis=1, keepdims=True)
        i1 = jnp.min(jnp.where(lg == m1, lane, float(LANES)), axis=1, keepdims=True)
        lg2 = jnp.where(lane == i1, neg, lg)
        m2 = jnp.max(lg2, axis=1, keepdims=True)
        i2 = jnp.min(jnp.where(lg2 == m2, lane, float(LANES)), axis=1, keepdims=True)
        e2 = jnp.exp(m2 - m1)
        p1 = 1.0 / (1.0 + e2)
        p2 = e2 * p1
        rt = jnp.where(lane == 0.0, i1, jnp.where(lane == 1.0, i2,
                       jnp.where(lane == 2.0, p1, jnp.where(lane == 3.0, p2, 0.0))))
        rt_ref[...] = rt


def _proj_ln(a0, a1, w0, w1, res, g, b, router_w=None, tm=256):
    t, d = res.shape
    k0, k1 = a0.shape[1], a1.shape[1]
    with_router = router_w is not None
    row = lambda i: (i, 0)
    fixed = lambda i: (0, 0)
    in_specs = [pl.BlockSpec((tm, k0), row), pl.BlockSpec((tm, k1), row),
                pl.BlockSpec((k0, d), fixed), pl.BlockSpec((k1, d), fixed),
                pl.BlockSpec((tm, d), row), pl.BlockSpec((1, d), fixed), pl.BlockSpec((1, d), fixed)]
    args = [a0, a1, w0, w1, res, g.reshape(1, d), b.reshape(1, d)]
    if with_router:
        n = d // LANES
        in_specs.append(pl.BlockSpec((2, d, LANES), lambda i: (0, 0, 0)))
        out_specs = [pl.BlockSpec((tm, d), row), pl.BlockSpec((tm * n, LANES), row),
                     pl.BlockSpec((tm, LANES), row)]
        out_shape = [jax.ShapeDtypeStruct((t, d), F32), jax.ShapeDtypeStruct((t * n, LANES), F32),
                     jax.ShapeDtypeStruct((t, LANES), F32)]
        args.append(router_w)
    else:
        out_specs = [pl.BlockSpec((tm, d), row), pl.BlockSpec((tm, d), row)]
        out_shape = [jax.ShapeDtypeStruct((t, d), F32), jax.ShapeDtypeStruct((t, d), BF16)]
    return pl.pallas_call(
        functools.partial(_proj_ln_kernel, with_router=with_router),
        grid=(t // tm,), in_specs=in_specs, out_specs=out_specs, out_shape=out_shape,
        compiler_params=_cp("parallel"),
        name="proj_ln_router" if with_router else "proj_ln",
    )(*args)


def _swiglu_acc(x, w1_ref, w3_ref, w2_ref, acc_ref):
    h1 = _dot(x, w1_ref[...])
    h3 = _dot(x, w3_ref[...])
    act = (h1 * jax.nn.sigmoid(h1) * h3).astype(BF16)
    acc_ref[...] += _dot(act, w2_ref[...])


def _ffn_kernel(x_ref, w1_ref, w3_ref, w2_ref, res_ref, g_ref, b_ref, of_ref, ob_ref, acc_ref, *, nj):
    j = pl.program_id(1)

    @pl.when(j == 0)
    def _():
        acc_ref[...] = jnp.zeros_like(acc_ref)

    _swiglu_acc(x_ref[...], w1_ref, w3_ref, w2_ref, acc_ref)

    @pl.when(j == nj - 1)
    def _():
        xn = _layer_norm(DEEPNORM_ALPHA * res_ref[...] + acc_ref[...], g_ref[...], b_ref[...])
        of_ref[...] = xn
        ob_ref[...] = xn.astype(BF16)


def _ffn_ln(xb, res, w1, w3, w2, g, b, tm=512, tf=512):
    t, d = res.shape
    dff = w1.shape[1]
    nj = dff // tf
    row = lambda i, j: (i, 0)
    fixed = lambda i, j: (0, 0)
    return pl.pallas_call(
        functools.partial(_ffn_kernel, nj=nj),
        grid=(t // tm, nj),
        in_specs=[pl.BlockSpec((tm, d), row),
                  pl.BlockSpec((d, tf), lambda i, j: (0, j)),
                  pl.BlockSpec((d, tf), lambda i, j: (0, j)),
                  pl.BlockSpec((tf, d), lambda i, j: (j, 0)),
                  pl.BlockSpec((tm, d), row), pl.BlockSpec((1, d), fixed), pl.BlockSpec((1, d), fixed)],
        out_specs=[pl.BlockSpec((tm, d), row), pl.BlockSpec((tm, d), row)],
        out_shape=[jax.ShapeDtypeStruct((t, d), F32), jax.ShapeDtypeStruct((t, d), BF16)],
        scratch_shapes=[pltpu.VMEM((tm, d), F32)],
        compiler_params=_cp("parallel", "arbitrary"),
        name="ffn_ln",
    )(xb, w1, w3, w2, res, g.reshape(1, d), b.reshape(1, d))


def _sb_kernel(q_ref, k_ref, v_ref, o_ref, acc_ref, carry_ref, *, tq, scale):
    i = pl.program_id(1)
    q = q_ref[...]
    row = lax.broadcasted_iota(jnp.int32, (tq, tq), 0)
    col = lax.broadcasted_iota(jnp.int32, (tq, tq), 1)
    later_sum = (row > col).astype(BF16)
    past = col < row

    def block(kb, diagonal):
        start = pl.multiple_of(kb * tq, tq)
        k = k_ref[pl.ds(start, tq), :]
        v = v_ref[pl.ds(start, tq), :]
        z = _dot_nt(q, k) * scale
        lk = -_softplus(z)
        if diagonal:
            lk = jnp.where(past, lk, 0.0)
        hi = lk.astype(BF16)
        lo = (lk - hi.astype(F32)).astype(BF16)
        later = _dot(hi, later_sum) + _dot(lo, later_sum)
        logw = z + lk + later + carry_ref[...]
        w = jnp.exp(logw)
        if diagonal:
            w = jnp.where(past, w, 0.0)
        acc_ref[...] += _dot(w.astype(BF16), v)
        carry_ref[...] += jnp.sum(lk, axis=1, keepdims=True)

    acc_ref[...] = jnp.zeros_like(acc_ref)
    carry_ref[...] = jnp.zeros_like(carry_ref)
    block(i, True)

    def cond(kb):
        return jnp.logical_and(kb >= 0, jnp.max(carry_ref[...]) > SB_SKIP_LOG)

    def body(kb):
        block(kb, False)
        return kb - 1

    lax.while_loop(cond, body, i - 1)
    o_ref[...] = acc_ref[...].astype(o_ref.dtype)


def _sb_attention(h, batch, seq, tq=256):
    nh = N_HEADS_SB
    nq = seq // tq
    return pl.pallas_call(
        functools.partial(_sb_kernel, tq=tq, scale=HEAD_DIM ** -0.5),
        grid=(batch * nh, nq),
        in_specs=[pl.BlockSpec((None, tq, HEAD_DIM), lambda bh, i: (bh // nh, i, bh % nh)),
                  pl.BlockSpec((None, seq, HEAD_DIM), lambda bh, i: (bh // nh, 0, nh + bh % nh)),
                  pl.BlockSpec((None, seq, HEAD_DIM), lambda bh, i: (bh // nh, 0, 2 * nh + bh % nh))],
        out_specs=pl.BlockSpec((None, tq, HEAD_DIM), lambda bh, i: (bh // nh, i, bh % nh)),
        out_shape=jax.ShapeDtypeStruct((batch, seq, nh * HEAD_DIM), BF16),
        scratch_shapes=[pltpu.VMEM((tq, HEAD_DIM), F32), pltpu.VMEM((tq, 1), F32)],
        compiler_params=_cp("parallel", "arbitrary"),
        name="sb_attention",
    )(h, h, h)


DW_UNROLL = 8


def _dw_kernel(q_ref, k_ref, v_ref, o_ref, m_ref, l_ref, acc_ref, *, scale):
    nk = DW_KEYS
    t0 = pl.program_id(1) * DW_TILE
    ii = lax.broadcasted_iota(jnp.int32, (nk, 2 * nk), 0)
    jj = lax.broadcasted_iota(jnp.int32, (nk, 2 * nk), 1)
    band = jnp.logical_and(jj >= ii, jj <= ii + nk)
    band_prev = jnp.logical_and(band, jj < nk)
    band_own = jnp.logical_and(band, jj >= nk)
    neg = jnp.float32(-jnp.inf)

    for p_idx, (window, dil) in enumerate(DW_PATTERNS):
        assert window // dil == nk
        span = nk * dil
        n_steps = DW_TILE // nk

        def step(s, _, dil=dil, span=span, p_idx=p_idx):
            blk = s // dil
            r = s % dil
            off = blk * span + r
            rows = pl.ds(off, nk, stride=dil)
            cur = pl.ds(t0 + off, nk, stride=dil)
            has_prev = (t0 + blk * span) > 0
            prev = pl.ds(jnp.maximum(t0 + blk * span - span, 0) + r, nk, stride=dil)
            qs = q_ref[rows, :].astype(BF16)
            keys = jnp.concatenate([k_ref[prev, :], k_ref[cur, :]], axis=0).astype(BF16)
            vals = jnp.concatenate([v_ref[prev, :], v_ref[cur, :]], axis=0).astype(BF16)
            sc = _dot_nt(qs, keys) * scale
            sc = jnp.where(jnp.logical_or(band_own, jnp.logical_and(band_prev, has_prev)), sc, neg)
            m_new = jnp.max(sc, axis=1, keepdims=True)
            p = jnp.exp(sc - m_new)
            l_new = jnp.sum(p, axis=1, keepdims=True)
            m_ref[p_idx, rows, :] = jnp.broadcast_to(m_new, (nk, HEAD_DIM))
            l_ref[p_idx, rows, :] = jnp.broadcast_to(l_new, (nk, HEAD_DIM))
            acc_ref[p_idx, rows, :] = _dot(p.astype(BF16), vals)
            return 0

        lax.fori_loop(0, n_steps, step, 0, unroll=DW_UNROLL)

    m_all = m_ref[...]
    m_tot = jnp.max(m_all, axis=0)
    w = jnp.exp(m_all - m_tot[None])
    den = jnp.sum(w * l_ref[...], axis=0)
    num = jnp.sum(w * acc_ref[...], axis=0)
    o_ref[...] = (num / den).astype(o_ref.dtype)


def _dw_attention(h, batch, seq):
    nh = N_HEADS_DW
    return pl.pallas_call(
        functools.partial(_dw_kernel, scale=HEAD_DIM ** -0.5),
        grid=(batch * nh, seq // DW_TILE),
        in_specs=[pl.BlockSpec((None, DW_TILE, HEAD_DIM), lambda bh, i: (bh // nh, i, bh % nh)),
                  pl.BlockSpec((None, seq, HEAD_DIM), lambda bh, i: (bh // nh, 0, nh + bh % nh)),
                  pl.BlockSpec((None, seq, HEAD_DIM), lambda bh, i: (bh // nh, 0, 2 * nh + bh % nh))],
        out_specs=pl.BlockSpec((None, DW_TILE, HEAD_DIM), lambda bh, i: (bh // nh, i, bh % nh)),
        out_shape=jax.ShapeDtypeStruct((batch, seq, nh * HEAD_DIM), BF16),
        scratch_shapes=[pltpu.VMEM((len(DW_PATTERNS), DW_TILE, HEAD_DIM), F32)] * 3,
        compiler_params=_cp("parallel", "arbitrary"),
        name="dw_attention",
    )(h, h, h)


def _s5_params(lam_re, lam_im, log_dt, b_re, b_im, c_re, c_im, n_scan):
    L = SSM_CHUNK
    gpl = LANES // SSM_GROUP
    nlb = SSM_GROUPS // gpl
    dt = jnp.exp(log_dt)[:, None]
    mag_log = lam_re * dt
    ang = lam_im * dt

    def power(n):
        n = jnp.asarray(n, F32)[..., None, None]
        mag = jnp.exp(mag_log * n)
        return mag * jnp.cos(ang * n), mag * jnp.sin(ang * n)

    lr, li = power(jnp.ones(()))
    den = lam_re * lam_re + lam_im * lam_im
    cr = ((lr - 1.0) * lam_re + li * lam_im) / den
    ci = (li * lam_re - (lr - 1.0) * lam_im) / den
    bbr = cr[..., None] * b_re - ci[..., None] * b_im
    bbi = cr[..., None] * b_im + ci[..., None] * b_re

    pr, pi = power(jnp.arange(L + 1, dtype=F32))
    mr = pr[:L, :, :, None] * bbr - pi[:L, :, :, None] * bbi
    mi = pr[:L, :, :, None] * bbi + pi[:L, :, :, None] * bbr
    kk = (jnp.einsum('ghp,tgpk->tghk', c_re, mr, precision=HIGHEST)
          - jnp.einsum('ghp,tgpk->tghk', c_im, mi, precision=HIGHEST))
    jj = jnp.arange(L)[:, None]
    ii = jnp.arange(L)[None, :]
    lag = ii - jj
    toe = jnp.where((lag >= 0)[:, :, None, None, None], kk[jnp.clip(lag, 0, L - 1)], 0.0)
    toe = toe.reshape(L, L, nlb, gpl, SSM_GROUP, SSM_GROUP)
    c_toe = toe.transpose(2, 0, 3, 5, 1, 4).reshape(nlb, L * LANES, L * SSM_GROUP)
    qr = pr[L - 1 - jnp.arange(L)]
    qi = pi[L - 1 - jnp.arange(L)]
    inr = (qr[..., None] * bbr - qi[..., None] * bbi).reshape(L, nlb, gpl, SSM_STATE, SSM_GROUP)
    ini = (qr[..., None] * bbi + qi[..., None] * bbr).reshape(L, nlb, gpl, SSM_STATE, SSM_GROUP)
    c_in = jnp.concatenate(
        [part.transpose(1, 0, 2, 4, 3).reshape(nlb, L * LANES, SSM_STATE) for part in (inr, ini)], axis=2)
    orr = c_re[None] * pr[1:, :, None, :] - c_im[None] * pi[1:, :, None, :]
    oii = c_re[None] * pi[1:, :, None, :] + c_im[None] * pr[1:, :, None, :]
    half = gpl * SSM_STATE
    c_out = jnp.concatenate(
        [part.reshape(L, nlb, gpl, SSM_GROUP, SSM_STATE).transpose(1, 2, 4, 0, 3)
         .reshape(nlb, half, L * SSM_GROUP) for part in (orr, -oii)], axis=1)
    ar, ai = power(float(L) * (2.0 ** jnp.arange(n_scan, dtype=F32)))
    ar = ar.reshape(n_scan, nlb, half).transpose(1, 0, 2)
    ai = ai.reshape(n_scan, nlb, half).transpose(1, 0, 2)
    return c_toe.astype(BF16), c_in.astype(BF16), c_out.astype(BF16), ar, ai


def _spread_matrix(inner, reps, period):
    rows = jnp.arange(inner * period)
    cols = jnp.arange(inner * reps * period)
    same_a = (rows[:, None] // period) == (cols[None, :] // (reps * period))
    same_c = (rows[:, None] % period) == (cols[None, :] % period)
    return jnp.logical_and(same_a, same_c).astype(BF16)


def _spread_groups(compact, spread, row_period, col_period):
    gpl = LANES // SSM_GROUP
    full = _dot(compact, spread)
    rg = (lax.broadcasted_iota(jnp.int32, full.shape, 0) // row_period) % gpl
    cg = (lax.broadcasted_iota(jnp.int32, full.shape, 1) // col_period) % gpl
    return jnp.where(rg == cg, full, 0.0).astype(BF16)


def _s5_kernel(x_ref, ct_ref, ci_ref, co_ref, et_ref, ei_ref, ar_ref, ai_ref, y_ref,
               wt_ref, wi_ref, wo_ref, sr_ref, si_ref, *, n_scan, m):
    L = SSM_CHUNK

    @pl.when(jnp.logical_and(pl.program_id(1) == 0, pl.program_id(2) == 0))
    def _():
        wt_ref[...] = _spread_groups(ct_ref[...], et_ref[...], SSM_GROUP, SSM_GROUP)
        wi_ref[...] = _spread_groups(ci_ref[...], ei_ref[...], SSM_GROUP, SSM_STATE)
        wo_ref[...] = _spread_groups(co_ref[...], et_ref[...], SSM_STATE, SSM_GROUP)

    @pl.when(pl.program_id(2) == 0)
    def _():
        sr_ref[...] = jnp.zeros_like(sr_ref)
        si_ref[...] = jnp.zeros_like(si_ref)

    xc = jnp.concatenate([x_ref[pl.ds(j, m, stride=L), :].astype(BF16) for j in range(L)], axis=1)
    yc = _dot(xc, wt_ref[...])
    z = _dot(xc, wi_ref[...])
    half = z.shape[1] // 2
    zr, zi = z[:, :half], z[:, half:]
    row = lax.broadcasted_iota(jnp.int32, (m, half), 0)
    pr, pi = sr_ref[0:1, :], si_ref[0:1, :]
    a1r, a1i = ar_ref[0:1, :], ai_ref[0:1, :]
    first = row == 0
    zr, zi = (zr + jnp.where(first, a1r * pr - a1i * pi, 0.0),
              zi + jnp.where(first, a1r * pi + a1i * pr, 0.0))
    for k in range(n_scan):
        s = 1 << k
        ar = ar_ref[k:k + 1, :]
        ai = ai_ref[k:k + 1, :]
        keep = row >= s
        tr = jnp.where(keep, pltpu.roll(zr, s, 0), 0.0)
        ti = jnp.where(keep, pltpu.roll(zi, s, 0), 0.0)
        zr, zi = zr + ar * tr - ai * ti, zi + ar * ti + ai * tr
    keep = row >= 1
    s_in = jnp.concatenate([jnp.where(keep, pltpu.roll(zr, 1, 0), pr),
                            jnp.where(keep, pltpu.roll(zi, 1, 0), pi)], axis=1)
    sr_ref[...] = jnp.broadcast_to(zr[m - 1:m, :], sr_ref.shape)
    si_ref[...] = jnp.broadcast_to(zi[m - 1:m, :], si_ref.shape)
    yc = yc + _dot(s_in.astype(BF16), wo_ref[...])
    for i in range(L):
        y_ref[pl.ds(i, m, stride=L), :] = yc[:, i * LANES:(i + 1) * LANES]


def _s5_scan(h1, lam_re, lam_im, log_dt, b_re, b_im, c_re, c_im, batch, seq, rows=2048):
    L = SSM_CHUNK
    rows = min(rows, seq)
    m = rows // L
    nt = seq // rows
    n_scan = max(1, math.ceil(math.log2(m)))
    c_toe, c_in, c_out, ar, ai = _s5_params(lam_re, lam_im, log_dt, b_re, b_im, c_re, c_im, n_scan)
    nlb, half = ar.shape[0], ar.shape[2]
    gpl = LANES // SSM_GROUP
    e_toe = _spread_matrix(L, gpl, SSM_GROUP)
    e_in = _spread_matrix(2, gpl, SSM_STATE)
    per_lb = lambda *blk: pl.BlockSpec((None,) + blk, lambda l, b, i: (l, 0, 0))
    whole = lambda a: pl.BlockSpec(a.shape, lambda l, b, i: (0, 0))
    tile = pl.BlockSpec((rows, LANES), lambda l, b, i: (b * nt + i, l))
    wide = L * LANES
    return pl.pallas_call(
        functools.partial(_s5_kernel, n_scan=n_scan, m=m),
        grid=(nlb, batch, nt),
        in_specs=[tile, per_lb(wide, c_toe.shape[2]), per_lb(wide, c_in.shape[2]),
                  per_lb(2 * half, c_out.shape[2]), whole(e_toe), whole(e_in),
                  per_lb(n_scan, half), per_lb(n_scan, half)],
        out_specs=tile,
        out_shape=jax.ShapeDtypeStruct((batch * seq, SSM_WIDTH), F32),
        scratch_shapes=[pltpu.VMEM((wide, wide), BF16), pltpu.VMEM((wide, 2 * half), BF16),
                        pltpu.VMEM((2 * half, wide), BF16)] + [pltpu.VMEM((SUBLANES, half), F32)] * 2,
        compiler_params=_cp("parallel", "arbitrary", "arbitrary"),
        name="s5_scan",
    )(h1, c_toe, c_in, c_out, e_toe, e_in, ar, ai)


def _s5_glu_kernel(y_ref, u_ref, d_ref, w_ref, b_ref, o_ref):
    y = y_ref[...] + d_ref[...] * u_ref[...]
    c = math.sqrt(2.0 / math.pi)
    z = 0.5 * y * (1.0 + jnp.tanh(c * (y + 0.044715 * (y * y * y))))
    gate = jax.nn.sigmoid(_dot(z.astype(BF16), w_ref[...]) + b_ref[...])
    o_ref[...] = (z * gate).astype(o_ref.dtype)


def _s5_glu(y, h1, d_skip, glu_w, glu_b, tm=512):
    t, w = y.shape
    row = lambda i: (i, 0)
    fixed = lambda i: (0, 0)
    return pl.pallas_call(
        _s5_glu_kernel,
        grid=(t // tm,),
        in_specs=[pl.BlockSpec((tm, w), row), pl.BlockSpec((tm, w), row),
                  pl.BlockSpec((1, w), fixed), pl.BlockSpec((w, w), fixed), pl.BlockSpec((1, w), fixed)],
        out_specs=pl.BlockSpec((tm, w), row),
        out_shape=jax.ShapeDtypeStruct((t, w), BF16),
        compiler_params=_cp("parallel"),
        name="s5_glu",
    )(y, h1, d_skip.reshape(1, w), glu_w, glu_b.reshape(1, w))


def _bdot(a, b):
    return _dot(a.astype(BF16), b.astype(BF16))


def _bdot_nt(a, b):
    return _dot_nt(a.astype(BF16), b.astype(BF16))


def _exact_dot(sel, x):
    sel = sel.astype(BF16)
    x1 = x.astype(BF16)
    r1 = x - x1.astype(F32)
    x2 = r1.astype(BF16)
    x3 = (r1 - x2.astype(F32)).astype(BF16)
    return _dot(sel, x1) + _dot(sel, x2) + _dot(sel, x3)


def _gdn_kernel(q_ref, k_ref, v_ref, gate_ref, ba_ref, wq_ref, wk_ref, wv_ref, alog_ref, dtb_ref,
                nw_ref, o_ref, state_ref, hq_ref, hk_ref, hv_ref, gct_ref, *, lt, nh, hp):
    d = GDN_D
    c = lt
    head0 = (pl.program_id(0) % (nh // hp)) * hp

    @pl.when(pl.program_id(1) == 0)
    def _():
        state_ref[...] = jnp.zeros_like(state_ref)
        for halo_ref in (hq_ref, hk_ref, hv_ref):
            halo_ref[0:SUBLANES, :] = jnp.zeros((SUBLANES, hp * d), F32)

    def conv_silu(x_ref, w_ref, xe_ref):
        w = w_ref[...]
        xe_ref[SUBLANES:, :] = x_ref[...].astype(F32)
        y = xe_ref[SUBLANES:, :] * w[GDN_CONV - 1:GDN_CONV, :]
        for back in range(1, GDN_CONV):
            y = y + xe_ref[pl.ds(SUBLANES - back, lt), :] * w[GDN_CONV - 1 - back:GDN_CONV - back, :]
        xe_ref[0:SUBLANES, :] = xe_ref[lt:, :]
        return y * jax.nn.sigmoid(y)

    def l2norm(t):
        return t * lax.rsqrt(jnp.sum(t * t, axis=-1, keepdims=True) + RMS_EPS)

    q_cs = conv_silu(q_ref, wq_ref, hq_ref)
    k_cs = conv_silu(k_ref, wk_ref, hk_ref)
    v_cs = conv_silu(v_ref, wv_ref, hv_ref)

    ba = ba_ref[...]
    lane = lax.broadcasted_iota(jnp.int32, ba.shape, 1)
    sig_ba = jax.nn.sigmoid(ba)
    g_full = -jnp.exp(alog_ref[...]) * _softplus(ba + dtb_ref[...])
    gc_full = g_full
    rows_i = lax.broadcasted_iota(jnp.int32, gc_full.shape, 0)
    step = 1
    while step < c:
        gc_full = gc_full + jnp.where(rows_i >= step, pltpu.roll(gc_full, step, 0), 0.0)
        step *= 2
    gct_ref[...] = gc_full.T

    ri = lax.broadcasted_iota(jnp.int32, (c, c), 0)
    ci = lax.broadcasted_iota(jnp.int32, (c, c), 1)
    tri = ri >= ci
    strict = ri > ci
    eye = (ri == ci).astype(F32)
    in16 = (ri // 16) == (ci // 16)
    merges = []
    width = 16
    while width < c:
        inner = (ri // width) == (ci // width)
        outer = (ri // (2 * width)) == (ci // (2 * width))
        merges.append(jnp.logical_and(outer, jnp.logical_not(inner)))
        width *= 2

    for hh in range(hp):
        cols = slice(hh * d, (hh + 1) * d)
        q = l2norm(q_cs[:, cols]) * (d ** -0.5)
        k = l2norm(k_cs[:, cols])
        v = v_cs[:, cols]
        beta = jnp.sum(jnp.where(lane == head0 + hh, sig_ba, 0.0), axis=1, keepdims=True)
        gc = jnp.sum(jnp.where(lane == nh + head0 + hh, gc_full, 0.0), axis=1, keepdims=True)
        gc_row = gct_ref[pl.ds(nh + head0 + hh, 1), :]
        gc_last = gc_row[:, c - 1:c]
        decay = jnp.exp(jnp.where(tri, gc - gc_row, -jnp.inf))
        kb = k * beta
        kq_kt = _bdot_nt(jnp.concatenate([kb, q], axis=0), k)
        lower = jnp.where(strict, kq_kt[:c] * decay, 0.0)
        neg = jnp.where(in16, -lower, 0.0)
        inv = eye + neg
        pw = neg
        for _ in range(3):
            pw = _bdot(pw, pw)
            inv = inv + _bdot(inv, pw)
        for off in merges:
            inv = inv - _bdot(_bdot(inv, jnp.where(off, lower, 0.0)), inv)
        e_gc = jnp.exp(gc)
        uw = _bdot(inv, jnp.concatenate([v * beta, kb * e_gc], axis=1))
        u_val, w_key = uw[:, :d], uw[:, d:]
        intra = kq_kt[c:] * decay
        q_dec = q * e_gc
        k_dec = k * jnp.exp(gc_last - gc)
        state = state_ref[hh]
        ws_qs = _bdot(jnp.concatenate([w_key, q_dec], axis=0), state)
        v_new = u_val - ws_qs[:c]
        o = ws_qs[c:] + _bdot(intra, v_new)
        state_ref[hh] = (state * jnp.exp(gc_last)
                         + _dot_tn(k_dec.astype(BF16), v_new.astype(BF16)))
        o = o * lax.rsqrt(jnp.mean(o * o, axis=-1, keepdims=True) + RMS_EPS) * nw_ref[...]
        gt = gate_ref[:, cols].astype(F32)
        o_ref[:, cols] = (o * (gt * jax.nn.sigmoid(gt))).astype(o_ref.dtype)


def _gated_deltanet(h1, ba, conv_w, a_log_pad, dt_bias_pad, norm_w, batch, seq, lt=256, hp=2):
    nh = N_HEADS_GDN
    d = GDN_D
    t = batch * seq
    nt = seq // lt
    ng = nh // hp
    wide = hp * d
    base = 0
    rows = lambda off: (lambda bg, i: ((bg // ng) * nt + i, off + bg % ng))
    cw = lambda off: (lambda bg, i: (0, off + bg % ng))
    fixed = lambda bg, i: (0, 0)
    return pl.pallas_call(
        functools.partial(_gdn_kernel, lt=lt, nh=nh, hp=hp),
        grid=(batch * ng, nt),
        in_specs=[pl.BlockSpec((lt, wide), rows(base)), pl.BlockSpec((lt, wide), rows(base + ng)),
                  pl.BlockSpec((lt, wide), rows(base + 2 * ng)), pl.BlockSpec((lt, wide), rows(base + 3 * ng)),
                  pl.BlockSpec((lt, LANES), lambda bg, i: ((bg // ng) * nt + i, 0)),
                  pl.BlockSpec((GDN_CONV, wide), cw(0)), pl.BlockSpec((GDN_CONV, wide), cw(ng)),
                  pl.BlockSpec((GDN_CONV, wide), cw(2 * ng)),
                  pl.BlockSpec((1, LANES), fixed), pl.BlockSpec((1, LANES), fixed),
                  pl.BlockSpec((1, d), fixed)],
        out_specs=pl.BlockSpec((lt, wide), rows(0)),
        out_shape=jax.ShapeDtypeStruct((t, nh * d), BF16),
        scratch_shapes=[pltpu.VMEM((hp, d, d), F32)] + [pltpu.VMEM((SUBLANES + lt, wide), F32)] * 3
                       + [pltpu.VMEM((LANES, lt), F32)],
        compiler_params=_cp("parallel", "arbitrary"),
        name="gated_deltanet",
    )(h1, h1, h1, h1, ba, conv_w, conv_w, conv_w, a_log_pad, dt_bias_pad, norm_w.reshape(1, d))


DMA_ISSUE_UNROLL = 8


def _row_gather_kernel(src_ref, x_hbm, o_ref, buf_ref, sem, *, tm, n):
    i = pl.program_id(0)

    def issue(step, slot):
        def body(r, _):
            tok = src_ref[step * tm + r]
            pltpu.make_async_copy(x_hbm.at[pl.ds(pl.multiple_of(tok * n, n), n), :],
                                  buf_ref.at[slot, pl.ds(pl.multiple_of(r * n, n), n), :],
                                  sem.at[slot]).start()
            return 0
        lax.fori_loop(0, tm, body, 0, unroll=DMA_ISSUE_UNROLL)

    @pl.when(i == 0)
    def _():
        issue(0, 0)

    @pl.when(i + 1 < pl.num_programs(0))
    def _():
        issue(i + 1, (i + 1) % 2)

    slot = i % 2
    pltpu.make_async_copy(x_hbm.at[pl.ds(0, tm * n), :], buf_ref.at[slot], sem.at[slot]).wait()
    o_ref[...] = _load_row_slabs(buf_ref.at[slot], tm, n).astype(o_ref.dtype)


def _row_gather(x_slabs, src, d, tm=512):
    p = src.shape[0]
    n = d // LANES
    return pl.pallas_call(
        functools.partial(_row_gather_kernel, tm=tm, n=n),
        grid_spec=pltpu.PrefetchScalarGridSpec(
            num_scalar_prefetch=1, grid=(p // tm,),
            in_specs=[pl.BlockSpec(memory_space=pl.ANY)],
            out_specs=pl.BlockSpec((tm, d), lambda i, src: (i, 0)),
            scratch_shapes=[pltpu.VMEM((2, tm * n, LANES), x_slabs.dtype), pltpu.SemaphoreType.DMA((2,))]),
        out_shape=jax.ShapeDtypeStruct((p, d), BF16),
        compiler_params=_cp("arbitrary"),
        name="moe_gather",
    )(src, x_slabs)


def _moe_kernel(te_ref, nv_ref, x_ref, w1_ref, w3_ref, w2_ref, gate_ref, o_ref, acc_ref, *, nj):
    i = pl.program_id(0)
    j = pl.program_id(1)
    valid = i < nv_ref[0]

    @pl.when(j == 0)
    def _():
        acc_ref[...] = jnp.zeros_like(acc_ref)

    @pl.when(valid)
    def _():
        _swiglu_acc(x_ref[...], w1_ref, w3_ref, w2_ref, acc_ref)

    @pl.when(j == nj - 1)
    def _():
        _store_row_slabs(o_ref, acc_ref[...] * gate_ref[...])


def _moe_ffn(xs, gates, tile_expert, n_valid, w1, w3, w2, tm, tf=512):
    p, d = xs.shape
    dff = w1.shape[2]
    nj = dff // tf
    jsel = lambda i, j, nv: jnp.where(i < nv[0], j, nj - 1)
    return pl.pallas_call(
        functools.partial(_moe_kernel, nj=nj),
        grid_spec=pltpu.PrefetchScalarGridSpec(
            num_scalar_prefetch=2, grid=(p // tm, nj),
            in_specs=[pl.BlockSpec((tm, d), lambda i, j, te, nv: (i, 0)),
                      pl.BlockSpec((None, d, tf), lambda i, j, te, nv: (te[i], 0, jsel(i, j, nv))),
                      pl.BlockSpec((None, d, tf), lambda i, j, te, nv: (te[i], 0, jsel(i, j, nv))),
                      pl.BlockSpec((None, tf, d), lambda i, j, te, nv: (te[i], jsel(i, j, nv), 0)),
                      pl.BlockSpec((tm, 1), lambda i, j, te, nv: (i, 0))],
            out_specs=pl.BlockSpec((tm * (d // LANES), LANES), lambda i, j, te, nv: (i, 0)),
            scratch_shapes=[pltpu.VMEM((tm, d), F32)]),
        out_shape=jax.ShapeDtypeStruct((p * (d // LANES), LANES), F32),
        compiler_params=_cp("parallel", "arbitrary"),
        name="moe_ffn",
    )(tile_expert, n_valid, xs, w1, w3, w2, gates)


def _combine_kernel(pos_ref, ys_hbm, res_ref, g_ref, b_ref, o_ref, buf_ref, sem, *, tm, t, n):
    i = pl.program_id(0)

    def issue(step, slot):
        def body(r, _):
            for choice in range(2):
                p = pos_ref[choice * t + step * tm + r]
                pltpu.make_async_copy(ys_hbm.at[pl.ds(pl.multiple_of(p * n, n), n), :],
                                      buf_ref.at[slot, choice, pl.ds(pl.multiple_of(r * n, n), n), :],
                                      sem.at[slot, choice]).start()
            return 0
        lax.fori_loop(0, tm, body, 0, unroll=DMA_ISSUE_UNROLL)

    @pl.when(i == 0)
    def _():
        issue(0, 0)

    @pl.when(i + 1 < pl.num_programs(0))
    def _():
        issue(i + 1, (i + 1) % 2)

    slot = i % 2
    for choice in range(2):
        pltpu.make_async_copy(ys_hbm.at[pl.ds(0, tm * n), :], buf_ref.at[slot, choice],
                              sem.at[slot, choice]).wait()
    y = _load_row_slabs(buf_ref.at[slot, 0], tm, n) + _load_row_slabs(buf_ref.at[slot, 1], tm, n)
    o_ref[...] = _layer_norm(DEEPNORM_ALPHA * res_ref[...] + y, g_ref[...], b_ref[...])


def _moe_combine_ln(ys, pos, res, g, b, tm=256):
    t, d = res.shape
    n = d // LANES
    return pl.pallas_call(
        functools.partial(_combine_kernel, tm=tm, t=t, n=n),
        grid_spec=pltpu.PrefetchScalarGridSpec(
            num_scalar_prefetch=1, grid=(t // tm,),
            in_specs=[pl.BlockSpec(memory_space=pl.ANY),
                      pl.BlockSpec((tm, d), lambda i, pos: (i, 0)),
                      pl.BlockSpec((1, d), lambda i, pos: (0, 0)),
                      pl.BlockSpec((1, d), lambda i, pos: (0, 0))],
            out_specs=pl.BlockSpec((tm, d), lambda i, pos: (i, 0)),
            scratch_shapes=[pltpu.VMEM((2, 2, tm * n, LANES), F32), pltpu.SemaphoreType.DMA((2, 2))]),
        out_shape=jax.ShapeDtypeStruct((t, d), F32),
        compiler_params=_cp("arbitrary"),
        name="moe_combine_ln",
    )(pos, ys, res, g.reshape(1, d), b.reshape(1, d))


def _moe_routing(route, tm):
    t = route.shape[0]
    e = N_EXPERTS
    idx = route[:, 0:2].astype(jnp.int32)
    wts = route[:, 2:4]
    flat_e = idx.T.reshape(-1)
    onehot = (flat_e[:, None] == jnp.arange(e, dtype=jnp.int32)[None, :]).astype(jnp.int32)
    rank = jnp.cumsum(onehot, axis=0) - onehot
    counts = jnp.sum(onehot, axis=0)
    tiles = (counts + tm - 1) // tm
    tile_end = jnp.cumsum(tiles)
    start = (tile_end - tiles) * tm
    pos = jnp.sum(onehot * (start[None, :] + rank), axis=1)
    n_slots = 2 * t + e * tm
    n_tiles = n_slots // tm
    owner = jnp.full((n_slots,), -1, jnp.int32).at[pos].set(jnp.arange(2 * t, dtype=jnp.int32))
    used = owner >= 0
    src = jnp.where(used, owner % t, 0)
    gates = jnp.where(used, wts.T.reshape(-1)[jnp.maximum(owner, 0)], 0.0)
    n_valid = tile_end[-1]
    tile_ids = jnp.arange(n_tiles, dtype=jnp.int32)
    tile_expert = jnp.sum((tile_ids[:, None] >= tile_end[None, :]).astype(jnp.int32), axis=1)
    last_expert = jnp.sum((n_valid - 1 >= tile_end).astype(jnp.int32))
    tile_expert = jnp.where(tile_ids < n_valid, tile_expert, last_expert).astype(jnp.int32)
    return src, gates.reshape(n_slots, 1), pos.astype(jnp.int32), tile_expert, n_valid.reshape(1).astype(jnp.int32)


def _even_layer(x, xb, batch, seq, w_in, w_out, ln_mix_g, ln_mix_b, w1, w3, w2, ln_ffn_g, ln_ffn_b):
    t, d = x.shape
    w_sb = N_HEADS_SB * HEAD_DIM
    w_in_b = w_in.astype(BF16)
    ha = _matmul(xb, w_in_b[:, :3 * w_sb], BF16, 1024, 1024, "in_proj_sb")
    hb = _matmul(xb, w_in_b[:, 3 * w_sb:], F32, 1024, 1024, "in_proj_dw")
    oa = _sb_attention(ha.reshape(batch, seq, -1), batch, seq).reshape(t, -1)
    ob = _dw_attention(hb.reshape(batch, seq, -1), batch, seq).reshape(t, -1)
    w_out_b = w_out.astype(BF16)
    x, xb = _proj_ln(oa, ob, w_out_b[:w_sb], w_out_b[w_sb:], x, ln_mix_g, ln_mix_b)
    return _ffn_ln(xb, x, w1.astype(BF16), w3.astype(BF16), w2.astype(BF16), ln_ffn_g, ln_ffn_b)


def _odd_layer(x, xb, batch, seq, w_in, lam_re, lam_im, log_dt, b_re, b_im, c_re, c_im, d_skip,
               glu_w, glu_b, conv_w, a_log, dt_bias, norm_w, w_out, ln_mix_g, ln_mix_b,
               router_w, w1, w3, w2, ln_ffn_g, ln_ffn_b):
    t, d = x.shape
    nh = N_HEADS_GDN
    wide = SSM_WIDTH + 4 * nh * GDN_D
    w_in_b = w_in[:, :wide].astype(BF16)
    u = _matmul(xb, w_in_b[:, :SSM_WIDTH], F32, 1024, 1024, "in_proj_ssm")
    h1 = _matmul(xb, w_in_b[:, SSM_WIDTH:], BF16, 1024, 1024, "in_proj_gdn")
    w_small = jnp.pad(w_in[:, wide:], ((0, 0), (0, LANES - 2 * nh)))
    ba = _matmul(x, _split_hi_lo(w_small), F32, 512, LANES, "in_proj_gates")

    y = _s5_scan(u, lam_re, lam_im, log_dt, b_re, b_im, c_re, c_im, batch, seq)
    oc = _s5_glu(y, u, d_skip, glu_w.astype(BF16), glu_b)

    pad_hi = LANES - 2 * nh
    a_log_pad = jnp.pad(a_log, (nh, pad_hi)).reshape(1, LANES)
    dt_bias_pad = jnp.pad(dt_bias, (nh, pad_hi)).reshape(1, LANES)
    od = _gated_deltanet(h1, ba, conv_w, a_log_pad, dt_bias_pad, norm_w, batch, seq)

    w_out_b = w_out.astype(BF16)
    rw = _split_hi_lo(jnp.pad(router_w, ((0, 0), (0, LANES - N_EXPERTS))))
    x, x_slabs, route = _proj_ln(oc, od, w_out_b[:SSM_WIDTH], w_out_b[SSM_WIDTH:], x, ln_mix_g, ln_mix_b,
                                 router_w=rw)

    tm = 512
    src, gates, pos, tile_expert, n_valid = _moe_routing(route, tm)
    xs = _row_gather(x_slabs, src, d)
    ys = _moe_ffn(xs, gates, tile_expert, n_valid, w1.astype(BF16), w3.astype(BF16), w2.astype(BF16), tm)
    return _moe_combine_ln(ys, pos, x, ln_ffn_g, ln_ffn_b)


def kernel(x, even_w_in, even_w_out, even_ln_mix_g, even_ln_mix_b, even_ffn_w1, even_ffn_w3, even_ffn_w2, even_ln_ffn_g, even_ln_ffn_b, odd_w_in, odd_ssm_lam_re, odd_ssm_lam_im, odd_ssm_log_dt, odd_ssm_b_re, odd_ssm_b_im, odd_ssm_c_re, odd_ssm_c_im, odd_ssm_d, odd_glu_w, odd_glu_b, odd_gdn_conv_w, odd_gdn_a_log, odd_gdn_dt_bias, odd_gdn_norm_w, odd_w_out, odd_ln_mix_g, odd_ln_mix_b, odd_router_w, odd_moe_w1, odd_moe_w3, odd_moe_w2, odd_ln_ffn_g, odd_ln_ffn_b):
    batch, seq, d = x.shape
    xf = x.reshape(batch * seq, d)
    xf, xb = _even_layer(xf, xf.astype(BF16), batch, seq, even_w_in[0], even_w_out[0],
                         even_ln_mix_g[0], even_ln_mix_b[0], even_ffn_w1[0], even_ffn_w3[0],
                         even_ffn_w2[0], even_ln_ffn_g[0], even_ln_ffn_b[0])
    out = _odd_layer(xf, xb, batch, seq, odd_w_in[0], odd_ssm_lam_re[0], odd_ssm_lam_im[0],
                     odd_ssm_log_dt[0], odd_ssm_b_re[0], odd_ssm_b_im[0], odd_ssm_c_re[0],
                     odd_ssm_c_im[0], odd_ssm_d[0], odd_glu_w[0], odd_glu_b[0], odd_gdn_conv_w[0],
                     odd_gdn_a_log[0], odd_gdn_dt_bias[0], odd_gdn_norm_w[0], odd_w_out[0],
                     odd_ln_mix_g[0], odd_ln_mix_b[0], odd_router_w[0], odd_moe_w1[0],
                     odd_moe_w3[0], odd_moe_w2[0], odd_ln_ffn_g[0], odd_ln_ffn_b[0])
    return out.reshape(batch, seq, d)
```

```python
import functools
import math

import jax
import jax.numpy as jnp
from jax import lax
from jax.experimental import pallas as pl
from jax.experimental.pallas import tpu as pltpu

F32 = jnp.float32
BF16 = jnp.bfloat16
HIGHEST = lax.Precision.HIGHEST

HEAD_DIM = 128
N_HEADS_SB = 8
N_HEADS_DW = 8
DW_PATTERNS = ((128, 1), (512, 4), (2048, 16))
DW_KEYS = 128
DW_TILE = 2048
SSM_WIDTH = 1024
SSM_GROUP = 16
SSM_GROUPS = 64
SSM_STATE = 64
SSM_CHUNK = 8
N_HEADS_GDN = 8
GDN_D = 128
GDN_CONV = 4
GDN_CHUNK = 64
N_EXPERTS = 8
DEPTH = 2
DEEPNORM_ALPHA = (2 * DEPTH) ** 0.25
LN_EPS = 1e-5
RMS_EPS = 1e-6

LANES = 128
SUBLANES = 8
VMEM_LIMIT = 56 * 1024 * 1024
SB_SKIP_LOG = -104.0


def _cp(*sem):
    return pltpu.CompilerParams(dimension_semantics=sem, vmem_limit_bytes=VMEM_LIMIT)


def _layer_norm(y, g, b):
    mu = jnp.mean(y, axis=-1, keepdims=True)
    yc = y - mu
    var = jnp.mean(yc * yc, axis=-1, keepdims=True)
    return yc * lax.rsqrt(var + LN_EPS) * g + b


def _store_row_slabs(ref, val):
    rows, d = val.shape
    n = d // LANES
    for c in range(n):
        ref[pl.ds(c, rows, stride=n), :] = val[:, c * LANES:(c + 1) * LANES]


def _load_row_slabs(ref, rows, n):
    return jnp.concatenate([ref[pl.ds(c, rows, stride=n), :] for c in range(n)], axis=1)


def _softplus(x):
    return jnp.maximum(x, 0.0) + jnp.log1p(jnp.exp(-jnp.abs(x)))


def _dot(a, b, precision=None):
    return jnp.dot(a, b, preferred_element_type=F32, precision=precision)


def _dot_nt(a, b, precision=None):
    return lax.dot_general(a, b, (((1,), (1,)), ((), ())),
                           preferred_element_type=F32, precision=precision)


def _dot_tn(a, b, precision=None):
    return lax.dot_general(a, b, (((0,), (0,)), ((), ())),
                           preferred_element_type=F32, precision=precision)


def _split_hi_lo(w):
    hi = w.astype(BF16)
    return jnp.stack([hi, (w - hi.astype(F32)).astype(BF16)])


def _dot_split(x, w_ref):
    xh = x.astype(BF16)
    xl = (x - xh.astype(F32)).astype(BF16)
    wh = w_ref[0]
    return _dot(xh, wh) + _dot(xl, wh) + _dot(xh, w_ref[1])


def _mm_kernel(a_ref, b_ref, o_ref):
    o_ref[...] = _dot(a_ref[...], b_ref[...]).astype(o_ref.dtype)


def _mm_split_kernel(a_ref, b_ref, o_ref):
    o_ref[...] = _dot_split(a_ref[...], b_ref).astype(o_ref.dtype)


def _matmul(a, b, out_dtype, tm, tn, name):
    m, k = a.shape
    n = b.shape[-1]
    if b.ndim == 3:
        body, b_spec = _mm_split_kernel, pl.BlockSpec((2, k, tn), lambda i, j: (0, 0, j))
    else:
        body, b_spec = _mm_kernel, pl.BlockSpec((k, tn), lambda i, j: (0, j))
    return pl.pallas_call(
        body,
        grid=(m // tm, n // tn),
        in_specs=[pl.BlockSpec((tm, k), lambda i, j: (i, 0)), b_spec],
        out_specs=pl.BlockSpec((tm, tn), lambda i, j: (i, j)),
        out_shape=jax.ShapeDtypeStruct((m, n), out_dtype),
        compiler_params=_cp("parallel", "parallel"),
        name=name,
    )(a, b)


def _proj_ln_kernel(a0_ref, a1_ref, w0_ref, w1_ref, res_ref, g_ref, b_ref, *rest, with_router):
    if with_router:
        rw_ref, of_ref, slab_ref, rt_ref = rest
    else:
        of_ref, ob_ref = rest
    mix = _dot(a0_ref[...], w0_ref[...]) + _dot(a1_ref[...], w1_ref[...])
    xn = _layer_norm(DEEPNORM_ALPHA * res_ref[...] + mix, g_ref[...], b_ref[...])
    of_ref[...] = xn
    if with_router:
        _store_row_slabs(slab_ref, xn)
    else:
        ob_ref[...] = xn.astype(BF16)
    if with_router:
        logits = _dot_split(xn, rw_ref)
        lane = lax.broadcasted_iota(jnp.int32, logits.shape, 1).astype(F32)
        neg = jnp.float32(-jnp.inf)
        lg = jnp.where(lane < N_EXPERTS, logits, neg)
        m1 = jnp.max(lg, axis=1, keepdims=True)
        i1 = jnp.min(jnp.where(lg == m1, lane, float(LANES)), axis=1, keepdims=True)
        lg2 = jnp.where(lane == i1, neg, lg)
        m2 = jnp.max(lg2, axis=1, keepdims=True)
        i2 = jnp.min(jnp.where(lg2 == m2, lane, float(LANES)), axis=1, keepdims=True)
        e2 = jnp.exp(m2 - m1)
        p1 = 1.0 / (1.0 + e2)
        p2 = e2 * p1
        rt = jnp.where(lane == 0.0, i1, jnp.where(lane == 1.0, i2,
                       jnp.where(lane == 2.0, p1, jnp.where(lane == 3.0, p2, 0.0))))
        rt_ref[...] = rt


def _proj_ln(a0, a1, w0, w1, res, g, b, router_w=None, tm=256):
    t, d = res.shape
    k0, k1 = a0.shape[1], a1.shape[1]
    with_router = router_w is not None
    row = lambda i: (i, 0)
    fixed = lambda i: (0, 0)
    in_specs = [pl.BlockSpec((tm, k0), row), pl.BlockSpec((tm, k1), row),
                pl.BlockSpec((k0, d), fixed), pl.BlockSpec((k1, d), fixed),
                pl.BlockSpec((tm, d), row), pl.BlockSpec((1, d), fixed), pl.BlockSpec((1, d), fixed)]
    args = [a0, a1, w0, w1, res, g.reshape(1, d), b.reshape(1, d)]
    if with_router:
        n = d // LANES
        in_specs.append(pl.BlockSpec((2, d, LANES), lambda i: (0, 0, 0)))
        out_specs = [pl.BlockSpec((tm, d), row), pl.BlockSpec((tm * n, LANES), row),
                     pl.BlockSpec((tm, LANES), row)]
        out_shape = [jax.ShapeDtypeStruct((t, d), F32), jax.ShapeDtypeStruct((t * n, LANES), F32),
                     jax.ShapeDtypeStruct((t, LANES), F32)]
        args.append(router_w)
    else:
        out_specs = [pl.BlockSpec((tm, d), row), pl.BlockSpec((tm, d), row)]
        out_shape = [jax.ShapeDtypeStruct((t, d), F32), jax.ShapeDtypeStruct((t, d), BF16)]
    return pl.pallas_call(
        functools.partial(_proj_ln_kernel, with_router=with_router),
        grid=(t // tm,), in_specs=in_specs, out_specs=out_specs, out_shape=out_shape,
        compiler_params=_cp("parallel"),
        name="proj_ln_router" if with_router else "proj_ln",
    )(*args)


def _swiglu_acc(x, w1_ref, w3_ref, w2_ref, acc_ref):
    h1 = _dot(x, w1_ref[...].astype(BF16))
    h3 = _dot(x, w3_ref[...].astype(BF16))
    act = (h1 * jax.nn.sigmoid(h1) * h3).astype(BF16)
    acc_ref[...] += _dot(act, w2_ref[...].astype(BF16))


def _ffn_kernel(x_ref, w1_ref, w3_ref, w2_ref, res_ref, g_ref, b_ref, of_ref, ob_ref, acc_ref, *, nj):
    j = pl.program_id(1)

    @pl.when(j == 0)
    def _():
        acc_ref[...] = jnp.zeros_like(acc_ref)

    _swiglu_acc(x_ref[...], w1_ref, w3_ref, w2_ref, acc_ref)

    @pl.when(j == nj - 1)
    def _():
        xn = _layer_norm(DEEPNORM_ALPHA * res_ref[...] + acc_ref[...], g_ref[...], b_ref[...])
        of_ref[...] = xn
        ob_ref[...] = xn.astype(BF16)


def _ffn_ln(xb, res, w1, w3, w2, g, b, tm=512, tf=512):
    t, d = res.shape
    dff = w1.shape[1]
    nj = dff // tf
    row = lambda i, j: (i, 0)
    fixed = lambda i, j: (0, 0)
    return pl.pallas_call(
        functools.partial(_ffn_kernel, nj=nj),
        grid=(t // tm, nj),
        in_specs=[pl.BlockSpec((tm, d), row),
                  pl.BlockSpec((d, tf), lambda i, j: (0, j)),
                  pl.BlockSpec((d, tf), lambda i, j: (0, j)),
                  pl.BlockSpec((tf, d), lambda i, j: (j, 0)),
                  pl.BlockSpec((tm, d), row), pl.BlockSpec((1, d), fixed), pl.BlockSpec((1, d), fixed)],
        out_specs=[pl.BlockSpec((tm, d), row), pl.BlockSpec((tm, d), row)],
        out_shape=[jax.ShapeDtypeStruct((t, d), F32), jax.ShapeDtypeStruct((t, d), BF16)],
        scratch_shapes=[pltpu.VMEM((tm, d), F32)],
        compiler_params=_cp("parallel", "arbitrary"),
        name="ffn_ln",
    )(xb, w1, w3, w2, res, g.reshape(1, d), b.reshape(1, d))


def _sb_kernel(q_ref, k_ref, v_ref, o_ref, acc_ref, carry_ref, *, tq, scale):
    i = pl.program_id(1)
    q = q_ref[...]
    row = lax.broadcasted_iota(jnp.int32, (tq, tq), 0)
    col = lax.broadcasted_iota(jnp.int32, (tq, tq), 1)
    later_sum = (row > col).astype(BF16)
    past = col < row

    def block_terms(kb, diagonal):
        start = pl.multiple_of(kb * tq, tq)
        k = k_ref[pl.ds(start, tq), :]
        v = v_ref[pl.ds(start, tq), :]
        z = _dot_nt(q, k) * scale
        lk = -_softplus(z)
        if diagonal:
            lk = jnp.where(past, lk, 0.0)
        hi = lk.astype(BF16)
        lo = (lk - hi.astype(F32)).astype(BF16)
        later = _dot(hi, later_sum) + _dot(lo, later_sum)
        return z + lk + later, jnp.sum(lk, axis=1, keepdims=True), v

    has_prev = i >= 1
    logw0, tot0, v0 = block_terms(i, True)
    logw1, tot1, v1 = block_terms(jnp.maximum(i - 1, 0), False)
    w0 = jnp.where(past, jnp.exp(logw0), 0.0)
    w1 = jnp.where(has_prev, jnp.exp(logw1 + tot0), 0.0)
    acc_ref[...] = _dot(w0.astype(BF16), v0) + _dot(w1.astype(BF16), v1)
    carry_ref[...] = tot0 + jnp.where(has_prev, tot1, 0.0)

    def cond(kb):
        return jnp.logical_and(kb >= 0, jnp.max(carry_ref[...]) > SB_SKIP_LOG)

    def body(kb):
        logw, tot, v = block_terms(kb, False)
        acc_ref[...] += _dot(jnp.exp(logw + carry_ref[...]).astype(BF16), v)
        carry_ref[...] += tot
        return kb - 1

    lax.while_loop(cond, body, i - 2)
    o_ref[...] = acc_ref[...].astype(o_ref.dtype)


def _sb_attention(h, batch, seq, tq=256):
    nh = N_HEADS_SB
    nq = seq // tq
    return pl.pallas_call(
        functools.partial(_sb_kernel, tq=tq, scale=HEAD_DIM ** -0.5),
        grid=(batch * nh, nq),
        in_specs=[pl.BlockSpec((None, tq, HEAD_DIM), lambda bh, i: (bh // nh, i, bh % nh)),
                  pl.BlockSpec((None, seq, HEAD_DIM), lambda bh, i: (bh // nh, 0, nh + bh % nh)),
                  pl.BlockSpec((None, seq, HEAD_DIM), lambda bh, i: (bh // nh, 0, 2 * nh + bh % nh))],
        out_specs=pl.BlockSpec((None, tq, HEAD_DIM), lambda bh, i: (bh // nh, i, bh % nh)),
        out_shape=jax.ShapeDtypeStruct((batch, seq, nh * HEAD_DIM), BF16),
        scratch_shapes=[pltpu.VMEM((tq, HEAD_DIM), F32), pltpu.VMEM((tq, 1), F32)],
        compiler_params=_cp("parallel", "arbitrary"),
        name="sb_attention",
    )(h, h, h)


DW_UNROLL = 8


def _dw_kernel(q_ref, k_ref, v_ref, o_ref, m_ref, l_ref, acc_ref, *, scale):
    nk = DW_KEYS
    t0 = pl.program_id(1) * DW_TILE
    ii = lax.broadcasted_iota(jnp.int32, (nk, 2 * nk), 0)
    jj = lax.broadcasted_iota(jnp.int32, (nk, 2 * nk), 1)
    band = jnp.logical_and(jj >= ii, jj <= ii + nk)
    band_prev = jnp.logical_and(band, jj < nk)
    band_own = jnp.logical_and(band, jj >= nk)
    neg = jnp.float32(-jnp.inf)

    for p_idx, (window, dil) in enumerate(DW_PATTERNS):
        assert window // dil == nk
        span = nk * dil
        n_steps = DW_TILE // nk

        def step(s, _, dil=dil, span=span, p_idx=p_idx):
            blk = s // dil
            r = s % dil
            off = blk * span + r
            rows = pl.ds(off, nk, stride=dil)
            cur = pl.ds(t0 + off, nk, stride=dil)
            has_prev = (t0 + blk * span) > 0
            prev = pl.ds(jnp.maximum(t0 + blk * span - span, 0) + r, nk, stride=dil)
            qs = q_ref[rows, :].astype(BF16)
            keys = jnp.concatenate([k_ref[prev, :], k_ref[cur, :]], axis=0).astype(BF16)
            vals = jnp.concatenate([v_ref[prev, :], v_ref[cur, :]], axis=0).astype(BF16)
            sc = _dot_nt(qs, keys) * scale
            sc = jnp.where(jnp.logical_or(band_own, jnp.logical_and(band_prev, has_prev)), sc, neg)
            m_new = jnp.max(sc, axis=1, keepdims=True)
            p = jnp.exp(sc - m_new)
            l_new = jnp.sum(p, axis=1, keepdims=True)
            m_ref[p_idx, rows, :] = jnp.broadcast_to(m_new, (nk, HEAD_DIM))
            l_ref[p_idx, rows, :] = jnp.broadcast_to(l_new, (nk, HEAD_DIM))
            acc_ref[p_idx, rows, :] = _dot(p.astype(BF16), vals)
            return 0

        lax.fori_loop(0, n_steps, step, 0, unroll=DW_UNROLL)

    m_all = m_ref[...]
    m_tot = jnp.max(m_all, axis=0)
    w = jnp.exp(m_all - m_tot[None])
    den = jnp.sum(w * l_ref[...], axis=0)
    num = jnp.sum(w * acc_ref[...], axis=0)
    o_ref[...] = (num / den).astype(o_ref.dtype)


def _dw_attention(h, batch, seq):
    nh = N_HEADS_DW
    return pl.pallas_call(
        functools.partial(_dw_kernel, scale=HEAD_DIM ** -0.5),
        grid=(batch * nh, seq // DW_TILE),
        in_specs=[pl.BlockSpec((None, DW_TILE, HEAD_DIM), lambda bh, i: (bh // nh, i, bh % nh)),
                  pl.BlockSpec((None, seq, HEAD_DIM), lambda bh, i: (bh // nh, 0, nh + bh % nh)),
                  pl.BlockSpec((None, seq, HEAD_DIM), lambda bh, i: (bh // nh, 0, 2 * nh + bh % nh))],
        out_specs=pl.BlockSpec((None, DW_TILE, HEAD_DIM), lambda bh, i: (bh // nh, i, bh % nh)),
        out_shape=jax.ShapeDtypeStruct((batch, seq, nh * HEAD_DIM), BF16),
        scratch_shapes=[pltpu.VMEM((len(DW_PATTERNS), DW_TILE, HEAD_DIM), F32)] * 3,
        compiler_params=_cp("parallel", "arbitrary"),
        name="dw_attention",
    )(h, h, h)


def _s5_params(lam_re, lam_im, log_dt, b_re, b_im, c_re, c_im, n_scan):
    L = SSM_CHUNK
    gpl = LANES // SSM_GROUP
    nlb = SSM_GROUPS // gpl
    dt = jnp.exp(log_dt)[:, None]
    mag_log = lam_re * dt
    ang = lam_im * dt

    def power(n):
        n = jnp.asarray(n, F32)[..., None, None]
        mag = jnp.exp(mag_log * n)
        return mag * jnp.cos(ang * n), mag * jnp.sin(ang * n)

    lr, li = power(jnp.ones(()))
    den = lam_re * lam_re + lam_im * lam_im
    cr = ((lr - 1.0) * lam_re + li * lam_im) / den
    ci = (li * lam_re - (lr - 1.0) * lam_im) / den
    bbr = cr[..., None] * b_re - ci[..., None] * b_im
    bbi = cr[..., None] * b_im + ci[..., None] * b_re

    pr, pi = power(jnp.arange(L + 1, dtype=F32))
    mr = pr[:L, :, :, None] * bbr - pi[:L, :, :, None] * bbi
    mi = pr[:L, :, :, None] * bbi + pi[:L, :, :, None] * bbr
    kk = (jnp.einsum('ghp,tgpk->tghk', c_re, mr, precision=HIGHEST)
          - jnp.einsum('ghp,tgpk->tghk', c_im, mi, precision=HIGHEST))
    jj = jnp.arange(L)[:, None]
    ii = jnp.arange(L)[None, :]
    lag = ii - jj
    toe = jnp.where((lag >= 0)[:, :, None, None, None], kk[jnp.clip(lag, 0, L - 1)], 0.0)
    toe = toe.reshape(L, L, nlb, gpl, SSM_GROUP, SSM_GROUP)
    c_toe = toe.transpose(2, 0, 3, 5, 1, 4).reshape(nlb, L * LANES, L * SSM_GROUP)
    qr = pr[L - 1 - jnp.arange(L)]
    qi = pi[L - 1 - jnp.arange(L)]
    inr = (qr[..., None] * bbr - qi[..., None] * bbi).reshape(L, nlb, gpl, SSM_STATE, SSM_GROUP)
    ini = (qr[..., None] * bbi + qi[..., None] * bbr).reshape(L, nlb, gpl, SSM_STATE, SSM_GROUP)
    c_in = jnp.concatenate(
        [part.transpose(1, 0, 2, 4, 3).reshape(nlb, L * LANES, SSM_STATE) for part in (inr, ini)], axis=2)
    orr = c_re[None] * pr[1:, :, None, :] - c_im[None] * pi[1:, :, None, :]
    oii = c_re[None] * pi[1:, :, None, :] + c_im[None] * pr[1:, :, None, :]
    half = gpl * SSM_STATE
    c_out = jnp.concatenate(
        [part.reshape(L, nlb, gpl, SSM_GROUP, SSM_STATE).transpose(1, 2, 4, 0, 3)
         .reshape(nlb, half, L * SSM_GROUP) for part in (orr, -oii)], axis=1)
    ar, ai = power(float(L) * (2.0 ** jnp.arange(n_scan, dtype=F32)))
    ar = ar.reshape(n_scan, nlb, half).transpose(1, 0, 2)
    ai = ai.reshape(n_scan, nlb, half).transpose(1, 0, 2)
    return c_toe.astype(BF16), c_in.astype(BF16), c_out.astype(BF16), ar, ai


def _spread_matrix(inner, reps, period):
    rows = jnp.arange(inner * period)
    cols = jnp.arange(inner * reps * period)
    same_a = (rows[:, None] // period) == (cols[None, :] // (reps * period))
    same_c = (rows[:, None] % period) == (cols[None, :] % period)
    return jnp.logical_and(same_a, same_c).astype(BF16)


def _spread_groups(compact, spread, row_period, col_period):
    gpl = LANES // SSM_GROUP
    full = _dot(compact, spread)
    rg = (lax.broadcasted_iota(jnp.int32, full.shape, 0) // row_period) % gpl
    cg = (lax.broadcasted_iota(jnp.int32, full.shape, 1) // col_period) % gpl
    return jnp.where(rg == cg, full, 0.0).astype(BF16)


def _s5_kernel(x_ref, ct_ref, ci_ref, co_ref, et_ref, ei_ref, ar_ref, ai_ref, y_ref,
               wt_ref, wi_ref, wo_ref, sr_ref, si_ref, *, n_scan, m):
    L = SSM_CHUNK

    @pl.when(jnp.logical_and(pl.program_id(1) == 0, pl.program_id(2) == 0))
    def _():
        wt_ref[...] = _spread_groups(ct_ref[...], et_ref[...], SSM_GROUP, SSM_GROUP)
        wi_ref[...] = _spread_groups(ci_ref[...], ei_ref[...], SSM_GROUP, SSM_STATE)
        wo_ref[...] = _spread_groups(co_ref[...], et_ref[...], SSM_STATE, SSM_GROUP)

    @pl.when(pl.program_id(2) == 0)
    def _():
        sr_ref[...] = jnp.zeros_like(sr_ref)
        si_ref[...] = jnp.zeros_like(si_ref)

    xc = jnp.concatenate([x_ref[pl.ds(j, m, stride=L), :].astype(BF16) for j in range(L)], axis=1)
    yc = _dot(xc, wt_ref[...])
    z = _dot(xc, wi_ref[...])
    half = z.shape[1] // 2
    zr, zi = z[:, :half], z[:, half:]
    row = lax.broadcasted_iota(jnp.int32, (m, half), 0)
    pr, pi = sr_ref[0:1, :], si_ref[0:1, :]
    a1r, a1i = ar_ref[0:1, :], ai_ref[0:1, :]
    first = row == 0
    zr, zi = (zr + jnp.where(first, a1r * pr - a1i * pi, 0.0),
              zi + jnp.where(first, a1r * pi + a1i * pr, 0.0))
    for k in range(n_scan):
        s = 1 << k
        ar = ar_ref[k:k + 1, :]
        ai = ai_ref[k:k + 1, :]
        keep = row >= s
        tr = jnp.where(keep, pltpu.roll(zr, s, 0), 0.0)
        ti = jnp.where(keep, pltpu.roll(zi, s, 0), 0.0)
        zr, zi = zr + ar * tr - ai * ti, zi + ar * ti + ai * tr
    keep = row >= 1
    s_in = jnp.concatenate([jnp.where(keep, pltpu.roll(zr, 1, 0), pr),
                            jnp.where(keep, pltpu.roll(zi, 1, 0), pi)], axis=1)
    sr_ref[...] = jnp.broadcast_to(zr[m - 1:m, :], sr_ref.shape)
    si_ref[...] = jnp.broadcast_to(zi[m - 1:m, :], si_ref.shape)
    yc = yc + _dot(s_in.astype(BF16), wo_ref[...])
    for i in range(L):
        y_ref[pl.ds(i, m, stride=L), :] = yc[:, i * LANES:(i + 1) * LANES]


def _s5_scan(h1, lam_re, lam_im, log_dt, b_re, b_im, c_re, c_im, batch, seq, rows=2048):
    L = SSM_CHUNK
    rows = min(rows, seq)
    m = rows // L
    nt = seq // rows
    n_scan = max(1, math.ceil(math.log2(m)))
    c_toe, c_in, c_out, ar, ai = _s5_params(lam_re, lam_im, log_dt, b_re, b_im, c_re, c_im, n_scan)
    nlb, half = ar.shape[0], ar.shape[2]
    gpl = LANES // SSM_GROUP
    e_toe = _spread_matrix(L, gpl, SSM_GROUP)
    e_in = _spread_matrix(2, gpl, SSM_STATE)
    per_lb = lambda *blk: pl.BlockSpec((None,) + blk, lambda l, b, i: (l, 0, 0))
    whole = lambda a: pl.BlockSpec(a.shape, lambda l, b, i: (0, 0))
    tile = pl.BlockSpec((rows, LANES), lambda l, b, i: (b * nt + i, l))
    wide = L * LANES
    return pl.pallas_call(
        functools.partial(_s5_kernel, n_scan=n_scan, m=m),
        grid=(nlb, batch, nt),
        in_specs=[tile, per_lb(wide, c_toe.shape[2]), per_lb(wide, c_in.shape[2]),
                  per_lb(2 * half, c_out.shape[2]), whole(e_toe), whole(e_in),
                  per_lb(n_scan, half), per_lb(n_scan, half)],
        out_specs=tile,
        out_shape=jax.ShapeDtypeStruct((batch * seq, SSM_WIDTH), F32),
        scratch_shapes=[pltpu.VMEM((wide, wide), BF16), pltpu.VMEM((wide, 2 * half), BF16),
                        pltpu.VMEM((2 * half, wide), BF16)] + [pltpu.VMEM((SUBLANES, half), F32)] * 2,
        compiler_params=_cp("parallel", "arbitrary", "arbitrary"),
        name="s5_scan",
    )(h1, c_toe, c_in, c_out, e_toe, e_in, ar, ai)


def _s5_glu_kernel(y_ref, u_ref, d_ref, w_ref, b_ref, o_ref):
    y = y_ref[...] + d_ref[...] * u_ref[...]
    c = math.sqrt(2.0 / math.pi)
    z = 0.5 * y * (1.0 + jnp.tanh(c * (y + 0.044715 * (y * y * y))))
    gate = jax.nn.sigmoid(_dot(z.astype(BF16), w_ref[...]) + b_ref[...])
    o_ref[...] = (z * gate).astype(o_ref.dtype)


def _s5_glu(y, h1, d_skip, glu_w, glu_b, tm=512):
    t, w = y.shape
    row = lambda i: (i, 0)
    fixed = lambda i: (0, 0)
    return pl.pallas_call(
        _s5_glu_kernel,
        grid=(t // tm,),
        in_specs=[pl.BlockSpec((tm, w), row), pl.BlockSpec((tm, w), row),
                  pl.BlockSpec((1, w), fixed), pl.BlockSpec((w, w), fixed), pl.BlockSpec((1, w), fixed)],
        out_specs=pl.BlockSpec((tm, w), row),
        out_shape=jax.ShapeDtypeStruct((t, w), BF16),
        compiler_params=_cp("parallel"),
        name="s5_glu",
    )(y, h1, d_skip.reshape(1, w), glu_w, glu_b.reshape(1, w))


def _bdot(a, b):
    return _dot(a.astype(BF16), b.astype(BF16))


def _bdot_nt(a, b):
    return _dot_nt(a.astype(BF16), b.astype(BF16))


def _exact_dot(sel, x):
    sel = sel.astype(BF16)
    x1 = x.astype(BF16)
    r1 = x - x1.astype(F32)
    x2 = r1.astype(BF16)
    x3 = (r1 - x2.astype(F32)).astype(BF16)
    return _dot(sel, x1) + _dot(sel, x2) + _dot(sel, x3)


def _gdn_kernel(q_ref, k_ref, v_ref, gate_ref, ba_ref, wq_ref, wk_ref, wv_ref, alog_ref, dtb_ref,
                nw_ref, o_ref, state_ref, hq_ref, hk_ref, hv_ref, gct_ref, *, lt, nh, hp):
    d = GDN_D
    c = lt
    head0 = (pl.program_id(0) % (nh // hp)) * hp

    @pl.when(pl.program_id(1) == 0)
    def _():
        state_ref[...] = jnp.zeros_like(state_ref)
        for halo_ref in (hq_ref, hk_ref, hv_ref):
            halo_ref[0:SUBLANES, :] = jnp.zeros((SUBLANES, hp * d), F32)

    def conv_silu(x_ref, w_ref, xe_ref):
        w = w_ref[...]
        xe_ref[SUBLANES:, :] = x_ref[...].astype(F32)
        y = xe_ref[SUBLANES:, :] * w[GDN_CONV - 1:GDN_CONV, :]
        for back in range(1, GDN_CONV):
            y = y + xe_ref[pl.ds(SUBLANES - back, lt), :] * w[GDN_CONV - 1 - back:GDN_CONV - back, :]
        xe_ref[0:SUBLANES, :] = xe_ref[lt:, :]
        return y * jax.nn.sigmoid(y)

    def l2norm(t):
        return t * lax.rsqrt(jnp.sum(t * t, axis=-1, keepdims=True) + RMS_EPS)

    q_cs = conv_silu(q_ref, wq_ref, hq_ref)
    k_cs = conv_silu(k_ref, wk_ref, hk_ref)
    v_cs = conv_silu(v_ref, wv_ref, hv_ref)

    ba = ba_ref[...]
    lane = lax.broadcasted_iota(jnp.int32, ba.shape, 1)
    sig_ba = jax.nn.sigmoid(ba)
    g_full = -jnp.exp(alog_ref[...]) * _softplus(ba + dtb_ref[...])
    gc_full = g_full
    rows_i = lax.broadcasted_iota(jnp.int32, gc_full.shape, 0)
    step = 1
    while step < c:
        gc_full = gc_full + jnp.where(rows_i >= step, pltpu.roll(gc_full, step, 0), 0.0)
        step *= 2
    gct_ref[...] = gc_full.T

    ri = lax.broadcasted_iota(jnp.int32, (c, c), 0)
    ci = lax.broadcasted_iota(jnp.int32, (c, c), 1)
    tri = ri >= ci
    strict = ri > ci
    eye = (ri == ci).astype(F32)
    in16 = (ri // 16) == (ci // 16)
    merges = []
    width = 16
    while width < c:
        inner = (ri // width) == (ci // width)
        outer = (ri // (2 * width)) == (ci // (2 * width))
        merges.append(jnp.logical_and(outer, jnp.logical_not(inner)))
        width *= 2

    for hh in range(hp):
        cols = slice(hh * d, (hh + 1) * d)
        q = l2norm(q_cs[:, cols]) * (d ** -0.5)
        k = l2norm(k_cs[:, cols])
        v = v_cs[:, cols]
        beta = jnp.sum(jnp.where(lane == head0 + hh, sig_ba, 0.0), axis=1, keepdims=True)
        gc = jnp.sum(jnp.where(lane == nh + head0 + hh, gc_full, 0.0), axis=1, keepdims=True)
        gc_row = gct_ref[pl.ds(nh + head0 + hh, 1), :]
        gc_last = gc_row[:, c - 1:c]
        decay = jnp.exp(jnp.where(tri, gc - gc_row, -jnp.inf))
        kb = k * beta
        kq_kt = _bdot_nt(jnp.concatenate([kb, q], axis=0), k)
        lower = jnp.where(strict, kq_kt[:c] * decay, 0.0)
        neg = jnp.where(in16, -lower, 0.0)
        inv = eye + neg
        pw = neg
        for _ in range(3):
            pw = _bdot(pw, pw)
            inv = inv + _bdot(inv, pw)
        for off in merges:
            inv = inv - _bdot(_bdot(inv, jnp.where(off, lower, 0.0)), inv)
        e_gc = jnp.exp(gc)
        uw = _bdot(inv, jnp.concatenate([v * beta, kb * e_gc], axis=1))
        u_val, w_key = uw[:, :d], uw[:, d:]
        intra = kq_kt[c:] * decay
        q_dec = q * e_gc
        k_dec = k * jnp.exp(gc_last - gc)
        state = state_ref[hh]
        ws_qs = _bdot(jnp.concatenate([w_key, q_dec], axis=0), state)
        v_new = u_val - ws_qs[:c]
        o = ws_qs[c:] + _bdot(intra, v_new)
        state_ref[hh] = (state * jnp.exp(gc_last)
                         + _dot_tn(k_dec.astype(BF16), v_new.astype(BF16)))
        o = o * lax.rsqrt(jnp.mean(o * o, axis=-1, keepdims=True) + RMS_EPS) * nw_ref[...]
        gt = gate_ref[:, cols].astype(F32)
        o_ref[:, cols] = (o * (gt * jax.nn.sigmoid(gt))).astype(o_ref.dtype)


def _gated_deltanet(h1, ba, conv_w, a_log_pad, dt_bias_pad, norm_w, batch, seq, lt=256, hp=2):
    nh = N_HEADS_GDN
    d = GDN_D
    t = batch * seq
    nt = seq // lt
    ng = nh // hp
    wide = hp * d
    base = 0
    rows = lambda off: (lambda bg, i: ((bg // ng) * nt + i, off + bg % ng))
    cw = lambda off: (lambda bg, i: (0, off + bg % ng))
    fixed = lambda bg, i: (0, 0)
    return pl.pallas_call(
        functools.partial(_gdn_kernel, lt=lt, nh=nh, hp=hp),
        grid=(batch * ng, nt),
        in_specs=[pl.BlockSpec((lt, wide), rows(base)), pl.BlockSpec((lt, wide), rows(base + ng)),
                  pl.BlockSpec((lt, wide), rows(base + 2 * ng)), pl.BlockSpec((lt, wide), rows(base + 3 * ng)),
                  pl.BlockSpec((lt, LANES), lambda bg, i: ((bg // ng) * nt + i, 0)),
                  pl.BlockSpec((GDN_CONV, wide), cw(0)), pl.BlockSpec((GDN_CONV, wide), cw(ng)),
                  pl.BlockSpec((GDN_CONV, wide), cw(2 * ng)),
                  pl.BlockSpec((1, LANES), fixed), pl.BlockSpec((1, LANES), fixed),
                  pl.BlockSpec((1, d), fixed)],
        out_specs=pl.BlockSpec((lt, wide), rows(0)),
        out_shape=jax.ShapeDtypeStruct((t, nh * d), BF16),
        scratch_shapes=[pltpu.VMEM((hp, d, d), F32)] + [pltpu.VMEM((SUBLANES + lt, wide), F32)] * 3
                       + [pltpu.VMEM((LANES, lt), F32)],
        compiler_params=_cp("parallel", "arbitrary"),
        name="gated_deltanet",
    )(h1, h1, h1, h1, ba, conv_w, conv_w, conv_w, a_log_pad, dt_bias_pad, norm_w.reshape(1, d))


DMA_ISSUE_UNROLL = 8


def _row_gather_kernel(src_ref, x_hbm, o_ref, buf_ref, sem, *, tm, n):
    i = pl.program_id(0)

    def issue(step, slot):
        def body(r, _):
            tok = src_ref[step * tm + r]
            pltpu.make_async_copy(x_hbm.at[pl.ds(pl.multiple_of(tok * n, n), n), :],
                                  buf_ref.at[slot, pl.ds(pl.multiple_of(r * n, n), n), :],
                                  sem.at[slot]).start()
            return 0
        lax.fori_loop(0, tm, body, 0, unroll=DMA_ISSUE_UNROLL)

    @pl.when(i == 0)
    def _():
        issue(0, 0)

    @pl.when(i + 1 < pl.num_programs(0))
    def _():
        issue(i + 1, (i + 1) % 2)

    slot = i % 2
    pltpu.make_async_copy(x_hbm.at[pl.ds(0, tm * n), :], buf_ref.at[slot], sem.at[slot]).wait()
    o_ref[...] = _load_row_slabs(buf_ref.at[slot], tm, n).astype(o_ref.dtype)


def _row_gather(x_slabs, src, d, tm=512):
    p = src.shape[0]
    n = d // LANES
    return pl.pallas_call(
        functools.partial(_row_gather_kernel, tm=tm, n=n),
        grid_spec=pltpu.PrefetchScalarGridSpec(
            num_scalar_prefetch=1, grid=(p // tm,),
            in_specs=[pl.BlockSpec(memory_space=pl.ANY)],
            out_specs=pl.BlockSpec((tm, d), lambda i, src: (i, 0)),
            scratch_shapes=[pltpu.VMEM((2, tm * n, LANES), x_slabs.dtype), pltpu.SemaphoreType.DMA((2,))]),
        out_shape=jax.ShapeDtypeStruct((p, d), BF16),
        compiler_params=_cp("arbitrary"),
        name="moe_gather",
    )(src, x_slabs)


def _moe_kernel(te_ref, nv_ref, x_ref, w1_ref, w3_ref, w2_ref, gate_ref, o_ref, acc_ref, *, nj):
    i = pl.program_id(0)
    j = pl.program_id(1)
    valid = i < nv_ref[0]

    @pl.when(j == 0)
    def _():
        acc_ref[...] = jnp.zeros_like(acc_ref)

    @pl.when(valid)
    def _():
        _swiglu_acc(x_ref[...], w1_ref, w3_ref, w2_ref, acc_ref)

    @pl.when(j == nj - 1)
    def _():
        _store_row_slabs(o_ref, acc_ref[...] * gate_ref[...])


def _moe_ffn(xs, gates, tile_expert, n_valid, w1, w3, w2, tm, tf=512):
    p, d = xs.shape
    dff = w1.shape[2]
    nj = dff // tf
    jsel = lambda i, j, nv: jnp.where(i < nv[0], j, nj - 1)
    return pl.pallas_call(
        functools.partial(_moe_kernel, nj=nj),
        grid_spec=pltpu.PrefetchScalarGridSpec(
            num_scalar_prefetch=2, grid=(p // tm, nj),
            in_specs=[pl.BlockSpec((tm, d), lambda i, j, te, nv: (i, 0)),
                      pl.BlockSpec((None, d, tf), lambda i, j, te, nv: (te[i], 0, jsel(i, j, nv))),
                      pl.BlockSpec((None, d, tf), lambda i, j, te, nv: (te[i], 0, jsel(i, j, nv))),
                      pl.BlockSpec((None, tf, d), lambda i, j, te, nv: (te[i], jsel(i, j, nv), 0)),
                      pl.BlockSpec((tm, 1), lambda i, j, te, nv: (i, 0))],
            out_specs=pl.BlockSpec((tm * (d // LANES), LANES), lambda i, j, te, nv: (i, 0)),
            scratch_shapes=[pltpu.VMEM((tm, d), F32)]),
        out_shape=jax.ShapeDtypeStruct((p * (d // LANES), LANES), F32),
        compiler_params=_cp("parallel", "arbitrary"),
        name="moe_ffn",
    )(tile_expert, n_valid, xs, w1, w3, w2, gates)


def _combine_kernel(pos_ref, ys_hbm, res_ref, g_ref, b_ref, o_ref, buf_ref, sem, *, tm, t, n):
    i = pl.program_id(0)

    def issue(step, slot):
        def body(r, _):
            for choice in range(2):
                p = pos_ref[choice * t + step * tm + r]
                pltpu.make_async_copy(ys_hbm.at[pl.ds(pl.multiple_of(p * n, n), n), :],
                                      buf_ref.at[slot, choice, pl.ds(pl.multiple_of(r * n, n), n), :],
                                      sem.at[slot, choice]).start()
            return 0
        lax.fori_loop(0, tm, body, 0, unroll=DMA_ISSUE_UNROLL)

    @pl.when(i == 0)
    def _():
        issue(0, 0)

    @pl.when(i + 1 < pl.num_programs(0))
    def _():
        issue(i + 1, (i + 1) % 2)

    slot = i % 2
    for choice in range(2):
        pltpu.make_async_copy(ys_hbm.at[pl.ds(0, tm * n), :], buf_ref.at[slot, choice],
                              sem.at[slot, choice]).wait()
    y = _load_row_slabs(buf_ref.at[slot, 0], tm, n) + _load_row_slabs(buf_ref.at[slot, 1], tm, n)
    o_ref[...] = _layer_norm(DEEPNORM_ALPHA * res_ref[...] + y, g_ref[...], b_ref[...])


def _moe_combine_ln(ys, pos, res, g, b, tm=256):
    t, d = res.shape
    n = d // LANES
    return pl.pallas_call(
        functools.partial(_combine_kernel, tm=tm, t=t, n=n),
        grid_spec=pltpu.PrefetchScalarGridSpec(
            num_scalar_prefetch=1, grid=(t // tm,),
            in_specs=[pl.BlockSpec(memory_space=pl.ANY),
                      pl.BlockSpec((tm, d), lambda i, pos: (i, 0)),
                      pl.BlockSpec((1, d), lambda i, pos: (0, 0)),
                      pl.BlockSpec((1, d), lambda i, pos: (0, 0))],
            out_specs=pl.BlockSpec((tm, d), lambda i, pos: (i, 0)),
            scratch_shapes=[pltpu.VMEM((2, 2, tm * n, LANES), F32), pltpu.SemaphoreType.DMA((2, 2))]),
        out_shape=jax.ShapeDtypeStruct((t, d), F32),
        compiler_params=_cp("arbitrary"),
        name="moe_combine_ln",
    )(pos, ys, res, g.reshape(1, d), b.reshape(1, d))


def _moe_routing(route, tm):
    t = route.shape[0]
    e = N_EXPERTS
    idx = route[:, 0:2].astype(jnp.int32)
    wts = route[:, 2:4]
    flat_e = idx.T.reshape(-1)
    onehot = (flat_e[:, None] == jnp.arange(e, dtype=jnp.int32)[None, :]).astype(jnp.int32)
    rank = jnp.cumsum(onehot, axis=0) - onehot
    counts = jnp.sum(onehot, axis=0)
    tiles = (counts + tm - 1) // tm
    tile_end = jnp.cumsum(tiles)
    start = (tile_end - tiles) * tm
    pos = jnp.sum(onehot * (start[None, :] + rank), axis=1)
    n_slots = 2 * t + e * tm
    n_tiles = n_slots // tm
    owner = jnp.full((n_slots,), -1, jnp.int32).at[pos].set(jnp.arange(2 * t, dtype=jnp.int32))
    used = owner >= 0
    src = jnp.where(used, owner % t, 0)
    gates = jnp.where(used, wts.T.reshape(-1)[jnp.maximum(owner, 0)], 0.0)
    n_valid = tile_end[-1]
    tile_ids = jnp.arange(n_tiles, dtype=jnp.int32)
    tile_expert = jnp.sum((tile_ids[:, None] >= tile_end[None, :]).astype(jnp.int32), axis=1)
    last_expert = jnp.sum((n_valid - 1 >= tile_end).astype(jnp.int32))
    tile_expert = jnp.where(tile_ids < n_valid, tile_expert, last_expert).astype(jnp.int32)
    return src, gates.reshape(n_slots, 1), pos.astype(jnp.int32), tile_expert, n_valid.reshape(1).astype(jnp.int32)


def _even_layer(x, xb, batch, seq, w_in, w_out, ln_mix_g, ln_mix_b, w1, w3, w2, ln_ffn_g, ln_ffn_b):
    t, d = x.shape
    w_sb = N_HEADS_SB * HEAD_DIM
    w_in_b = w_in.astype(BF16)
    ha = _matmul(xb, w_in_b[:, :3 * w_sb], BF16, 1024, 1024, "in_proj_sb")
    hb = _matmul(xb, w_in_b[:, 3 * w_sb:], F32, 1024, 1024, "in_proj_dw")
    oa = _sb_attention(ha.reshape(batch, seq, -1), batch, seq).reshape(t, -1)
    ob = _dw_attention(hb.reshape(batch, seq, -1), batch, seq).reshape(t, -1)
    w_out_b = w_out.astype(BF16)
    x, xb = _proj_ln(oa, ob, w_out_b[:w_sb], w_out_b[w_sb:], x, ln_mix_g, ln_mix_b)
    return _ffn_ln(xb, x, w1.astype(BF16), w3.astype(BF16), w2.astype(BF16), ln_ffn_g, ln_ffn_b)


def _odd_layer(x, xb, batch, seq, w_in, lam_re, lam_im, log_dt, b_re, b_im, c_re, c_im, d_skip,
               glu_w, glu_b, conv_w, a_log, dt_bias, norm_w, w_out, ln_mix_g, ln_mix_b,
               router_w, w1, w3, w2, ln_ffn_g, ln_ffn_b):
    t, d = x.shape
    nh = N_HEADS_GDN
    wide = SSM_WIDTH + 4 * nh * GDN_D
    w_in_b = w_in[:, :wide].astype(BF16)
    u = _matmul(xb, w_in_b[:, :SSM_WIDTH], F32, 1024, 1024, "in_proj_ssm")
    h1 = _matmul(xb, w_in_b[:, SSM_WIDTH:], BF16, 1024, 1024, "in_proj_gdn")
    w_small = jnp.pad(w_in[:, wide:], ((0, 0), (0, LANES - 2 * nh)))
    ba = _matmul(x, _split_hi_lo(w_small), F32, 512, LANES, "in_proj_gates")

    y = _s5_scan(u, lam_re, lam_im, log_dt, b_re, b_im, c_re, c_im, batch, seq)
    oc = _s5_glu(y, u, d_skip, glu_w.astype(BF16), glu_b)

    pad_hi = LANES - 2 * nh
    a_log_pad = jnp.pad(a_log, (nh, pad_hi)).reshape(1, LANES)
    dt_bias_pad = jnp.pad(dt_bias, (nh, pad_hi)).reshape(1, LANES)
    od = _gated_deltanet(h1, ba, conv_w, a_log_pad, dt_bias_pad, norm_w, batch, seq)

    w_out_b = w_out.astype(BF16)
    rw = _split_hi_lo(jnp.pad(router_w, ((0, 0), (0, LANES - N_EXPERTS))))
    x, x_slabs, route = _proj_ln(oc, od, w_out_b[:SSM_WIDTH], w_out_b[SSM_WIDTH:], x, ln_mix_g, ln_mix_b,
                                 router_w=rw)

    tm = 1024
    src, gates, pos, tile_expert, n_valid = _moe_routing(route, tm)
    xs = _row_gather(x_slabs, src, d)
    ys = _moe_ffn(xs, gates, tile_expert, n_valid, w1, w3, w2, tm, tf=256)
    return _moe_combine_ln(ys, pos, x, ln_ffn_g, ln_ffn_b)


def kernel(x, even_w_in, even_w_out, even_ln_mix_g, even_ln_mix_b, even_ffn_w1, even_ffn_w3, even_ffn_w2, even_ln_ffn_g, even_ln_ffn_b, odd_w_in, odd_ssm_lam_re, odd_ssm_lam_im, odd_ssm_log_dt, odd_ssm_b_re, odd_ssm_b_im, odd_ssm_c_re, odd_ssm_c_im, odd_ssm_d, odd_glu_w, odd_glu_b, odd_gdn_conv_w, odd_gdn_a_log, odd_gdn_dt_bias, odd_gdn_norm_w, odd_w_out, odd_ln_mix_g, odd_ln_mix_b, odd_router_w, odd_moe_w1, odd_moe_w3, odd_moe_w2, odd_ln_ffn_g, odd_ln_ffn_b):
    batch, seq, d = x.shape
    xf = x.reshape(batch * seq, d)
    xf, xb = _even_layer(xf, xf.astype(BF16), batch, seq, even_w_in[0], even_w_out[0],
                         even_ln_mix_g[0], even_ln_mix_b[0], even_ffn_w1[0], even_ffn_w3[0],
                         even_ffn_w2[0], even_ln_ffn_g[0], even_ln_ffn_b[0])
    out = _odd_layer(xf, xb, batch, seq, odd_w_in[0], odd_ssm_lam_re[0], odd_ssm_lam_im[0],
                     odd_ssm_log_dt[0], odd_ssm_b_re[0], odd_ssm_b_im[0], odd_ssm_c_re[0],
                     odd_ssm_c_im[0], odd_ssm_d[0], odd_glu_w[0], odd_glu_b[0], odd_gdn_conv_w[0],
                     odd_gdn_a_log[0], odd_gdn_dt_bias[0], odd_gdn_norm_w[0], odd_w_out[0],
                     odd_ln_mix_g[0], odd_ln_mix_b[0], odd_router_w[0], odd_moe_w1[0],
                     odd_moe_w3[0], odd_moe_w2[0], odd_ln_ffn_g[0], odd_ln_ffn_b[0])
    return out.reshape(batch, seq, d)
```

```python
import functools
import math

import jax
import jax.numpy as jnp
from jax import lax
from jax.experimental import pallas as pl
from jax.experimental.pallas import tpu as pltpu

F32 = jnp.float32
BF16 = jnp.bfloat16
HIGHEST = lax.Precision.HIGHEST

HEAD_DIM = 128
N_HEADS_SB = 8
N_HEADS_DW = 8
DW_PATTERNS = ((128, 1), (512, 4), (2048, 16))
DW_KEYS = 128
DW_TILE = 2048
SSM_WIDTH = 1024
SSM_GROUP = 16
SSM_GROUPS = 64
SSM_STATE = 64
SSM_CHUNK = 8
N_HEADS_GDN = 8
GDN_D = 128
GDN_CONV = 4
GDN_TILE = 256
N_EXPERTS = 8
DEPTH = 2
DEEPNORM_ALPHA = (2 * DEPTH) ** 0.25
LN_EPS = 1e-5
RMS_EPS = 1e-6

LANES = 128
SUBLANES = 8
VMEM_LIMIT = 56 * 1024 * 1024
SB_SKIP_LOG = -104.0


def _cp(*sem):
    return pltpu.CompilerParams(dimension_semantics=sem, vmem_limit_bytes=VMEM_LIMIT)


def _layer_norm(y, g, b):
    mu = jnp.mean(y, axis=-1, keepdims=True)
    yc = y - mu
    var = jnp.mean(yc * yc, axis=-1, keepdims=True)
    return yc * lax.rsqrt(var + LN_EPS) * g + b


def _store_row_slabs(ref, val):
    rows, d = val.shape
    n = d // LANES
    for c in range(n):
        ref[pl.ds(c, rows, stride=n), :] = val[:, c * LANES:(c + 1) * LANES]


def _load_row_slabs(ref, rows, n):
    return jnp.concatenate([ref[pl.ds(c, rows, stride=n), :] for c in range(n)], axis=1)


def _softplus(x):
    return jnp.maximum(x, 0.0) + jnp.log1p(jnp.exp(-jnp.abs(x)))


def _dot(a, b, precision=None):
    return jnp.dot(a, b, preferred_element_type=F32, precision=precision)


def _dot_nt(a, b, precision=None):
    return lax.dot_general(a, b, (((1,), (1,)), ((), ())),
                           preferred_element_type=F32, precision=precision)


def _dot_tn(a, b, precision=None):
    return lax.dot_general(a, b, (((0,), (0,)), ((), ())),
                           preferred_element_type=F32, precision=precision)


def _split_hi_lo(w):
    hi = w.astype(BF16)
    return jnp.stack([hi, (w - hi.astype(F32)).astype(BF16)])


def _dot_split(x, w_ref):
    xh = x.astype(BF16)
    xl = (x - xh.astype(F32)).astype(BF16)
    wh = w_ref[0]
    return _dot(xh, wh) + _dot(xl, wh) + _dot(xh, w_ref[1])


def _mm_kernel(a_ref, b_ref, o_ref):
    o_ref[...] = _dot(a_ref[...], b_ref[...]).astype(o_ref.dtype)


def _mm_split_kernel(a_ref, b_ref, o_ref):
    o_ref[...] = _dot_split(a_ref[...], b_ref).astype(o_ref.dtype)


def _matmul(a, b, out_dtype, tm, tn, name):
    m, k = a.shape
    n = b.shape[-1]
    if b.ndim == 3:
        body, b_spec = _mm_split_kernel, pl.BlockSpec((2, k, tn), lambda i, j: (0, 0, j))
    else:
        body, b_spec = _mm_kernel, pl.BlockSpec((k, tn), lambda i, j: (0, j))
    return pl.pallas_call(
        body,
        grid=(m // tm, n // tn),
        in_specs=[pl.BlockSpec((tm, k), lambda i, j: (i, 0)), b_spec],
        out_specs=pl.BlockSpec((tm, tn), lambda i, j: (i, j)),
        out_shape=jax.ShapeDtypeStruct((m, n), out_dtype),
        compiler_params=_cp("parallel", "parallel"),
        name=name,
    )(a, b)


def _proj_ln_kernel(a0_ref, a1_ref, w0_ref, w1_ref, res_ref, g_ref, b_ref, *rest, with_router):
    if with_router:
        rw_ref, of_ref, slab_ref, rt_ref = rest
    else:
        of_ref, ob_ref = rest
    mix = _dot(a0_ref[...], w0_ref[...]) + _dot(a1_ref[...], w1_ref[...])
    xn = _layer_norm(DEEPNORM_ALPHA * res_ref[...] + mix, g_ref[...], b_ref[...])
    of_ref[...] = xn
    if with_router:
        _store_row_slabs(slab_ref, xn)
    else:
        ob_ref[...] = xn.astype(BF16)
    if with_router:
        logits = _dot_split(xn, rw_ref)
        lane = lax.broadcasted_iota(jnp.int32, logits.shape, 1).astype(F32)
        neg = jnp.float32(-jnp.inf)
        lg = jnp.where(lane < N_EXPERTS, logits, neg)
        m1 = jnp.max(lg, axis=1, keepdims=True)
        i1 = jnp.min(jnp.where(lg == m1, lane, float(LANES)), axis=1, keepdims=True)
        lg2 = jnp.where(lane == i1, neg, lg)
        m2 = jnp.max(lg2, axis=1, keepdims=True)
        i2 = jnp.min(jnp.where(lg2 == m2, lane, float(LANES)), axis=1, keepdims=True)
        e2 = jnp.exp(m2 - m1)
        p1 = 1.0 / (1.0 + e2)
        p2 = e2 * p1
        rt = jnp.where(lane == 0.0, i1, jnp.where(lane == 1.0, i2,
                       jnp.where(lane == 2.0, p1, jnp.where(lane == 3.0, p2, 0.0))))
        rt_ref[...] = rt


def _proj_ln(a0, a1, w0, w1, res, g, b, router_w=None, tm=256):
    t, d = res.shape
    k0, k1 = a0.shape[1], a1.shape[1]
    with_router = router_w is not None
    row = lambda i: (i, 0)
    fixed = lambda i: (0, 0)
    in_specs = [pl.BlockSpec((tm, k0), row), pl.BlockSpec((tm, k1), row),
                pl.BlockSpec((k0, d), fixed), pl.BlockSpec((k1, d), fixed),
                pl.BlockSpec((tm, d), row), pl.BlockSpec((1, d), fixed), pl.BlockSpec((1, d), fixed)]
    args = [a0, a1, w0, w1, res, g.reshape(1, d), b.reshape(1, d)]
    if with_router:
        n = d // LANES
        in_specs.append(pl.BlockSpec((2, d, LANES), lambda i: (0, 0, 0)))
        out_specs = [pl.BlockSpec((tm, d), row), pl.BlockSpec((tm * n, LANES), row),
                     pl.BlockSpec((tm, LANES), row)]
        out_shape = [jax.ShapeDtypeStruct((t, d), F32), jax.ShapeDtypeStruct((t * n, LANES), F32),
                     jax.ShapeDtypeStruct((t, LANES), F32)]
        args.append(router_w)
    else:
        out_specs = [pl.BlockSpec((tm, d), row), pl.BlockSpec((tm, d), row)]
        out_shape = [jax.ShapeDtypeStruct((t, d), F32), jax.ShapeDtypeStruct((t, d), BF16)]
    return pl.pallas_call(
        functools.partial(_proj_ln_kernel, with_router=with_router),
        grid=(t // tm,), in_specs=in_specs, out_specs=out_specs, out_shape=out_shape,
        compiler_params=_cp("parallel"),
        name="proj_ln_router" if with_router else "proj_ln",
    )(*args)


def _swiglu_acc(x, w1_ref, w3_ref, w2_ref, acc_ref):
    h1 = _dot(x, w1_ref[...].astype(BF16))
    h3 = _dot(x, w3_ref[...].astype(BF16))
    act = (h1 * jax.nn.sigmoid(h1) * h3).astype(BF16)
    acc_ref[...] += _dot(act, w2_ref[...].astype(BF16))


def _ffn_kernel(x_ref, w1_ref, w3_ref, w2_ref, res_ref, g_ref, b_ref, of_ref, ob_ref, acc_ref, *, nj):
    j = pl.program_id(1)

    @pl.when(j == 0)
    def _():
        acc_ref[...] = jnp.zeros_like(acc_ref)

    _swiglu_acc(x_ref[...], w1_ref, w3_ref, w2_ref, acc_ref)

    @pl.when(j == nj - 1)
    def _():
        xn = _layer_norm(DEEPNORM_ALPHA * res_ref[...] + acc_ref[...], g_ref[...], b_ref[...])
        of_ref[...] = xn
        ob_ref[...] = xn.astype(BF16)


def _ffn_ln(xb, res, w1, w3, w2, g, b, tm=512, tf=512):
    t, d = res.shape
    dff = w1.shape[1]
    nj = dff // tf
    row = lambda i, j: (i, 0)
    fixed = lambda i, j: (0, 0)
    return pl.pallas_call(
        functools.partial(_ffn_kernel, nj=nj),
        grid=(t // tm, nj),
        in_specs=[pl.BlockSpec((tm, d), row),
                  pl.BlockSpec((d, tf), lambda i, j: (0, j)),
                  pl.BlockSpec((d, tf), lambda i, j: (0, j)),
                  pl.BlockSpec((tf, d), lambda i, j: (j, 0)),
                  pl.BlockSpec((tm, d), row), pl.BlockSpec((1, d), fixed), pl.BlockSpec((1, d), fixed)],
        out_specs=[pl.BlockSpec((tm, d), row), pl.BlockSpec((tm, d), row)],
        out_shape=[jax.ShapeDtypeStruct((t, d), F32), jax.ShapeDtypeStruct((t, d), BF16)],
        scratch_shapes=[pltpu.VMEM((tm, d), F32)],
        compiler_params=_cp("parallel", "arbitrary"),
        name="ffn_ln",
    )(xb, w1, w3, w2, res, g.reshape(1, d), b.reshape(1, d))


def _sb_kernel(q_ref, k_ref, v_ref, o_ref, acc_ref, carry_ref, *, tq, scale):
    i = pl.program_id(1)
    q = q_ref[...]
    row = lax.broadcasted_iota(jnp.int32, (tq, tq), 0)
    col = lax.broadcasted_iota(jnp.int32, (tq, tq), 1)
    later_sum = (row > col).astype(BF16)
    past = col < row

    def block_terms(kb, diagonal):
        start = pl.multiple_of(kb * tq, tq)
        k = k_ref[pl.ds(start, tq), :]
        v = v_ref[pl.ds(start, tq), :]
        z = _dot_nt(q, k) * scale
        lk = -_softplus(z)
        if diagonal:
            lk = jnp.where(past, lk, 0.0)
        hi = lk.astype(BF16)
        lo = (lk - hi.astype(F32)).astype(BF16)
        later = _dot(hi, later_sum) + _dot(lo, later_sum)
        return z + lk + later, jnp.sum(lk, axis=1, keepdims=True), v

    has_prev = i >= 1
    logw0, tot0, v0 = block_terms(i, True)
    logw1, tot1, v1 = block_terms(jnp.maximum(i - 1, 0), False)
    w0 = jnp.where(past, jnp.exp(logw0), 0.0)
    w1 = jnp.where(has_prev, jnp.exp(logw1 + tot0), 0.0)
    acc_ref[...] = _dot(w0.astype(BF16), v0) + _dot(w1.astype(BF16), v1)
    carry_ref[...] = tot0 + jnp.where(has_prev, tot1, 0.0)

    def cond(kb):
        return jnp.logical_and(kb >= 0, jnp.max(carry_ref[...]) > SB_SKIP_LOG)

    def body(kb):
        logw, tot, v = block_terms(kb, False)
        acc_ref[...] += _dot(jnp.exp(logw + carry_ref[...]).astype(BF16), v)
        carry_ref[...] += tot
        return kb - 1

    lax.while_loop(cond, body, i - 2)
    o_ref[...] = acc_ref[...].astype(o_ref.dtype)


def _sb_attention(h, batch, seq, tq=256):
    nh = N_HEADS_SB
    nq = seq // tq
    return pl.pallas_call(
        functools.partial(_sb_kernel, tq=tq, scale=HEAD_DIM ** -0.5),
        grid=(batch * nh, nq),
        in_specs=[pl.BlockSpec((None, tq, HEAD_DIM), lambda bh, i: (bh // nh, i, bh % nh)),
                  pl.BlockSpec((None, seq, HEAD_DIM), lambda bh, i: (bh // nh, 0, nh + bh % nh)),
                  pl.BlockSpec((None, seq, HEAD_DIM), lambda bh, i: (bh // nh, 0, 2 * nh + bh % nh))],
        out_specs=pl.BlockSpec((None, tq, HEAD_DIM), lambda bh, i: (bh // nh, i, bh % nh)),
        out_shape=jax.ShapeDtypeStruct((batch, seq, nh * HEAD_DIM), BF16),
        scratch_shapes=[pltpu.VMEM((tq, HEAD_DIM), F32), pltpu.VMEM((tq, 1), F32)],
        compiler_params=_cp("parallel", "arbitrary"),
        name="sb_attention",
    )(h, h, h)


DW_UNROLL = 8


def _dw_kernel(q_ref, k_ref, v_ref, o_ref, m_ref, l_ref, acc_ref, *, scale):
    nk = DW_KEYS
    t0 = pl.program_id(1) * DW_TILE
    ii = lax.broadcasted_iota(jnp.int32, (nk, 2 * nk), 0)
    jj = lax.broadcasted_iota(jnp.int32, (nk, 2 * nk), 1)
    band = jnp.logical_and(jj >= ii, jj <= ii + nk)
    band_prev = jnp.logical_and(band, jj < nk)
    band_own = jnp.logical_and(band, jj >= nk)
    neg = jnp.float32(-jnp.inf)

    for p_idx, (window, dil) in enumerate(DW_PATTERNS):
        assert window // dil == nk
        span = nk * dil
        n_steps = DW_TILE // nk

        def step(s, _, dil=dil, span=span, p_idx=p_idx):
            blk = s // dil
            r = s % dil
            off = blk * span + r
            rows = pl.ds(off, nk, stride=dil)
            cur = pl.ds(t0 + off, nk, stride=dil)
            has_prev = (t0 + blk * span) > 0
            prev = pl.ds(jnp.maximum(t0 + blk * span - span, 0) + r, nk, stride=dil)
            qs = q_ref[rows, :].astype(BF16)
            keys = jnp.concatenate([k_ref[prev, :], k_ref[cur, :]], axis=0).astype(BF16)
            vals = jnp.concatenate([v_ref[prev, :], v_ref[cur, :]], axis=0).astype(BF16)
            sc = _dot_nt(qs, keys) * scale
            sc = jnp.where(jnp.logical_or(band_own, jnp.logical_and(band_prev, has_prev)), sc, neg)
            m_new = jnp.max(sc, axis=1, keepdims=True)
            p = jnp.exp(sc - m_new)
            l_new = jnp.sum(p, axis=1, keepdims=True)
            m_ref[p_idx, rows, :] = jnp.broadcast_to(m_new, (nk, HEAD_DIM))
            l_ref[p_idx, rows, :] = jnp.broadcast_to(l_new, (nk, HEAD_DIM))
            acc_ref[p_idx, rows, :] = _dot(p.astype(BF16), vals)
            return 0

        lax.fori_loop(0, n_steps, step, 0, unroll=DW_UNROLL)

    m_all = m_ref[...]
    m_tot = jnp.max(m_all, axis=0)
    w = jnp.exp(m_all - m_tot[None])
    den = jnp.sum(w * l_ref[...], axis=0)
    num = jnp.sum(w * acc_ref[...], axis=0)
    o_ref[...] = (num / den).astype(o_ref.dtype)


def _dw_attention(h, batch, seq):
    nh = N_HEADS_DW
    return pl.pallas_call(
        functools.partial(_dw_kernel, scale=HEAD_DIM ** -0.5),
        grid=(batch * nh, seq // DW_TILE),
        in_specs=[pl.BlockSpec((None, DW_TILE, HEAD_DIM), lambda bh, i: (bh // nh, i, bh % nh)),
                  pl.BlockSpec((None, seq, HEAD_DIM), lambda bh, i: (bh // nh, 0, nh + bh % nh)),
                  pl.BlockSpec((None, seq, HEAD_DIM), lambda bh, i: (bh // nh, 0, 2 * nh + bh % nh))],
        out_specs=pl.BlockSpec((None, DW_TILE, HEAD_DIM), lambda bh, i: (bh // nh, i, bh % nh)),
        out_shape=jax.ShapeDtypeStruct((batch, seq, nh * HEAD_DIM), BF16),
        scratch_shapes=[pltpu.VMEM((len(DW_PATTERNS), DW_TILE, HEAD_DIM), F32)] * 3,
        compiler_params=_cp("parallel", "arbitrary"),
        name="dw_attention",
    )(h, h, h)


def _s5_params(lam_re, lam_im, log_dt, b_re, b_im, c_re, c_im, n_scan):
    L = SSM_CHUNK
    gpl = LANES // SSM_GROUP
    nlb = SSM_GROUPS // gpl
    dt = jnp.exp(log_dt)[:, None]
    mag_log = lam_re * dt
    ang = lam_im * dt

    def power(n):
        n = jnp.asarray(n, F32)[..., None, None]
        mag = jnp.exp(mag_log * n)
        return mag * jnp.cos(ang * n), mag * jnp.sin(ang * n)

    lr, li = power(jnp.ones(()))
    den = lam_re * lam_re + lam_im * lam_im
    cr = ((lr - 1.0) * lam_re + li * lam_im) / den
    ci = (li * lam_re - (lr - 1.0) * lam_im) / den
    bbr = cr[..., None] * b_re - ci[..., None] * b_im
    bbi = cr[..., None] * b_im + ci[..., None] * b_re

    pr, pi = power(jnp.arange(L + 1, dtype=F32))
    mr = pr[:L, :, :, None] * bbr - pi[:L, :, :, None] * bbi
    mi = pr[:L, :, :, None] * bbi + pi[:L, :, :, None] * bbr
    kk = (jnp.einsum('ghp,tgpk->tghk', c_re, mr, precision=HIGHEST)
          - jnp.einsum('ghp,tgpk->tghk', c_im, mi, precision=HIGHEST))
    jj = jnp.arange(L)[:, None]
    ii = jnp.arange(L)[None, :]
    lag = ii - jj
    toe = jnp.where((lag >= 0)[:, :, None, None, None], kk[jnp.clip(lag, 0, L - 1)], 0.0)
    toe = toe.reshape(L, L, nlb, gpl, SSM_GROUP, SSM_GROUP)
    c_toe = toe.transpose(2, 0, 3, 5, 1, 4).reshape(nlb, L * LANES, L * SSM_GROUP)
    qr = pr[L - 1 - jnp.arange(L)]
    qi = pi[L - 1 - jnp.arange(L)]
    inr = (qr[..., None] * bbr - qi[..., None] * bbi).reshape(L, nlb, gpl, SSM_STATE, SSM_GROUP)
    ini = (qr[..., None] * bbi + qi[..., None] * bbr).reshape(L, nlb, gpl, SSM_STATE, SSM_GROUP)
    c_in = jnp.concatenate(
        [part.transpose(1, 0, 2, 4, 3).reshape(nlb, L * LANES, SSM_STATE) for part in (inr, ini)], axis=2)
    orr = c_re[None] * pr[1:, :, None, :] - c_im[None] * pi[1:, :, None, :]
    oii = c_re[None] * pi[1:, :, None, :] + c_im[None] * pr[1:, :, None, :]
    half = gpl * SSM_STATE
    c_out = jnp.concatenate(
        [part.reshape(L, nlb, gpl, SSM_GROUP, SSM_STATE).transpose(1, 2, 4, 0, 3)
         .reshape(nlb, half, L * SSM_GROUP) for part in (orr, -oii)], axis=1)
    ar, ai = power(float(L) * (2.0 ** jnp.arange(n_scan, dtype=F32)))
    ar = ar.reshape(n_scan, nlb, half).transpose(1, 0, 2)
    ai = ai.reshape(n_scan, nlb, half).transpose(1, 0, 2)
    return c_toe.astype(BF16), c_in.astype(BF16), c_out.astype(BF16), ar, ai


def _spread_matrix(inner, reps, period):
    rows = jnp.arange(inner * period)
    cols = jnp.arange(inner * reps * period)
    same_a = (rows[:, None] // period) == (cols[None, :] // (reps * period))
    same_c = (rows[:, None] % period) == (cols[None, :] % period)
    return jnp.logical_and(same_a, same_c).astype(BF16)


def _spread_groups(compact, spread, row_period, col_period):
    gpl = LANES // SSM_GROUP
    full = _dot(compact, spread)
    rg = (lax.broadcasted_iota(jnp.int32, full.shape, 0) // row_period) % gpl
    cg = (lax.broadcasted_iota(jnp.int32, full.shape, 1) // col_period) % gpl
    return jnp.where(rg == cg, full, 0.0).astype(BF16)


def _s5_kernel(x_ref, ct_ref, ci_ref, co_ref, et_ref, ei_ref, ar_ref, ai_ref, y_ref,
               wt_ref, wi_ref, wo_ref, sr_ref, si_ref, *, n_scan, m):
    L = SSM_CHUNK

    @pl.when(jnp.logical_and(pl.program_id(1) == 0, pl.program_id(2) == 0))
    def _():
        wt_ref[...] = _spread_groups(ct_ref[...], et_ref[...], SSM_GROUP, SSM_GROUP)
        wi_ref[...] = _spread_groups(ci_ref[...], ei_ref[...], SSM_GROUP, SSM_STATE)
        wo_ref[...] = _spread_groups(co_ref[...], et_ref[...], SSM_STATE, SSM_GROUP)

    @pl.when(pl.program_id(2) == 0)
    def _():
        sr_ref[...] = jnp.zeros_like(sr_ref)
        si_ref[...] = jnp.zeros_like(si_ref)

    xc = jnp.concatenate([x_ref[pl.ds(j, m, stride=L), :].astype(BF16) for j in range(L)], axis=1)
    yc = _dot(xc, wt_ref[...])
    z = _dot(xc, wi_ref[...])
    half = z.shape[1] // 2
    zr, zi = z[:, :half], z[:, half:]
    row = lax.broadcasted_iota(jnp.int32, (m, half), 0)
    pr, pi = sr_ref[0:1, :], si_ref[0:1, :]
    a1r, a1i = ar_ref[0:1, :], ai_ref[0:1, :]
    first = row == 0
    zr, zi = (zr + jnp.where(first, a1r * pr - a1i * pi, 0.0),
              zi + jnp.where(first, a1r * pi + a1i * pr, 0.0))
    for k in range(n_scan):
        s = 1 << k
        ar = ar_ref[k:k + 1, :]
        ai = ai_ref[k:k + 1, :]
        keep = row >= s
        tr = jnp.where(keep, pltpu.roll(zr, s, 0), 0.0)
        ti = jnp.where(keep, pltpu.roll(zi, s, 0), 0.0)
        zr, zi = zr + ar * tr - ai * ti, zi + ar * ti + ai * tr
    keep = row >= 1
    s_in = jnp.concatenate([jnp.where(keep, pltpu.roll(zr, 1, 0), pr),
                            jnp.where(keep, pltpu.roll(zi, 1, 0), pi)], axis=1)
    sr_ref[...] = jnp.broadcast_to(zr[m - 1:m, :], sr_ref.shape)
    si_ref[...] = jnp.broadcast_to(zi[m - 1:m, :], si_ref.shape)
    yc = yc + _dot(s_in.astype(BF16), wo_ref[...])
    for i in range(L):
        y_ref[pl.ds(i, m, stride=L), :] = yc[:, i * LANES:(i + 1) * LANES]


def _s5_scan(h1, lam_re, lam_im, log_dt, b_re, b_im, c_re, c_im, batch, seq, rows=2048):
    L = SSM_CHUNK
    rows = min(rows, seq)
    m = rows // L
    nt = seq // rows
    n_scan = max(1, math.ceil(math.log2(m)))
    c_toe, c_in, c_out, ar, ai = _s5_params(lam_re, lam_im, log_dt, b_re, b_im, c_re, c_im, n_scan)
    nlb, half = ar.shape[0], ar.shape[2]
    gpl = LANES // SSM_GROUP
    e_toe = _spread_matrix(L, gpl, SSM_GROUP)
    e_in = _spread_matrix(2, gpl, SSM_STATE)
    per_lb = lambda *blk: pl.BlockSpec((None,) + blk, lambda l, b, i: (l, 0, 0))
    whole = lambda a: pl.BlockSpec(a.shape, lambda l, b, i: (0, 0))
    tile = pl.BlockSpec((rows, LANES), lambda l, b, i: (b * nt + i, l))
    wide = L * LANES
    return pl.pallas_call(
        functools.partial(_s5_kernel, n_scan=n_scan, m=m),
        grid=(nlb, batch, nt),
        in_specs=[tile, per_lb(wide, c_toe.shape[2]), per_lb(wide, c_in.shape[2]),
                  per_lb(2 * half, c_out.shape[2]), whole(e_toe), whole(e_in),
                  per_lb(n_scan, half), per_lb(n_scan, half)],
        out_specs=tile,
        out_shape=jax.ShapeDtypeStruct((batch * seq, SSM_WIDTH), F32),
        scratch_shapes=[pltpu.VMEM((wide, wide), BF16), pltpu.VMEM((wide, 2 * half), BF16),
                        pltpu.VMEM((2 * half, wide), BF16)] + [pltpu.VMEM((SUBLANES, half), F32)] * 2,
        compiler_params=_cp("parallel", "arbitrary", "arbitrary"),
        name="s5_scan",
    )(h1, c_toe, c_in, c_out, e_toe, e_in, ar, ai)


def _s5_glu_kernel(y_ref, u_ref, d_ref, w_ref, b_ref, o_ref):
    y = y_ref[...] + d_ref[...] * u_ref[...]
    c = math.sqrt(2.0 / math.pi)
    z = 0.5 * y * (1.0 + jnp.tanh(c * (y + 0.044715 * (y * y * y))))
    gate = jax.nn.sigmoid(_dot(z.astype(BF16), w_ref[...]) + b_ref[...])
    o_ref[...] = (z * gate).astype(o_ref.dtype)


def _s5_glu(y, h1, d_skip, glu_w, glu_b, tm=512):
    t, w = y.shape
    row = lambda i: (i, 0)
    fixed = lambda i: (0, 0)
    return pl.pallas_call(
        _s5_glu_kernel,
        grid=(t // tm,),
        in_specs=[pl.BlockSpec((tm, w), row), pl.BlockSpec((tm, w), row),
                  pl.BlockSpec((1, w), fixed), pl.BlockSpec((w, w), fixed), pl.BlockSpec((1, w), fixed)],
        out_specs=pl.BlockSpec((tm, w), row),
        out_shape=jax.ShapeDtypeStruct((t, w), BF16),
        compiler_params=_cp("parallel"),
        name="s5_glu",
    )(y, h1, d_skip.reshape(1, w), glu_w, glu_b.reshape(1, w))


def _bdot(a, b):
    return _dot(a.astype(BF16), b.astype(BF16))


def _bdot_nt(a, b):
    return _dot_nt(a.astype(BF16), b.astype(BF16))


def _gdn_kernel(q_ref, k_ref, v_ref, gate_ref, ba_ref, wq_ref, wk_ref, wv_ref, alog_ref, dtb_ref,
                nw_ref, o_ref, state_ref, hq_ref, hk_ref, hv_ref, gct_ref, *, lt, nh, hp):
    d = GDN_D
    c = lt
    head0 = (pl.program_id(0) % (nh // hp)) * hp

    @pl.when(pl.program_id(1) == 0)
    def _():
        state_ref[...] = jnp.zeros_like(state_ref)
        for halo_ref in (hq_ref, hk_ref, hv_ref):
            halo_ref[0:SUBLANES, :] = jnp.zeros((SUBLANES, hp * d), F32)

    def conv_silu(x_ref, w_ref, xe_ref):
        w = w_ref[...]
        xe_ref[SUBLANES:, :] = x_ref[...].astype(F32)
        y = xe_ref[SUBLANES:, :] * w[GDN_CONV - 1:GDN_CONV, :]
        for back in range(1, GDN_CONV):
            y = y + xe_ref[pl.ds(SUBLANES - back, lt), :] * w[GDN_CONV - 1 - back:GDN_CONV - back, :]
        xe_ref[0:SUBLANES, :] = xe_ref[lt:, :]
        return y * jax.nn.sigmoid(y)

    def l2norm(t):
        return t * lax.rsqrt(jnp.sum(t * t, axis=-1, keepdims=True) + RMS_EPS)

    q_cs = conv_silu(q_ref, wq_ref, hq_ref)
    k_cs = conv_silu(k_ref, wk_ref, hk_ref)
    v_cs = conv_silu(v_ref, wv_ref, hv_ref)

    ba = ba_ref[...]
    lane = lax.broadcasted_iota(jnp.int32, ba.shape, 1)
    sig_ba = jax.nn.sigmoid(ba)
    g_full = -jnp.exp(alog_ref[...]) * _softplus(ba + dtb_ref[...])
    gc_full = g_full
    rows_i = lax.broadcasted_iota(jnp.int32, gc_full.shape, 0)
    step = 1
    while step < c:
        gc_full = gc_full + jnp.where(rows_i >= step, pltpu.roll(gc_full, step, 0), 0.0)
        step *= 2
    gct_ref[...] = gc_full.T

    ri = lax.broadcasted_iota(jnp.int32, (c, c), 0)
    ci = lax.broadcasted_iota(jnp.int32, (c, c), 1)
    tri = ri >= ci
    strict = ri > ci
    eye = (ri == ci).astype(F32)
    in16 = (ri // 16) == (ci // 16)
    merges = []
    width = 16
    while width < c:
        inner = (ri // width) == (ci // width)
        outer = (ri // (2 * width)) == (ci // (2 * width))
        merges.append(jnp.logical_and(outer, jnp.logical_not(inner)))
        width *= 2

    for hh in range(hp):
        cols = slice(hh * d, (hh + 1) * d)
        q = l2norm(q_cs[:, cols]) * (d ** -0.5)
        k = l2norm(k_cs[:, cols])
        v = v_cs[:, cols]
        beta = jnp.sum(jnp.where(lane == head0 + hh, sig_ba, 0.0), axis=1, keepdims=True)
        gc = jnp.sum(jnp.where(lane == nh + head0 + hh, gc_full, 0.0), axis=1, keepdims=True)
        gc_row = gct_ref[pl.ds(nh + head0 + hh, 1), :]
        gc_last = gc_row[:, c - 1:c]
        decay = jnp.exp(jnp.where(tri, gc - gc_row, -jnp.inf))
        kb = k * beta
        kq_kt = _bdot_nt(jnp.concatenate([kb, q], axis=0), k)
        lower = jnp.where(strict, kq_kt[:c] * decay, 0.0)
        neg = jnp.where(in16, -lower, 0.0)
        inv = eye + neg
        pw = neg
        for _ in range(3):
            pw = _bdot(pw, pw)
            inv = inv + _bdot(inv, pw)
        for off in merges:
            inv = inv - _bdot(_bdot(inv, jnp.where(off, lower, 0.0)), inv)
        e_gc = jnp.exp(gc)
        uw = _bdot(inv, jnp.concatenate([v * beta, kb * e_gc], axis=1))
        u_val, w_key = uw[:, :d], uw[:, d:]
        intra = kq_kt[c:] * decay
        q_dec = q * e_gc
        k_dec = k * jnp.exp(gc_last - gc)
        state = state_ref[hh]
        ws_qs = _bdot(jnp.concatenate([w_key, q_dec], axis=0), state)
        v_new = u_val - ws_qs[:c]
        o = ws_qs[c:] + _bdot(intra, v_new)
        state_ref[hh] = (state * jnp.exp(gc_last)
                         + _dot_tn(k_dec.astype(BF16), v_new.astype(BF16)))
        o = o * lax.rsqrt(jnp.mean(o * o, axis=-1, keepdims=True) + RMS_EPS) * nw_ref[...]
        gt = gate_ref[:, cols].astype(F32)
        o_ref[:, cols] = (o * (gt * jax.nn.sigmoid(gt))).astype(o_ref.dtype)


def _gated_deltanet(h1, ba, conv_w, a_log_pad, dt_bias_pad, norm_w, batch, seq, lt=GDN_TILE, hp=2):
    nh = N_HEADS_GDN
    d = GDN_D
    t = batch * seq
    nt = seq // lt
    ng = nh // hp
    wide = hp * d
    base = 0
    rows = lambda off: (lambda bg, i: ((bg // ng) * nt + i, off + bg % ng))
    cw = lambda off: (lambda bg, i: (0, off + bg % ng))
    fixed = lambda bg, i: (0, 0)
    return pl.pallas_call(
        functools.partial(_gdn_kernel, lt=lt, nh=nh, hp=hp),
        grid=(batch * ng, nt),
        in_specs=[pl.BlockSpec((lt, wide), rows(base)), pl.BlockSpec((lt, wide), rows(base + ng)),
                  pl.BlockSpec((lt, wide), rows(base + 2 * ng)), pl.BlockSpec((lt, wide), rows(base + 3 * ng)),
                  pl.BlockSpec((lt, LANES), lambda bg, i: ((bg // ng) * nt + i, 0)),
                  pl.BlockSpec((GDN_CONV, wide), cw(0)), pl.BlockSpec((GDN_CONV, wide), cw(ng)),
                  pl.BlockSpec((GDN_CONV, wide), cw(2 * ng)),
                  pl.BlockSpec((1, LANES), fixed), pl.BlockSpec((1, LANES), fixed),
                  pl.BlockSpec((1, d), fixed)],
        out_specs=pl.BlockSpec((lt, wide), rows(0)),
        out_shape=jax.ShapeDtypeStruct((t, nh * d), BF16),
        scratch_shapes=[pltpu.VMEM((hp, d, d), F32)] + [pltpu.VMEM((SUBLANES + lt, wide), F32)] * 3
                       + [pltpu.VMEM((LANES, lt), F32)],
        compiler_params=_cp("parallel", "arbitrary"),
        name="gated_deltanet",
    )(h1, h1, h1, h1, ba, conv_w, conv_w, conv_w, a_log_pad, dt_bias_pad, norm_w.reshape(1, d))


DMA_ISSUE_UNROLL = 8


def _moe_kernel(te_ref, nv_ref, src_ref, x_hbm, w1_ref, w3_ref, w2_ref, gate_ref, o_ref,
                slab_ref, xb_ref, acc_ref, sem, *, nj, tm, n, per_step):
    i = pl.program_id(0)
    j = pl.program_id(1)
    n_used = nv_ref[0]
    valid = i < n_used

    def start_rows(tile, first, count):
        for k in range(count):
            r = first + k
            tok = src_ref[tile * tm + jnp.minimum(r, tm - 1)]
            pltpu.make_async_copy(x_hbm.at[pl.ds(pl.multiple_of(tok * n, n), n), :],
                                  slab_ref.at[pl.ds(pl.multiple_of(r * n, n), n), :], sem).start()

    @pl.when(jnp.logical_and(i == 0, j == 0))
    def _():
        def body(s, _):
            start_rows(0, s * per_step, per_step)
            return 0
        lax.fori_loop(0, nj, body, 0)

    @pl.when(j == 0)
    def _():
        acc_ref[...] = jnp.zeros_like(acc_ref)

    @pl.when(jnp.logical_and(j == 0, i <= n_used))
    def _():
        pltpu.make_async_copy(x_hbm.at[pl.ds(0, nj * per_step * n), :], slab_ref, sem).wait()
        xb_ref[...] = _load_row_slabs(slab_ref, tm, n).astype(BF16)

    @pl.when(valid)
    def _():
        start_rows(i + 1, j * per_step, per_step)
        _swiglu_acc(xb_ref[...], w1_ref, w3_ref, w2_ref, acc_ref)

    @pl.when(j == nj - 1)
    def _():
        _store_row_slabs(o_ref, acc_ref[...] * gate_ref[...])


def _moe_ffn(x_slabs, src, gates, tile_expert, n_valid, w1, w3, w2, tm, tf):
    p = src.shape[0]
    d, dff = w1.shape[1], w1.shape[2]
    n = d // LANES
    nj = dff // tf
    per_step = -(-tm // nj)
    jsel = lambda i, j, nv: jnp.where(i < nv[0], j, nj - 1)
    return pl.pallas_call(
        functools.partial(_moe_kernel, nj=nj, tm=tm, n=n, per_step=per_step),
        grid_spec=pltpu.PrefetchScalarGridSpec(
            num_scalar_prefetch=3, grid=(p // tm, nj),
            in_specs=[pl.BlockSpec(memory_space=pl.ANY),
                      pl.BlockSpec((None, d, tf), lambda i, j, te, nv, src: (te[i], 0, jsel(i, j, nv))),
                      pl.BlockSpec((None, d, tf), lambda i, j, te, nv, src: (te[i], 0, jsel(i, j, nv))),
                      pl.BlockSpec((None, tf, d), lambda i, j, te, nv, src: (te[i], jsel(i, j, nv), 0)),
                      pl.BlockSpec((tm, 1), lambda i, j, te, nv, src: (i, 0))],
            out_specs=pl.BlockSpec((tm * n, LANES), lambda i, j, te, nv, src: (i, 0)),
            scratch_shapes=[pltpu.VMEM((nj * per_step * n, LANES), F32), pltpu.VMEM((tm, d), BF16),
                            pltpu.VMEM((tm, d), F32), pltpu.SemaphoreType.DMA(())]),
        out_shape=jax.ShapeDtypeStruct((p * n, LANES), F32),
        compiler_params=_cp("arbitrary", "arbitrary"),
        name="moe_ffn",
    )(tile_expert, n_valid, src, x_slabs, w1, w3, w2, gates)


def _combine_kernel(pos_ref, ys_hbm, res_ref, g_ref, b_ref, o_ref, buf_ref, sem, *, tm, t, n):
    i = pl.program_id(0)

    def issue(step, slot):
        def body(r, _):
            for choice in range(2):
                p = pos_ref[choice * t + step * tm + r]
                pltpu.make_async_copy(ys_hbm.at[pl.ds(pl.multiple_of(p * n, n), n), :],
                                      buf_ref.at[slot, choice, pl.ds(pl.multiple_of(r * n, n), n), :],
                                      sem.at[slot, choice]).start()
            return 0
        lax.fori_loop(0, tm, body, 0, unroll=DMA_ISSUE_UNROLL)

    @pl.when(i == 0)
    def _():
        issue(0, 0)

    @pl.when(i + 1 < pl.num_programs(0))
    def _():
        issue(i + 1, (i + 1) % 2)

    slot = i % 2
    for choice in range(2):
        pltpu.make_async_copy(ys_hbm.at[pl.ds(0, tm * n), :], buf_ref.at[slot, choice],
                              sem.at[slot, choice]).wait()
    y = _load_row_slabs(buf_ref.at[slot, 0], tm, n) + _load_row_slabs(buf_ref.at[slot, 1], tm, n)
    o_ref[...] = _layer_norm(DEEPNORM_ALPHA * res_ref[...] + y, g_ref[...], b_ref[...])


def _moe_combine_ln(ys, pos, res, g, b, tm=256):
    t, d = res.shape
    n = d // LANES
    return pl.pallas_call(
        functools.partial(_combine_kernel, tm=tm, t=t, n=n),
        grid_spec=pltpu.PrefetchScalarGridSpec(
            num_scalar_prefetch=1, grid=(t // tm,),
            in_specs=[pl.BlockSpec(memory_space=pl.ANY),
                      pl.BlockSpec((tm, d), lambda i, pos: (i, 0)),
                      pl.BlockSpec((1, d), lambda i, pos: (0, 0)),
                      pl.BlockSpec((1, d), lambda i, pos: (0, 0))],
            out_specs=pl.BlockSpec((tm, d), lambda i, pos: (i, 0)),
            scratch_shapes=[pltpu.VMEM((2, 2, tm * n, LANES), F32), pltpu.SemaphoreType.DMA((2, 2))]),
        out_shape=jax.ShapeDtypeStruct((t, d), F32),
        compiler_params=_cp("arbitrary"),
        name="moe_combine_ln",
    )(pos, ys, res, g.reshape(1, d), b.reshape(1, d))


def _moe_routing(route, tm):
    t = route.shape[0]
    e = N_EXPERTS
    idx = route[:, 0:2].astype(jnp.int32)
    wts = route[:, 2:4]
    flat_e = idx.T.reshape(-1)
    onehot = (flat_e[:, None] == jnp.arange(e, dtype=jnp.int32)[None, :]).astype(jnp.int32)
    rank = jnp.cumsum(onehot, axis=0) - onehot
    counts = jnp.sum(onehot, axis=0)
    tiles = (counts + tm - 1) // tm
    tile_end = jnp.cumsum(tiles)
    start = (tile_end - tiles) * tm
    pos = jnp.sum(onehot * (start[None, :] + rank), axis=1)
    n_slots = 2 * t + e * tm
    n_tiles = n_slots // tm
    owner = jnp.full((n_slots,), -1, jnp.int32).at[pos].set(jnp.arange(2 * t, dtype=jnp.int32))
    used = owner >= 0
    src = jnp.where(used, owner % t, jnp.arange(n_slots, dtype=jnp.int32) % t)
    gates = jnp.where(used, wts.T.reshape(-1)[jnp.maximum(owner, 0)], 0.0)
    n_valid = tile_end[-1]
    tile_ids = jnp.arange(n_tiles, dtype=jnp.int32)
    tile_expert = jnp.sum((tile_ids[:, None] >= tile_end[None, :]).astype(jnp.int32), axis=1)
    last_expert = jnp.sum((n_valid - 1 >= tile_end).astype(jnp.int32))
    tile_expert = jnp.where(tile_ids < n_valid, tile_expert, last_expert).astype(jnp.int32)
    return src, gates.reshape(n_slots, 1), pos.astype(jnp.int32), tile_expert, n_valid.reshape(1).astype(jnp.int32)


def _even_layer(x, xb, batch, seq, w_in, w_out, ln_mix_g, ln_mix_b, w1, w3, w2, ln_ffn_g, ln_ffn_b):
    t, d = x.shape
    w_sb = N_HEADS_SB * HEAD_DIM
    w_in_b = w_in.astype(BF16)
    ha = _matmul(xb, w_in_b[:, :3 * w_sb], BF16, 1024, 1024, "in_proj_sb")
    hb = _matmul(xb, w_in_b[:, 3 * w_sb:], F32, 1024, 1024, "in_proj_dw")
    oa = _sb_attention(ha.reshape(batch, seq, -1), batch, seq).reshape(t, -1)
    ob = _dw_attention(hb.reshape(batch, seq, -1), batch, seq).reshape(t, -1)
    w_out_b = w_out.astype(BF16)
    x, xb = _proj_ln(oa, ob, w_out_b[:w_sb], w_out_b[w_sb:], x, ln_mix_g, ln_mix_b)
    return _ffn_ln(xb, x, w1.astype(BF16), w3.astype(BF16), w2.astype(BF16), ln_ffn_g, ln_ffn_b)


def _odd_layer(x, xb, batch, seq, w_in, lam_re, lam_im, log_dt, b_re, b_im, c_re, c_im, d_skip,
               glu_w, glu_b, conv_w, a_log, dt_bias, norm_w, w_out, ln_mix_g, ln_mix_b,
               router_w, w1, w3, w2, ln_ffn_g, ln_ffn_b):
    t, d = x.shape
    nh = N_HEADS_GDN
    wide = SSM_WIDTH + 4 * nh * GDN_D
    w_in_b = w_in[:, :wide].astype(BF16)
    u = _matmul(xb, w_in_b[:, :SSM_WIDTH], F32, 1024, 1024, "in_proj_ssm")
    h1 = _matmul(xb, w_in_b[:, SSM_WIDTH:], BF16, 1024, 1024, "in_proj_gdn")
    w_small = jnp.pad(w_in[:, wide:], ((0, 0), (0, LANES - 2 * nh)))
    ba = _matmul(x, _split_hi_lo(w_small), F32, 512, LANES, "in_proj_gates")

    y = _s5_scan(u, lam_re, lam_im, log_dt, b_re, b_im, c_re, c_im, batch, seq)
    oc = _s5_glu(y, u, d_skip, glu_w.astype(BF16), glu_b)

    pad_hi = LANES - 2 * nh
    a_log_pad = jnp.pad(a_log, (nh, pad_hi)).reshape(1, LANES)
    dt_bias_pad = jnp.pad(dt_bias, (nh, pad_hi)).reshape(1, LANES)
    od = _gated_deltanet(h1, ba, conv_w, a_log_pad, dt_bias_pad, norm_w, batch, seq)

    w_out_b = w_out.astype(BF16)
    rw = _split_hi_lo(jnp.pad(router_w, ((0, 0), (0, LANES - N_EXPERTS))))
    x, x_slabs, route = _proj_ln(oc, od, w_out_b[:SSM_WIDTH], w_out_b[SSM_WIDTH:], x, ln_mix_g, ln_mix_b,
                                 router_w=rw)

    tm = 1024
    src, gates, pos, tile_expert, n_valid = _moe_routing(route, tm)
    ys = _moe_ffn(x_slabs, src, gates, tile_expert, n_valid, w1, w3, w2, tm, tf=256)
    return _moe_combine_ln(ys, pos, x, ln_ffn_g, ln_ffn_b)


def kernel(x, even_w_in, even_w_out, even_ln_mix_g, even_ln_mix_b, even_ffn_w1, even_ffn_w3, even_ffn_w2, even_ln_ffn_g, even_ln_ffn_b, odd_w_in, odd_ssm_lam_re, odd_ssm_lam_im, odd_ssm_log_dt, odd_ssm_b_re, odd_ssm_b_im, odd_ssm_c_re, odd_ssm_c_im, odd_ssm_d, odd_glu_w, odd_glu_b, odd_gdn_conv_w, odd_gdn_a_log, odd_gdn_dt_bias, odd_gdn_norm_w, odd_w_out, odd_ln_mix_g, odd_ln_mix_b, odd_router_w, odd_moe_w1, odd_moe_w3, odd_moe_w2, odd_ln_ffn_g, odd_ln_ffn_b):
    batch, seq, d = x.shape
    xf = x.reshape(batch * seq, d)
    xf, xb = _even_layer(xf, xf.astype(BF16), batch, seq, even_w_in[0], even_w_out[0],
                         even_ln_mix_g[0], even_ln_mix_b[0], even_ffn_w1[0], even_ffn_w3[0],
                         even_ffn_w2[0], even_ln_ffn_g[0], even_ln_ffn_b[0])
    out = _odd_layer(xf, xb, batch, seq, odd_w_in[0], odd_ssm_lam_re[0], odd_ssm_lam_im[0],
                     odd_ssm_log_dt[0], odd_ssm_b_re[0], odd_ssm_b_im[0], odd_ssm_c_re[0],
                     odd_ssm_c_im[0], odd_ssm_d[0], odd_glu_w[0], odd_glu_b[0], odd_gdn_conv_w[0],
                     odd_gdn_a_log[0], odd_gdn_dt_bias[0], odd_gdn_norm_w[0], odd_w_out[0],
                     odd_ln_mix_g[0], odd_ln_mix_b[0], odd_router_w[0], odd_moe_w1[0],
                     odd_moe_w3[0], odd_moe_w2[0], odd_ln_ffn_g[0], odd_ln_ffn_b[0])
    return out.reshape(batch, seq, d)
```

```python
import functools
import math

import jax
import jax.numpy as jnp
from jax import lax
from jax.experimental import pallas as pl
from jax.experimental.pallas import tpu as pltpu

F32 = jnp.float32
BF16 = jnp.bfloat16
HIGHEST = lax.Precision.HIGHEST

HEAD_DIM = 128
N_HEADS_SB = 8
N_HEADS_DW = 8
DW_PATTERNS = ((128, 1), (512, 4), (2048, 16))
DW_KEYS = 128
DW_TILE = 2048
SSM_WIDTH = 1024
SSM_GROUP = 16
SSM_GROUPS = 64
SSM_STATE = 64
SSM_CHUNK = 8
N_HEADS_GDN = 8
GDN_D = 128
GDN_CONV = 4
GDN_TILE = 256
N_EXPERTS = 8
DEPTH = 2
DEEPNORM_ALPHA = (2 * DEPTH) ** 0.25
LN_EPS = 1e-5
RMS_EPS = 1e-6

LANES = 128
SUBLANES = 8
VMEM_LIMIT = 56 * 1024 * 1024
SB_SKIP_LOG = -104.0


def _cp(*sem):
    return pltpu.CompilerParams(dimension_semantics=sem, vmem_limit_bytes=VMEM_LIMIT)


def _layer_norm(y, g, b):
    mu = jnp.mean(y, axis=-1, keepdims=True)
    yc = y - mu
    var = jnp.mean(yc * yc, axis=-1, keepdims=True)
    return yc * lax.rsqrt(var + LN_EPS) * g + b


def _store_row_slabs(ref, val):
    rows, d = val.shape
    n = d // LANES
    for c in range(n):
        ref[pl.ds(c, rows, stride=n), :] = val[:, c * LANES:(c + 1) * LANES]


def _load_row_slabs(ref, rows, n):
    return jnp.concatenate([ref[pl.ds(c, rows, stride=n), :] for c in range(n)], axis=1)


def _softplus(x):
    return jnp.maximum(x, 0.0) + jnp.log1p(jnp.exp(-jnp.abs(x)))


def _dot(a, b, precision=None):
    return jnp.dot(a, b, preferred_element_type=F32, precision=precision)


def _dot_nt(a, b, precision=None):
    return lax.dot_general(a, b, (((1,), (1,)), ((), ())),
                           preferred_element_type=F32, precision=precision)


def _dot_tn(a, b, precision=None):
    return lax.dot_general(a, b, (((0,), (0,)), ((), ())),
                           preferred_element_type=F32, precision=precision)


def _split_hi_lo(w):
    hi = w.astype(BF16)
    return jnp.stack([hi, (w - hi.astype(F32)).astype(BF16)])


def _dot_split(x, w_ref):
    xh = x.astype(BF16)
    xl = (x - xh.astype(F32)).astype(BF16)
    wh = w_ref[0]
    return _dot(xh, wh) + _dot(xl, wh) + _dot(xh, w_ref[1])


def _mm_kernel(a_ref, b_ref, o_ref):
    o_ref[...] = _dot(a_ref[...], b_ref[...]).astype(o_ref.dtype)


def _mm_split_kernel(a_ref, b_ref, o_ref):
    o_ref[...] = _dot_split(a_ref[...], b_ref).astype(o_ref.dtype)


def _matmul(a, b, out_dtype, tm, tn, name):
    m, k = a.shape
    n = b.shape[-1]
    if b.ndim == 3:
        body, b_spec = _mm_split_kernel, pl.BlockSpec((2, k, tn), lambda i, j: (0, 0, j))
    else:
        body, b_spec = _mm_kernel, pl.BlockSpec((k, tn), lambda i, j: (0, j))
    return pl.pallas_call(
        body,
        grid=(m // tm, n // tn),
        in_specs=[pl.BlockSpec((tm, k), lambda i, j: (i, 0)), b_spec],
        out_specs=pl.BlockSpec((tm, tn), lambda i, j: (i, j)),
        out_shape=jax.ShapeDtypeStruct((m, n), out_dtype),
        compiler_params=_cp("parallel", "parallel"),
        name=name,
    )(a, b)


def _proj_ln_kernel(a0_ref, a1_ref, w0_ref, w1_ref, res_ref, g_ref, b_ref, *rest, with_router):
    if with_router:
        rw_ref, of_ref, slab_ref, rt_ref = rest
    else:
        of_ref, ob_ref = rest
    mix = _dot(a0_ref[...], w0_ref[...]) + _dot(a1_ref[...], w1_ref[...])
    xn = _layer_norm(DEEPNORM_ALPHA * res_ref[...] + mix, g_ref[...], b_ref[...])
    of_ref[...] = xn
    if with_router:
        _store_row_slabs(slab_ref, xn)
    else:
        ob_ref[...] = xn.astype(BF16)
    if with_router:
        logits = _dot_split(xn, rw_ref)
        lane = lax.broadcasted_iota(jnp.int32, logits.shape, 1).astype(F32)
        neg = jnp.float32(-jnp.inf)
        lg = jnp.where(lane < N_EXPERTS, logits, neg)
        m1 = jnp.max(lg, axis=1, keepdims=True)
        i1 = jnp.min(jnp.where(lg == m1, lane, float(LANES)), axis=1, keepdims=True)
        lg2 = jnp.where(lane == i1, neg, lg)
        m2 = jnp.max(lg2, axis=1, keepdims=True)
        i2 = jnp.min(jnp.where(lg2 == m2, lane, float(LANES)), axis=1, keepdims=True)
        e2 = jnp.exp(m2 - m1)
        p1 = 1.0 / (1.0 + e2)
        p2 = e2 * p1
        rt = jnp.where(lane == 0.0, i1, jnp.where(lane == 1.0, i2,
                       jnp.where(lane == 2.0, p1, jnp.where(lane == 3.0, p2, 0.0))))
        rt_ref[...] = rt


def _proj_ln(a0, a1, w0, w1, res, g, b, router_w=None, tm=256):
    t, d = res.shape
    k0, k1 = a0.shape[1], a1.shape[1]
    with_router = router_w is not None
    row = lambda i: (i, 0)
    fixed = lambda i: (0, 0)
    in_specs = [pl.BlockSpec((tm, k0), row), pl.BlockSpec((tm, k1), row),
                pl.BlockSpec((k0, d), fixed), pl.BlockSpec((k1, d), fixed),
                pl.BlockSpec((tm, d), row), pl.BlockSpec((1, d), fixed), pl.BlockSpec((1, d), fixed)]
    args = [a0, a1, w0, w1, res, g.reshape(1, d), b.reshape(1, d)]
    if with_router:
        n = d // LANES
        in_specs.append(pl.BlockSpec((2, d, LANES), lambda i: (0, 0, 0)))
        out_specs = [pl.BlockSpec((tm, d), row), pl.BlockSpec((tm * n, LANES), row),
                     pl.BlockSpec((tm, LANES), row)]
        out_shape = [jax.ShapeDtypeStruct((t, d), F32), jax.ShapeDtypeStruct((t * n, LANES), F32),
                     jax.ShapeDtypeStruct((t, LANES), F32)]
        args.append(router_w)
    else:
        out_specs = [pl.BlockSpec((tm, d), row), pl.BlockSpec((tm, d), row)]
        out_shape = [jax.ShapeDtypeStruct((t, d), F32), jax.ShapeDtypeStruct((t, d), BF16)]
    return pl.pallas_call(
        functools.partial(_proj_ln_kernel, with_router=with_router),
        grid=(t // tm,), in_specs=in_specs, out_specs=out_specs, out_shape=out_shape,
        compiler_params=_cp("parallel"),
        name="proj_ln_router" if with_router else "proj_ln",
    )(*args)


def _swiglu_acc(x, w1_ref, w3_ref, w2_ref, acc_ref):
    h1 = _dot(x, w1_ref[...].astype(BF16))
    h3 = _dot(x, w3_ref[...].astype(BF16))
    act = (h1 * jax.nn.sigmoid(h1) * h3).astype(BF16)
    acc_ref[...] += _dot(act, w2_ref[...].astype(BF16))


def _ffn_kernel(x_ref, w1_ref, w3_ref, w2_ref, res_ref, g_ref, b_ref, of_ref, ob_ref, acc_ref, *, nj):
    j = pl.program_id(1)

    @pl.when(j == 0)
    def _():
        acc_ref[...] = jnp.zeros_like(acc_ref)

    _swiglu_acc(x_ref[...], w1_ref, w3_ref, w2_ref, acc_ref)

    @pl.when(j == nj - 1)
    def _():
        xn = _layer_norm(DEEPNORM_ALPHA * res_ref[...] + acc_ref[...], g_ref[...], b_ref[...])
        of_ref[...] = xn
        ob_ref[...] = xn.astype(BF16)


def _ffn_ln(xb, res, w1, w3, w2, g, b, tm=512, tf=512):
    t, d = res.shape
    dff = w1.shape[1]
    nj = dff // tf
    row = lambda i, j: (i, 0)
    fixed = lambda i, j: (0, 0)
    return pl.pallas_call(
        functools.partial(_ffn_kernel, nj=nj),
        grid=(t // tm, nj),
        in_specs=[pl.BlockSpec((tm, d), row),
                  pl.BlockSpec((d, tf), lambda i, j: (0, j)),
                  pl.BlockSpec((d, tf), lambda i, j: (0, j)),
                  pl.BlockSpec((tf, d), lambda i, j: (j, 0)),
                  pl.BlockSpec((tm, d), row), pl.BlockSpec((1, d), fixed), pl.BlockSpec((1, d), fixed)],
        out_specs=[pl.BlockSpec((tm, d), row), pl.BlockSpec((tm, d), row)],
        out_shape=[jax.ShapeDtypeStruct((t, d), F32), jax.ShapeDtypeStruct((t, d), BF16)],
        scratch_shapes=[pltpu.VMEM((tm, d), F32)],
        compiler_params=_cp("parallel", "arbitrary"),
        name="ffn_ln",
    )(xb, w1, w3, w2, res, g.reshape(1, d), b.reshape(1, d))


def _sb_kernel(q_ref, k_ref, v_ref, o_ref, acc_ref, carry_ref, *, tq, scale):
    i = pl.program_id(1)
    q = q_ref[...]
    row = lax.broadcasted_iota(jnp.int32, (tq, tq), 0)
    col = lax.broadcasted_iota(jnp.int32, (tq, tq), 1)
    later_sum = (row > col).astype(BF16)
    past = col < row

    def blocks_terms(kbs, diagonals):
        starts = [pl.multiple_of(kb * tq, tq) for kb in kbs]
        z = [_dot_nt(q, k_ref[pl.ds(s, tq), :]) * scale for s in starts]
        lk = [-_softplus(zz) for zz in z]
        lk = [jnp.where(past, l, 0.0) if dg else l for l, dg in zip(lk, diagonals)]
        hi = [l.astype(BF16) for l in lk]
        lo = [(l - h.astype(F32)).astype(BF16) for l, h in zip(lk, hi)]
        later = [_dot(h, later_sum) + _dot(l, later_sum) for h, l in zip(hi, lo)]
        logw = [zz + l + lt for zz, l, lt in zip(z, lk, later)]
        tot = [jnp.sum(l, axis=1, keepdims=True) for l in lk]
        return logw, tot, [v_ref[pl.ds(s, tq), :] for s in starts]

    has_prev = i >= 1
    (logw0, logw1), (tot0, tot1), (v0, v1) = blocks_terms([i, jnp.maximum(i - 1, 0)], [True, False])
    w0 = jnp.where(past, jnp.exp(logw0), 0.0)
    w1 = jnp.where(has_prev, jnp.exp(logw1 + tot0), 0.0)
    acc_ref[...] = _dot(w0.astype(BF16), v0) + _dot(w1.astype(BF16), v1)
    carry_ref[...] = tot0 + jnp.where(has_prev, tot1, 0.0)

    def cond(kb):
        return jnp.logical_and(kb >= 0, jnp.max(carry_ref[...]) > SB_SKIP_LOG)

    def body(kb):
        (logw,), (tot,), (v,) = blocks_terms([kb], [False])
        acc_ref[...] += _dot(jnp.exp(logw + carry_ref[...]).astype(BF16), v)
        carry_ref[...] += tot
        return kb - 1

    lax.while_loop(cond, body, i - 2)
    o_ref[...] = acc_ref[...].astype(o_ref.dtype)


def _sb_attention(h, batch, seq, tq=256):
    nh = N_HEADS_SB
    nq = seq // tq
    return pl.pallas_call(
        functools.partial(_sb_kernel, tq=tq, scale=HEAD_DIM ** -0.5),
        grid=(batch * nh, nq),
        in_specs=[pl.BlockSpec((None, tq, HEAD_DIM), lambda bh, i: (bh // nh, i, bh % nh)),
                  pl.BlockSpec((None, seq, HEAD_DIM), lambda bh, i: (bh // nh, 0, nh + bh % nh)),
                  pl.BlockSpec((None, seq, HEAD_DIM), lambda bh, i: (bh // nh, 0, 2 * nh + bh % nh))],
        out_specs=pl.BlockSpec((None, tq, HEAD_DIM), lambda bh, i: (bh // nh, i, bh % nh)),
        out_shape=jax.ShapeDtypeStruct((batch, seq, nh * HEAD_DIM), BF16),
        scratch_shapes=[pltpu.VMEM((tq, HEAD_DIM), F32), pltpu.VMEM((tq, 1), F32)],
        compiler_params=_cp("parallel", "arbitrary"),
        name="sb_attention",
    )(h, h, h)


DW_UNROLL = 8


def _dw_kernel(q_ref, k_ref, v_ref, o_ref, m_ref, l_ref, acc_ref, *, scale):
    nk = DW_KEYS
    t0 = pl.program_id(1) * DW_TILE
    ii = lax.broadcasted_iota(jnp.int32, (nk, 2 * nk), 0)
    jj = lax.broadcasted_iota(jnp.int32, (nk, 2 * nk), 1)
    band = jnp.logical_and(jj >= ii, jj <= ii + nk)
    band_prev = jnp.logical_and(band, jj < nk)
    band_own = jnp.logical_and(band, jj >= nk)
    neg = jnp.float32(-jnp.inf)

    for p_idx, (window, dil) in enumerate(DW_PATTERNS):
        assert window // dil == nk
        span = nk * dil
        n_steps = DW_TILE // nk

        def group(g, _, dil=dil, span=span, p_idx=p_idx):
            subs = range(DW_UNROLL)
            blk = [(g * DW_UNROLL + u) // dil for u in subs]
            res = [(g * DW_UNROLL + u) % dil for u in subs]
            off = [b * span + r for b, r in zip(blk, res)]
            rows = [pl.ds(o, nk, stride=dil) for o in off]
            cur = [pl.ds(t0 + o, nk, stride=dil) for o in off]
            has_prev = [(t0 + b * span) > 0 for b in blk]
            prev = [pl.ds(jnp.maximum(t0 + b * span - span, 0) + r, nk, stride=dil) for b, r in zip(blk, res)]
            qs = [q_ref[rw, :].astype(BF16) for rw in rows]
            keys = [jnp.concatenate([k_ref[pv, :], k_ref[cr, :]], axis=0).astype(BF16)
                    for pv, cr in zip(prev, cur)]
            sc = [_dot_nt(qq, kk) * scale for qq, kk in zip(qs, keys)]
            sc = [jnp.where(jnp.logical_or(band_own, jnp.logical_and(band_prev, hp)), s, neg)
                  for s, hp in zip(sc, has_prev)]
            m_new = [jnp.max(s, axis=1, keepdims=True) for s in sc]
            p = [jnp.exp(s - m) for s, m in zip(sc, m_new)]
            l_new = [jnp.sum(pp, axis=1, keepdims=True) for pp in p]
            vals = [jnp.concatenate([v_ref[pv, :], v_ref[cr, :]], axis=0).astype(BF16)
                    for pv, cr in zip(prev, cur)]
            num = [_dot(pp.astype(BF16), vv) for pp, vv in zip(p, vals)]
            for rw, m, l, nm in zip(rows, m_new, l_new, num):
                m_ref[p_idx, rw, :] = jnp.broadcast_to(m, (nk, HEAD_DIM))
                l_ref[p_idx, rw, :] = jnp.broadcast_to(l, (nk, HEAD_DIM))
                acc_ref[p_idx, rw, :] = nm
            return 0

        lax.fori_loop(0, n_steps // DW_UNROLL, group, 0)

    m_all = m_ref[...]
    m_tot = jnp.max(m_all, axis=0)
    w = jnp.exp(m_all - m_tot[None])
    den = jnp.sum(w * l_ref[...], axis=0)
    num = jnp.sum(w * acc_ref[...], axis=0)
    o_ref[...] = (num / den).astype(o_ref.dtype)


def _dw_attention(h, batch, seq):
    nh = N_HEADS_DW
    return pl.pallas_call(
        functools.partial(_dw_kernel, scale=HEAD_DIM ** -0.5),
        grid=(batch * nh, seq // DW_TILE),
        in_specs=[pl.BlockSpec((None, DW_TILE, HEAD_DIM), lambda bh, i: (bh // nh, i, bh % nh)),
                  pl.BlockSpec((None, seq, HEAD_DIM), lambda bh, i: (bh // nh, 0, nh + bh % nh)),
                  pl.BlockSpec((None, seq, HEAD_DIM), lambda bh, i: (bh // nh, 0, 2 * nh + bh % nh))],
        out_specs=pl.BlockSpec((None, DW_TILE, HEAD_DIM), lambda bh, i: (bh // nh, i, bh % nh)),
        out_shape=jax.ShapeDtypeStruct((batch, seq, nh * HEAD_DIM), BF16),
        scratch_shapes=[pltpu.VMEM((len(DW_PATTERNS), DW_TILE, HEAD_DIM), F32)] * 3,
        compiler_params=_cp("parallel", "arbitrary"),
        name="dw_attention",
    )(h, h, h)


def _s5_params(lam_re, lam_im, log_dt, b_re, b_im, c_re, c_im, n_scan):
    L = SSM_CHUNK
    gpl = LANES // SSM_GROUP
    nlb = SSM_GROUPS // gpl
    dt = jnp.exp(log_dt)[:, None]
    mag_log = lam_re * dt
    ang = lam_im * dt

    def power(n):
        n = jnp.asarray(n, F32)[..., None, None]
        mag = jnp.exp(mag_log * n)
        return mag * jnp.cos(ang * n), mag * jnp.sin(ang * n)

    lr, li = power(jnp.ones(()))
    den = lam_re * lam_re + lam_im * lam_im
    cr = ((lr - 1.0) * lam_re + li * lam_im) / den
    ci = (li * lam_re - (lr - 1.0) * lam_im) / den
    bbr = cr[..., None] * b_re - ci[..., None] * b_im
    bbi = cr[..., None] * b_im + ci[..., None] * b_re

    pr, pi = power(jnp.arange(L + 1, dtype=F32))
    mr = pr[:L, :, :, None] * bbr - pi[:L, :, :, None] * bbi
    mi = pr[:L, :, :, None] * bbi + pi[:L, :, :, None] * bbr
    kk = (jnp.einsum('ghp,tgpk->tghk', c_re, mr, precision=HIGHEST)
          - jnp.einsum('ghp,tgpk->tghk', c_im, mi, precision=HIGHEST))
    jj = jnp.arange(L)[:, None]
    ii = jnp.arange(L)[None, :]
    lag = ii - jj
    toe = jnp.where((lag >= 0)[:, :, None, None, None], kk[jnp.clip(lag, 0, L - 1)], 0.0)
    toe = toe.reshape(L, L, nlb, gpl, SSM_GROUP, SSM_GROUP)
    c_toe = toe.transpose(2, 0, 3, 5, 1, 4).reshape(nlb, L * LANES, L * SSM_GROUP)
    qr = pr[L - 1 - jnp.arange(L)]
    qi = pi[L - 1 - jnp.arange(L)]
    inr = (qr[..., None] * bbr - qi[..., None] * bbi).reshape(L, nlb, gpl, SSM_STATE, SSM_GROUP)
    ini = (qr[..., None] * bbi + qi[..., None] * bbr).reshape(L, nlb, gpl, SSM_STATE, SSM_GROUP)
    c_in = jnp.concatenate(
        [part.transpose(1, 0, 2, 4, 3).reshape(nlb, L * LANES, SSM_STATE) for part in (inr, ini)], axis=2)
    orr = c_re[None] * pr[1:, :, None, :] - c_im[None] * pi[1:, :, None, :]
    oii = c_re[None] * pi[1:, :, None, :] + c_im[None] * pr[1:, :, None, :]
    half = gpl * SSM_STATE
    c_out = jnp.concatenate(
        [part.reshape(L, nlb, gpl, SSM_GROUP, SSM_STATE).transpose(1, 2, 4, 0, 3)
         .reshape(nlb, half, L * SSM_GROUP) for part in (orr, -oii)], axis=1)
    ar, ai = power(float(L) * (2.0 ** jnp.arange(n_scan, dtype=F32)))
    ar = ar.reshape(n_scan, nlb, half).transpose(1, 0, 2)
    ai = ai.reshape(n_scan, nlb, half).transpose(1, 0, 2)
    return c_toe.astype(BF16), c_in.astype(BF16), c_out.astype(BF16), ar, ai


def _spread_matrix(inner, reps, period):
    rows = jnp.arange(inner * period)
    cols = jnp.arange(inner * reps * period)
    same_a = (rows[:, None] // period) == (cols[None, :] // (reps * period))
    same_c = (rows[:, None] % period) == (cols[None, :] % period)
    return jnp.logical_and(same_a, same_c).astype(BF16)


def _spread_groups(compact, spread, row_period, col_period):
    gpl = LANES // SSM_GROUP
    full = _dot(compact, spread)
    rg = (lax.broadcasted_iota(jnp.int32, full.shape, 0) // row_period) % gpl
    cg = (lax.broadcasted_iota(jnp.int32, full.shape, 1) // col_period) % gpl
    return jnp.where(rg == cg, full, 0.0).astype(BF16)


def _s5_kernel(x_ref, ct_ref, ci_ref, co_ref, et_ref, ei_ref, ar_ref, ai_ref, y_ref,
               wt_ref, wi_ref, wo_ref, sr_ref, si_ref, *, n_scan, m):
    L = SSM_CHUNK

    @pl.when(jnp.logical_and(pl.program_id(1) == 0, pl.program_id(2) == 0))
    def _():
        wt_ref[...] = _spread_groups(ct_ref[...], et_ref[...], SSM_GROUP, SSM_GROUP)
        wi_ref[...] = _spread_groups(ci_ref[...], ei_ref[...], SSM_GROUP, SSM_STATE)
        wo_ref[...] = _spread_groups(co_ref[...], et_ref[...], SSM_STATE, SSM_GROUP)

    @pl.when(pl.program_id(2) == 0)
    def _():
        sr_ref[...] = jnp.zeros_like(sr_ref)
        si_ref[...] = jnp.zeros_like(si_ref)

    xc = jnp.concatenate([x_ref[pl.ds(j, m, stride=L), :].astype(BF16) for j in range(L)], axis=1)
    yc = _dot(xc, wt_ref[...])
    z = _dot(xc, wi_ref[...])
    half = z.shape[1] // 2
    zr, zi = z[:, :half], z[:, half:]
    row = lax.broadcasted_iota(jnp.int32, (m, half), 0)
    pr, pi = sr_ref[0:1, :], si_ref[0:1, :]
    a1r, a1i = ar_ref[0:1, :], ai_ref[0:1, :]
    first = row == 0
    zr, zi = (zr + jnp.where(first, a1r * pr - a1i * pi, 0.0),
              zi + jnp.where(first, a1r * pi + a1i * pr, 0.0))
    for k in range(n_scan):
        s = 1 << k
        ar = ar_ref[k:k + 1, :]
        ai = ai_ref[k:k + 1, :]
        keep = row >= s
        tr = jnp.where(keep, pltpu.roll(zr, s, 0), 0.0)
        ti = jnp.where(keep, pltpu.roll(zi, s, 0), 0.0)
        zr, zi = zr + ar * tr - ai * ti, zi + ar * ti + ai * tr
    keep = row >= 1
    s_in = jnp.concatenate([jnp.where(keep, pltpu.roll(zr, 1, 0), pr),
                            jnp.where(keep, pltpu.roll(zi, 1, 0), pi)], axis=1)
    sr_ref[...] = jnp.broadcast_to(zr[m - 1:m, :], sr_ref.shape)
    si_ref[...] = jnp.broadcast_to(zi[m - 1:m, :], si_ref.shape)
    yc = yc + _dot(s_in.astype(BF16), wo_ref[...])
    for i in range(L):
        y_ref[pl.ds(i, m, stride=L), :] = yc[:, i * LANES:(i + 1) * LANES]


def _s5_scan(h1, lam_re, lam_im, log_dt, b_re, b_im, c_re, c_im, batch, seq, rows=2048):
    L = SSM_CHUNK
    rows = min(rows, seq)
    m = rows // L
    nt = seq // rows
    n_scan = max(1, math.ceil(math.log2(m)))
    c_toe, c_in, c_out, ar, ai = _s5_params(lam_re, lam_im, log_dt, b_re, b_im, c_re, c_im, n_scan)
    nlb, half = ar.shape[0], ar.shape[2]
    gpl = LANES // SSM_GROUP
    e_toe = _spread_matrix(L, gpl, SSM_GROUP)
    e_in = _spread_matrix(2, gpl, SSM_STATE)
    per_lb = lambda *blk: pl.BlockSpec((None,) + blk, lambda l, b, i: (l, 0, 0))
    whole = lambda a: pl.BlockSpec(a.shape, lambda l, b, i: (0, 0))
    tile = pl.BlockSpec((rows, LANES), lambda l, b, i: (b * nt + i, l))
    wide = L * LANES
    return pl.pallas_call(
        functools.partial(_s5_kernel, n_scan=n_scan, m=m),
        grid=(nlb, batch, nt),
        in_specs=[tile, per_lb(wide, c_toe.shape[2]), per_lb(wide, c_in.shape[2]),
                  per_lb(2 * half, c_out.shape[2]), whole(e_toe), whole(e_in),
                  per_lb(n_scan, half), per_lb(n_scan, half)],
        out_specs=tile,
        out_shape=jax.ShapeDtypeStruct((batch * seq, SSM_WIDTH), F32),
        scratch_shapes=[pltpu.VMEM((wide, wide), BF16), pltpu.VMEM((wide, 2 * half), BF16),
                        pltpu.VMEM((2 * half, wide), BF16)] + [pltpu.VMEM((SUBLANES, half), F32)] * 2,
        compiler_params=_cp("parallel", "arbitrary", "arbitrary"),
        name="s5_scan",
    )(h1, c_toe, c_in, c_out, e_toe, e_in, ar, ai)


def _s5_glu_kernel(y_ref, u_ref, d_ref, w_ref, b_ref, o_ref):
    y = y_ref[...] + d_ref[...] * u_ref[...]
    c = math.sqrt(2.0 / math.pi)
    z = 0.5 * y * (1.0 + jnp.tanh(c * (y + 0.044715 * (y * y * y))))
    gate = jax.nn.sigmoid(_dot(z.astype(BF16), w_ref[...]) + b_ref[...])
    o_ref[...] = (z * gate).astype(o_ref.dtype)


def _s5_glu(y, h1, d_skip, glu_w, glu_b, tm=512):
    t, w = y.shape
    row = lambda i: (i, 0)
    fixed = lambda i: (0, 0)
    return pl.pallas_call(
        _s5_glu_kernel,
        grid=(t // tm,),
        in_specs=[pl.BlockSpec((tm, w), row), pl.BlockSpec((tm, w), row),
                  pl.BlockSpec((1, w), fixed), pl.BlockSpec((w, w), fixed), pl.BlockSpec((1, w), fixed)],
        out_specs=pl.BlockSpec((tm, w), row),
        out_shape=jax.ShapeDtypeStruct((t, w), BF16),
        compiler_params=_cp("parallel"),
        name="s5_glu",
    )(y, h1, d_skip.reshape(1, w), glu_w, glu_b.reshape(1, w))


def _bdot(a, b):
    return _dot(a.astype(BF16), b.astype(BF16))


def _bdot_nt(a, b):
    return _dot_nt(a.astype(BF16), b.astype(BF16))


def _gdn_kernel(q_ref, k_ref, v_ref, gate_ref, ba_ref, wq_ref, wk_ref, wv_ref, alog_ref, dtb_ref,
                nw_ref, o_ref, state_ref, hq_ref, hk_ref, hv_ref, gct_ref, *, lt, nh, hp):
    d = GDN_D
    c = lt
    head0 = (pl.program_id(0) % (nh // hp)) * hp

    @pl.when(pl.program_id(1) == 0)
    def _():
        state_ref[...] = jnp.zeros_like(state_ref)
        for halo_ref in (hq_ref, hk_ref, hv_ref):
            halo_ref[0:SUBLANES, :] = jnp.zeros((SUBLANES, hp * d), F32)

    def conv_silu(x_ref, w_ref, xe_ref):
        w = w_ref[...]
        xe_ref[SUBLANES:, :] = x_ref[...].astype(F32)
        y = xe_ref[SUBLANES:, :] * w[GDN_CONV - 1:GDN_CONV, :]
        for back in range(1, GDN_CONV):
            y = y + xe_ref[pl.ds(SUBLANES - back, lt), :] * w[GDN_CONV - 1 - back:GDN_CONV - back, :]
        xe_ref[0:SUBLANES, :] = xe_ref[lt:, :]
        return y * jax.nn.sigmoid(y)

    def l2norm(t):
        return t * lax.rsqrt(jnp.sum(t * t, axis=-1, keepdims=True) + RMS_EPS)

    q_cs = conv_silu(q_ref, wq_ref, hq_ref)
    k_cs = conv_silu(k_ref, wk_ref, hk_ref)
    v_cs = conv_silu(v_ref, wv_ref, hv_ref)

    ba = ba_ref[...]
    lane = lax.broadcasted_iota(jnp.int32, ba.shape, 1)
    sig_ba = jax.nn.sigmoid(ba)
    g_full = -jnp.exp(alog_ref[...]) * _softplus(ba + dtb_ref[...])
    gc_full = g_full
    rows_i = lax.broadcasted_iota(jnp.int32, gc_full.shape, 0)
    step = 1
    while step < c:
        gc_full = gc_full + jnp.where(rows_i >= step, pltpu.roll(gc_full, step, 0), 0.0)
        step *= 2
    gct_ref[...] = gc_full.T

    ri = lax.broadcasted_iota(jnp.int32, (c, c), 0)
    ci = lax.broadcasted_iota(jnp.int32, (c, c), 1)
    tri = ri >= ci
    strict = ri > ci
    eye = (ri == ci).astype(F32)
    in16 = (ri // 16) == (ci // 16)
    merges = []
    width = 16
    while width < c:
        inner = (ri // width) == (ci // width)
        outer = (ri // (2 * width)) == (ci // (2 * width))
        merges.append(jnp.logical_and(outer, jnp.logical_not(inner)))
        width *= 2

    heads = range(hp)
    cols = [slice(hh * d, (hh + 1) * d) for hh in heads]
    q = [l2norm(q_cs[:, cols[hh]]) * (d ** -0.5) for hh in heads]
    k = [l2norm(k_cs[:, cols[hh]]) for hh in heads]
    v = [v_cs[:, cols[hh]] for hh in heads]
    beta = [jnp.sum(jnp.where(lane == head0 + hh, sig_ba, 0.0), axis=1, keepdims=True) for hh in heads]
    gc = [jnp.sum(jnp.where(lane == nh + head0 + hh, gc_full, 0.0), axis=1, keepdims=True)
          for hh in heads]
    gc_row = [gct_ref[pl.ds(nh + head0 + hh, 1), :] for hh in heads]
    gc_last = [gc_row[hh][:, c - 1:c] for hh in heads]
    decay = [jnp.exp(jnp.where(tri, gc[hh] - gc_row[hh], -jnp.inf)) for hh in heads]
    kb = [k[hh] * beta[hh] for hh in heads]
    kq_kt = [_bdot_nt(jnp.concatenate([kb[hh], q[hh]], axis=0), k[hh]) for hh in heads]
    lower = [jnp.where(strict, kq_kt[hh][:c] * decay[hh], 0.0) for hh in heads]
    pw = [jnp.where(in16, -lower[hh], 0.0) for hh in heads]
    inv = [eye + pw[hh] for hh in heads]
    for _ in range(3):
        pw = [_bdot(pw[hh], pw[hh]) for hh in heads]
        inv = [inv[hh] + _bdot(inv[hh], pw[hh]) for hh in heads]
    for off in merges:
        part = [_bdot(inv[hh], jnp.where(off, lower[hh], 0.0)) for hh in heads]
        inv = [inv[hh] - _bdot(part[hh], inv[hh]) for hh in heads]
    e_gc = [jnp.exp(gc[hh]) for hh in heads]
    uw = [_bdot(inv[hh], jnp.concatenate([v[hh] * beta[hh], kb[hh] * e_gc[hh]], axis=1)) for hh in heads]
    state = [state_ref[hh] for hh in heads]
    ws_qs = [_bdot(jnp.concatenate([uw[hh][:, d:], q[hh] * e_gc[hh]], axis=0), state[hh]) for hh in heads]
    v_new = [uw[hh][:, :d] - ws_qs[hh][:c] for hh in heads]
    o = [ws_qs[hh][c:] + _bdot(kq_kt[hh][c:] * decay[hh], v_new[hh]) for hh in heads]
    for hh in heads:
        k_dec = k[hh] * jnp.exp(gc_last[hh] - gc[hh])
        state_ref[hh] = (state[hh] * jnp.exp(gc_last[hh])
                         + _dot_tn(k_dec.astype(BF16), v_new[hh].astype(BF16)))
        on = o[hh] * lax.rsqrt(jnp.mean(o[hh] * o[hh], axis=-1, keepdims=True) + RMS_EPS) * nw_ref[...]
        gt = gate_ref[:, cols[hh]].astype(F32)
        o_ref[:, cols[hh]] = (on * (gt * jax.nn.sigmoid(gt))).astype(o_ref.dtype)


def _gated_deltanet(h1, ba, conv_w, a_log_pad, dt_bias_pad, norm_w, batch, seq, lt=GDN_TILE, hp=8):
    nh = N_HEADS_GDN
    d = GDN_D
    t = batch * seq
    nt = seq // lt
    ng = nh // hp
    wide = hp * d
    base = 0
    rows = lambda off: (lambda bg, i: ((bg // ng) * nt + i, off + bg % ng))
    cw = lambda off: (lambda bg, i: (0, off + bg % ng))
    fixed = lambda bg, i: (0, 0)
    return pl.pallas_call(
        functools.partial(_gdn_kernel, lt=lt, nh=nh, hp=hp),
        grid=(batch * ng, nt),
        in_specs=[pl.BlockSpec((lt, wide), rows(base)), pl.BlockSpec((lt, wide), rows(base + ng)),
                  pl.BlockSpec((lt, wide), rows(base + 2 * ng)), pl.BlockSpec((lt, wide), rows(base + 3 * ng)),
                  pl.BlockSpec((lt, LANES), lambda bg, i: ((bg // ng) * nt + i, 0)),
                  pl.BlockSpec((GDN_CONV, wide), cw(0)), pl.BlockSpec((GDN_CONV, wide), cw(ng)),
                  pl.BlockSpec((GDN_CONV, wide), cw(2 * ng)),
                  pl.BlockSpec((1, LANES), fixed), pl.BlockSpec((1, LANES), fixed),
                  pl.BlockSpec((1, d), fixed)],
        out_specs=pl.BlockSpec((lt, wide), rows(0)),
        out_shape=jax.ShapeDtypeStruct((t, nh * d), BF16),
        scratch_shapes=[pltpu.VMEM((hp, d, d), F32)] + [pltpu.VMEM((SUBLANES + lt, wide), F32)] * 3
                       + [pltpu.VMEM((LANES, lt), F32)],
        compiler_params=_cp("parallel", "arbitrary"),
        name="gated_deltanet",
    )(h1, h1, h1, h1, ba, conv_w, conv_w, conv_w, a_log_pad, dt_bias_pad, norm_w.reshape(1, d))


DMA_ISSUE_UNROLL = 8


def _moe_kernel(te_ref, nv_ref, rows_ref, src_ref, x_hbm, w1_ref, w3_ref, w2_ref, gate_ref, o_ref,
                slab_ref, xb_ref, acc_ref, sem, *, nj, tm, n, per_step):
    i = pl.program_id(0)
    j = pl.program_id(1)
    n_used = nv_ref[0]
    valid = i < n_used

    def start_rows(tile, first, count):
        for k in range(count):
            r = first + k
            tok = src_ref[tile * tm + jnp.minimum(r, tm - 1)]
            pltpu.make_async_copy(x_hbm.at[pl.ds(pl.multiple_of(tok * n, n), n), :],
                                  slab_ref.at[pl.ds(pl.multiple_of(r * n, n), n), :], sem).start()

    @pl.when(jnp.logical_and(i == 0, j == 0))
    def _():
        def body(s, _):
            start_rows(0, s * per_step, per_step)
            return 0
        lax.fori_loop(0, nj, body, 0)

    @pl.when(j == 0)
    def _():
        acc_ref[...] = jnp.zeros_like(acc_ref)

    @pl.when(jnp.logical_and(j == 0, i <= n_used))
    def _():
        pltpu.make_async_copy(x_hbm.at[pl.ds(0, nj * per_step * n), :], slab_ref, sem).wait()
        xb_ref[...] = _load_row_slabs(slab_ref, tm, n).astype(BF16)

    half = tm // 2
    lower_only = rows_ref[i] <= half

    @pl.when(jnp.logical_and(valid, jnp.logical_not(lower_only)))
    def _():
        start_rows(i + 1, j * per_step, per_step)
        _swiglu_acc(xb_ref[...], w1_ref, w3_ref, w2_ref, acc_ref)

    @pl.when(jnp.logical_and(valid, lower_only))
    def _():
        start_rows(i + 1, j * per_step, per_step)
        _swiglu_acc(xb_ref[0:half, :], w1_ref, w3_ref, w2_ref, acc_ref.at[0:half, :])

    @pl.when(j == nj - 1)
    def _():
        _store_row_slabs(o_ref, acc_ref[...] * gate_ref[...])


def _moe_ffn(x_slabs, src, gates, tile_expert, n_valid, tile_rows, w1, w3, w2, tm, tf):
    p = src.shape[0]
    d, dff = w1.shape[1], w1.shape[2]
    n = d // LANES
    nj = dff // tf
    per_step = -(-tm // nj)
    jsel = lambda i, j, nv: jnp.where(i < nv[0], j, nj - 1)
    return pl.pallas_call(
        functools.partial(_moe_kernel, nj=nj, tm=tm, n=n, per_step=per_step),
        grid_spec=pltpu.PrefetchScalarGridSpec(
            num_scalar_prefetch=4, grid=(p // tm, nj),
            in_specs=[pl.BlockSpec(memory_space=pl.ANY),
                      pl.BlockSpec((None, d, tf), lambda i, j, te, nv, rows, src: (te[i], 0, jsel(i, j, nv))),
                      pl.BlockSpec((None, d, tf), lambda i, j, te, nv, rows, src: (te[i], 0, jsel(i, j, nv))),
                      pl.BlockSpec((None, tf, d), lambda i, j, te, nv, rows, src: (te[i], jsel(i, j, nv), 0)),
                      pl.BlockSpec((tm, 1), lambda i, j, te, nv, rows, src: (i, 0))],
            out_specs=pl.BlockSpec((tm * n, LANES), lambda i, j, te, nv, rows, src: (i, 0)),
            scratch_shapes=[pltpu.VMEM((nj * per_step * n, LANES), F32), pltpu.VMEM((tm, d), BF16),
                            pltpu.VMEM((tm, d), F32), pltpu.SemaphoreType.DMA(())]),
        out_shape=jax.ShapeDtypeStruct((p * n, LANES), F32),
        compiler_params=_cp("arbitrary", "arbitrary"),
        name="moe_ffn",
    )(tile_expert, n_valid, tile_rows, src, x_slabs, w1, w3, w2, gates)


def _combine_kernel(pos_ref, ys_hbm, res_ref, g_ref, b_ref, o_ref, buf_ref, sem, *, tm, t, n):
    i = pl.program_id(0)

    def issue(step, slot):
        def body(r, _):
            for choice in range(2):
                p = pos_ref[choice * t + step * tm + r]
                pltpu.make_async_copy(ys_hbm.at[pl.ds(pl.multiple_of(p * n, n), n), :],
                                      buf_ref.at[slot, choice, pl.ds(pl.multiple_of(r * n, n), n), :],
                                      sem.at[slot, choice]).start()
            return 0
        lax.fori_loop(0, tm, body, 0, unroll=DMA_ISSUE_UNROLL)

    @pl.when(i == 0)
    def _():
        issue(0, 0)

    @pl.when(i + 1 < pl.num_programs(0))
    def _():
        issue(i + 1, (i + 1) % 2)

    slot = i % 2
    for choice in range(2):
        pltpu.make_async_copy(ys_hbm.at[pl.ds(0, tm * n), :], buf_ref.at[slot, choice],
                              sem.at[slot, choice]).wait()
    y = _load_row_slabs(buf_ref.at[slot, 0], tm, n) + _load_row_slabs(buf_ref.at[slot, 1], tm, n)
    o_ref[...] = _layer_norm(DEEPNORM_ALPHA * res_ref[...] + y, g_ref[...], b_ref[...])


def _moe_combine_ln(ys, pos, res, g, b, tm=256):
    t, d = res.shape
    n = d // LANES
    return pl.pallas_call(
        functools.partial(_combine_kernel, tm=tm, t=t, n=n),
        grid_spec=pltpu.PrefetchScalarGridSpec(
            num_scalar_prefetch=1, grid=(t // tm,),
            in_specs=[pl.BlockSpec(memory_space=pl.ANY),
                      pl.BlockSpec((tm, d), lambda i, pos: (i, 0)),
                      pl.BlockSpec((1, d), lambda i, pos: (0, 0)),
                      pl.BlockSpec((1, d), lambda i, pos: (0, 0))],
            out_specs=pl.BlockSpec((tm, d), lambda i, pos: (i, 0)),
            scratch_shapes=[pltpu.VMEM((2, 2, tm * n, LANES), F32), pltpu.SemaphoreType.DMA((2, 2))]),
        out_shape=jax.ShapeDtypeStruct((t, d), F32),
        compiler_params=_cp("arbitrary"),
        name="moe_combine_ln",
    )(pos, ys, res, g.reshape(1, d), b.reshape(1, d))


def _moe_routing(route, tm):
    t = route.shape[0]
    e = N_EXPERTS
    idx = route[:, 0:2].astype(jnp.int32)
    wts = route[:, 2:4]
    flat_e = idx.T.reshape(-1)
    onehot = (flat_e[:, None] == jnp.arange(e, dtype=jnp.int32)[None, :]).astype(jnp.int32)
    rank = jnp.cumsum(onehot, axis=0) - onehot
    counts = jnp.sum(onehot, axis=0)
    tiles = (counts + tm - 1) // tm
    tile_end = jnp.cumsum(tiles)
    start = (tile_end - tiles) * tm
    pos = jnp.sum(onehot * (start[None, :] + rank), axis=1)
    n_slots = 2 * t + e * tm
    n_tiles = n_slots // tm
    owner = jnp.full((n_slots,), -1, jnp.int32).at[pos].set(jnp.arange(2 * t, dtype=jnp.int32))
    used = owner >= 0
    src = jnp.where(used, owner % t, jnp.arange(n_slots, dtype=jnp.int32) % t)
    gates = jnp.where(used, wts.T.reshape(-1)[jnp.maximum(owner, 0)], 0.0)
    n_valid = tile_end[-1]
    tile_ids = jnp.arange(n_tiles, dtype=jnp.int32)
    tile_expert = jnp.sum((tile_ids[:, None] >= tile_end[None, :]).astype(jnp.int32), axis=1)
    last_expert = jnp.sum((n_valid - 1 >= tile_end).astype(jnp.int32))
    tile_expert = jnp.where(tile_ids < n_valid, tile_expert, last_expert).astype(jnp.int32)
    tile_rows = jnp.clip((start + counts)[tile_expert] - tile_ids * tm, 0, tm)
    tile_rows = jnp.where(tile_ids < n_valid, tile_rows, 0).astype(jnp.int32)
    return (src, gates.reshape(n_slots, 1), pos.astype(jnp.int32), tile_expert,
            n_valid.reshape(1).astype(jnp.int32), tile_rows)


def _even_layer(x, xb, batch, seq, w_in, w_out, ln_mix_g, ln_mix_b, w1, w3, w2, ln_ffn_g, ln_ffn_b):
    t, d = x.shape
    w_sb = N_HEADS_SB * HEAD_DIM
    w_in_b = w_in.astype(BF16)
    ha = _matmul(xb, w_in_b[:, :3 * w_sb], BF16, 1024, 1024, "in_proj_sb")
    hb = _matmul(xb, w_in_b[:, 3 * w_sb:], F32, 1024, 1024, "in_proj_dw")
    oa = _sb_attention(ha.reshape(batch, seq, -1), batch, seq).reshape(t, -1)
    ob = _dw_attention(hb.reshape(batch, seq, -1), batch, seq).reshape(t, -1)
    w_out_b = w_out.astype(BF16)
    x, xb = _proj_ln(oa, ob, w_out_b[:w_sb], w_out_b[w_sb:], x, ln_mix_g, ln_mix_b)
    return _ffn_ln(xb, x, w1.astype(BF16), w3.astype(BF16), w2.astype(BF16), ln_ffn_g, ln_ffn_b)


def _odd_layer(x, xb, batch, seq, w_in, lam_re, lam_im, log_dt, b_re, b_im, c_re, c_im, d_skip,
               glu_w, glu_b, conv_w, a_log, dt_bias, norm_w, w_out, ln_mix_g, ln_mix_b,
               router_w, w1, w3, w2, ln_ffn_g, ln_ffn_b):
    t, d = x.shape
    nh = N_HEADS_GDN
    wide = SSM_WIDTH + 4 * nh * GDN_D
    w_in_b = w_in[:, :wide].astype(BF16)
    u = _matmul(xb, w_in_b[:, :SSM_WIDTH], F32, 1024, 1024, "in_proj_ssm")
    h1 = _matmul(xb, w_in_b[:, SSM_WIDTH:], BF16, 1024, 1024, "in_proj_gdn")
    w_small = jnp.pad(w_in[:, wide:], ((0, 0), (0, LANES - 2 * nh)))
    ba = _matmul(x, _split_hi_lo(w_small), F32, 512, LANES, "in_proj_gates")

    y = _s5_scan(u, lam_re, lam_im, log_dt, b_re, b_im, c_re, c_im, batch, seq)
    oc = _s5_glu(y, u, d_skip, glu_w.astype(BF16), glu_b)

    pad_hi = LANES - 2 * nh
    a_log_pad = jnp.pad(a_log, (nh, pad_hi)).reshape(1, LANES)
    dt_bias_pad = jnp.pad(dt_bias, (nh, pad_hi)).reshape(1, LANES)
    od = _gated_deltanet(h1, ba, conv_w, a_log_pad, dt_bias_pad, norm_w, batch, seq)

    w_out_b = w_out.astype(BF16)
    rw = _split_hi_lo(jnp.pad(router_w, ((0, 0), (0, LANES - N_EXPERTS))))
    x, x_slabs, route = _proj_ln(oc, od, w_out_b[:SSM_WIDTH], w_out_b[SSM_WIDTH:], x, ln_mix_g, ln_mix_b,
                                 router_w=rw)

    tm = 1024
    src, gates, pos, tile_expert, n_valid, tile_rows = _moe_routing(route, tm)
    ys = _moe_ffn(x_slabs, src, gates, tile_expert, n_valid, tile_rows, w1, w3, w2, tm, tf=256)
    return _moe_combine_ln(ys, pos, x, ln_ffn_g, ln_ffn_b)


def kernel(x, even_w_in, even_w_out, even_ln_mix_g, even_ln_mix_b, even_ffn_w1, even_ffn_w3, even_ffn_w2, even_ln_ffn_g, even_ln_ffn_b, odd_w_in, odd_ssm_lam_re, odd_ssm_lam_im, odd_ssm_log_dt, odd_ssm_b_re, odd_ssm_b_im, odd_ssm_c_re, odd_ssm_c_im, odd_ssm_d, odd_glu_w, odd_glu_b, odd_gdn_conv_w, odd_gdn_a_log, odd_gdn_dt_bias, odd_gdn_norm_w, odd_w_out, odd_ln_mix_g, odd_ln_mix_b, odd_router_w, odd_moe_w1, odd_moe_w3, odd_moe_w2, odd_ln_ffn_g, odd_ln_ffn_b):
    batch, seq, d = x.shape
    xf = x.reshape(batch * seq, d)
    xf, xb = _even_layer(xf, xf.astype(BF16), batch, seq, even_w_in[0], even_w_out[0],
                         even_ln_mix_g[0], even_ln_mix_b[0], even_ffn_w1[0], even_ffn_w3[0],
                         even_ffn_w2[0], even_ln_ffn_g[0], even_ln_ffn_b[0])
    out = _odd_layer(xf, xb, batch, seq, odd_w_in[0], odd_ssm_lam_re[0], odd_ssm_lam_im[0],
                     odd_ssm_log_dt[0], odd_ssm_b_re[0], odd_ssm_b_im[0], odd_ssm_c_re[0],
                     odd_ssm_c_im[0], odd_ssm_d[0], odd_glu_w[0], odd_glu_b[0], odd_gdn_conv_w[0],
                     odd_gdn_a_log[0], odd_gdn_dt_bias[0], odd_gdn_norm_w[0], odd_w_out[0],
                     odd_ln_mix_g[0], odd_ln_mix_b[0], odd_router_w[0], odd_moe_w1[0],
                     odd_moe_w3[0], odd_moe_w2[0], odd_ln_ffn_g[0], odd_ln_ffn_b[0])
    return out.reshape(batch, seq, d)
```

```python
import functools
import math

import jax
import jax.numpy as jnp
from jax import lax
from jax.experimental import pallas as pl
from jax.experimental.pallas import tpu as pltpu

F32 = jnp.float32
BF16 = jnp.bfloat16
HIGHEST = lax.Precision.HIGHEST

HEAD_DIM = 128
N_HEADS_SB = 8
N_HEADS_DW = 8
DW_PATTERNS = ((128, 1), (512, 4), (2048, 16))
DW_KEYS = 128
DW_TILE = 2048
SSM_WIDTH = 1024
SSM_GROUP = 16
SSM_GROUPS = 64
SSM_STATE = 64
SSM_CHUNK = 8
N_HEADS_GDN = 8
GDN_D = 128
GDN_CONV = 4
GDN_TILE = 256
N_EXPERTS = 8
DEPTH = 2
DEEPNORM_ALPHA = (2 * DEPTH) ** 0.25
LN_EPS = 1e-5
RMS_EPS = 1e-6

LANES = 128
SUBLANES = 8
VMEM_LIMIT = 56 * 1024 * 1024
SB_SKIP_LOG = -104.0


def _cp(*sem):
    return pltpu.CompilerParams(dimension_semantics=sem, vmem_limit_bytes=VMEM_LIMIT)


def _layer_norm(y, g, b):
    mu = jnp.mean(y, axis=-1, keepdims=True)
    yc = y - mu
    var = jnp.mean(yc * yc, axis=-1, keepdims=True)
    return yc * lax.rsqrt(var + LN_EPS) * g + b


def _softplus(x):
    return jnp.maximum(x, 0.0) + jnp.log1p(jnp.exp(-jnp.abs(x)))


def _dot(a, b, precision=None):
    return jnp.dot(a, b, preferred_element_type=F32, precision=precision)


def _dot_nt(a, b, precision=None):
    return lax.dot_general(a, b, (((1,), (1,)), ((), ())),
                           preferred_element_type=F32, precision=precision)


def _dot_tn(a, b, precision=None):
    return lax.dot_general(a, b, (((0,), (0,)), ((), ())),
                           preferred_element_type=F32, precision=precision)


def _split_hi_lo(w):
    hi = w.astype(BF16)
    return jnp.stack([hi, (w - hi.astype(F32)).astype(BF16)])


def _dot_split(x, w_ref):
    xh = x.astype(BF16)
    xl = (x - xh.astype(F32)).astype(BF16)
    wh = w_ref[0]
    return _dot(xh, wh) + _dot(xl, wh) + _dot(xh, w_ref[1])


def _mm_kernel(a_ref, b_ref, o_ref):
    o_ref[...] = _dot(a_ref[...], b_ref[...]).astype(o_ref.dtype)


def _mm_split_kernel(a_ref, b_ref, o_ref):
    o_ref[...] = _dot_split(a_ref[...], b_ref).astype(o_ref.dtype)


def _matmul(a, b, out_dtype, tm, tn, name):
    m, k = a.shape
    n = b.shape[-1]
    if b.ndim == 3:
        body, b_spec = _mm_split_kernel, pl.BlockSpec((2, k, tn), lambda i, j: (0, 0, j))
    else:
        body, b_spec = _mm_kernel, pl.BlockSpec((k, tn), lambda i, j: (0, j))
    return pl.pallas_call(
        body,
        grid=(m // tm, n // tn),
        in_specs=[pl.BlockSpec((tm, k), lambda i, j: (i, 0)), b_spec],
        out_specs=pl.BlockSpec((tm, tn), lambda i, j: (i, j)),
        out_shape=jax.ShapeDtypeStruct((m, n), out_dtype),
        compiler_params=_cp("parallel", "parallel"),
        name=name,
    )(a, b)


def _proj_ln_kernel(a0_ref, a1_ref, w0_ref, w1_ref, res_ref, g_ref, b_ref, *rest, with_router):
    if with_router:
        rw_ref, of_ref, rt_ref = rest
    else:
        of_ref, ob_ref = rest
    mix = _dot(a0_ref[...], w0_ref[...]) + _dot(a1_ref[...], w1_ref[...])
    xn = _layer_norm(DEEPNORM_ALPHA * res_ref[...] + mix, g_ref[...], b_ref[...])
    of_ref[...] = xn
    if with_router:
        logits = _dot_split(xn, rw_ref)
        lane = lax.broadcasted_iota(jnp.int32, logits.shape, 1).astype(F32)
        neg = jnp.float32(-jnp.inf)
        lg = jnp.where(lane < N_EXPERTS, logits, neg)
        m1 = jnp.max(lg, axis=1, keepdims=True)
        i1 = jnp.min(jnp.where(lg == m1, lane, float(LANES)), axis=1, keepdims=True)
        lg2 = jnp.where(lane == i1, neg, lg)
        m2 = jnp.max(lg2, axis=1, keepdims=True)
        i2 = jnp.min(jnp.where(lg2 == m2, lane, float(LANES)), axis=1, keepdims=True)
        e2 = jnp.exp(m2 - m1)
        p1 = 1.0 / (1.0 + e2)
        p2 = e2 * p1
        rt = jnp.where(lane == 0.0, i1, jnp.where(lane == 1.0, i2,
                       jnp.where(lane == 2.0, p1, jnp.where(lane == 3.0, p2, 0.0))))
        rt_ref[...] = rt
    else:
        ob_ref[...] = xn.astype(BF16)


def _proj_ln(a0, a1, w0, w1, res, g, b, router_w=None, tm=256):
    t, d = res.shape
    k0, k1 = a0.shape[1], a1.shape[1]
    with_router = router_w is not None
    row = lambda i: (i, 0)
    fixed = lambda i: (0, 0)
    in_specs = [pl.BlockSpec((tm, k0), row), pl.BlockSpec((tm, k1), row),
                pl.BlockSpec((k0, d), fixed), pl.BlockSpec((k1, d), fixed),
                pl.BlockSpec((tm, d), row), pl.BlockSpec((1, d), fixed), pl.BlockSpec((1, d), fixed)]
    args = [a0, a1, w0, w1, res, g.reshape(1, d), b.reshape(1, d)]
    if with_router:
        in_specs.append(pl.BlockSpec((2, d, LANES), lambda i: (0, 0, 0)))
        out_specs = [pl.BlockSpec((tm, d), row), pl.BlockSpec((tm, LANES), row)]
        out_shape = [jax.ShapeDtypeStruct((t, d), F32), jax.ShapeDtypeStruct((t, LANES), F32)]
        args.append(router_w)
    else:
        out_specs = [pl.BlockSpec((tm, d), row), pl.BlockSpec((tm, d), row)]
        out_shape = [jax.ShapeDtypeStruct((t, d), F32), jax.ShapeDtypeStruct((t, d), BF16)]
    return pl.pallas_call(
        functools.partial(_proj_ln_kernel, with_router=with_router),
        grid=(t // tm,), in_specs=in_specs, out_specs=out_specs, out_shape=out_shape,
        compiler_params=_cp("parallel"),
        name="proj_ln_router" if with_router else "proj_ln",
    )(*args)


def _swiglu_acc(x, w1_ref, w3_ref, w2_ref, acc_ref):
    h1 = _dot(x, w1_ref[...].astype(BF16))
    h3 = _dot(x, w3_ref[...].astype(BF16))
    act = (h1 * jax.nn.sigmoid(h1) * h3).astype(BF16)
    acc_ref[...] += _dot(act, w2_ref[...].astype(BF16))


def _ffn_kernel(x_ref, w1_ref, w3_ref, w2_ref, res_ref, g_ref, b_ref, of_ref, ob_ref, acc_ref, *, nj):
    j = pl.program_id(1)

    @pl.when(j == 0)
    def _():
        acc_ref[...] = jnp.zeros_like(acc_ref)

    _swiglu_acc(x_ref[...], w1_ref, w3_ref, w2_ref, acc_ref)

    @pl.when(j == nj - 1)
    def _():
        xn = _layer_norm(DEEPNORM_ALPHA * res_ref[...] + acc_ref[...], g_ref[...], b_ref[...])
        of_ref[...] = xn
        ob_ref[...] = xn.astype(BF16)


def _ffn_ln(xb, res, w1, w3, w2, g, b, tm=512, tf=512):
    t, d = res.shape
    dff = w1.shape[1]
    nj = dff // tf
    row = lambda i, j: (i, 0)
    fixed = lambda i, j: (0, 0)
    return pl.pallas_call(
        functools.partial(_ffn_kernel, nj=nj),
        grid=(t // tm, nj),
        in_specs=[pl.BlockSpec((tm, d), row),
                  pl.BlockSpec((d, tf), lambda i, j: (0, j)),
                  pl.BlockSpec((d, tf), lambda i, j: (0, j)),
                  pl.BlockSpec((tf, d), lambda i, j: (j, 0)),
                  pl.BlockSpec((tm, d), row), pl.BlockSpec((1, d), fixed), pl.BlockSpec((1, d), fixed)],
        out_specs=[pl.BlockSpec((tm, d), row), pl.BlockSpec((tm, d), row)],
        out_shape=[jax.ShapeDtypeStruct((t, d), F32), jax.ShapeDtypeStruct((t, d), BF16)],
        scratch_shapes=[pltpu.VMEM((tm, d), F32)],
        compiler_params=_cp("parallel", "arbitrary"),
        name="ffn_ln",
    )(xb, w1, w3, w2, res, g.reshape(1, d), b.reshape(1, d))


SB_QTILES = 2


def _sb_kernel(q_ref, k_ref, v_ref, o_ref, acc_ref, carry_ref, *, tq, scale):
    first_tile = pl.program_id(1) * SB_QTILES
    row = lax.broadcasted_iota(jnp.int32, (tq, tq), 0)
    col = lax.broadcasted_iota(jnp.int32, (tq, tq), 1)
    later_sum = (row > col).astype(BF16)
    past = col < row
    qs = [q_ref[t * tq:(t + 1) * tq, :] for t in range(SB_QTILES)]

    def blocks_terms(chains):
        starts = [pl.multiple_of(kb * tq, tq) for _, kb, _ in chains]
        z = [_dot_nt(q, k_ref[pl.ds(s, tq), :]) * scale for (q, _, _), s in zip(chains, starts)]
        lk = [-_softplus(zz) for zz in z]
        lk = [jnp.where(past, l, 0.0) if dg else l for l, (_, _, dg) in zip(lk, chains)]
        hi = [l.astype(BF16) for l in lk]
        lo = [(l - h.astype(F32)).astype(BF16) for l, h in zip(lk, hi)]
        later = [_dot(h, later_sum) + _dot(l, later_sum) for h, l in zip(hi, lo)]
        logw = [zz + l + lt for zz, l, lt in zip(z, lk, later)]
        tot = [jnp.sum(l, axis=1, keepdims=True) for l in lk]
        return logw, tot, [v_ref[pl.ds(s, tq), :] for s in starts]

    chains = []
    for t in range(SB_QTILES):
        chains += [(qs[t], first_tile + t, True), (qs[t], jnp.maximum(first_tile + t - 1, 0), False)]
    logw, tot, vals = blocks_terms(chains)
    for t in range(SB_QTILES):
        has_prev = first_tile + t >= 1
        w0 = jnp.where(past, jnp.exp(logw[2 * t]), 0.0)
        w1 = jnp.where(has_prev, jnp.exp(logw[2 * t + 1] + tot[2 * t]), 0.0)
        acc_ref[t] = _dot(w0.astype(BF16), vals[2 * t]) + _dot(w1.astype(BF16), vals[2 * t + 1])
        carry_ref[t] = tot[2 * t] + jnp.where(has_prev, tot[2 * t + 1], 0.0)

    for t in range(SB_QTILES):
        def cond(kb, t=t):
            return jnp.logical_and(kb >= 0, jnp.max(carry_ref[t]) > SB_SKIP_LOG)

        def body(kb, t=t):
            (lw,), (tt,), (v,) = blocks_terms([(qs[t], kb, False)])
            acc_ref[t] += _dot(jnp.exp(lw + carry_ref[t]).astype(BF16), v)
            carry_ref[t] += tt
            return kb - 1

        lax.while_loop(cond, body, first_tile + t - 2)
        o_ref[t * tq:(t + 1) * tq, :] = acc_ref[t].astype(o_ref.dtype)


def _sb_attention(h, batch, seq, tq=256):
    nh = N_HEADS_SB
    rows = SB_QTILES * tq
    return pl.pallas_call(
        functools.partial(_sb_kernel, tq=tq, scale=HEAD_DIM ** -0.5),
        grid=(batch * nh, seq // rows),
        in_specs=[pl.BlockSpec((None, rows, HEAD_DIM), lambda bh, i: (bh // nh, i, bh % nh)),
                  pl.BlockSpec((None, seq, HEAD_DIM), lambda bh, i: (bh // nh, 0, nh + bh % nh)),
                  pl.BlockSpec((None, seq, HEAD_DIM), lambda bh, i: (bh // nh, 0, 2 * nh + bh % nh))],
        out_specs=pl.BlockSpec((None, rows, HEAD_DIM), lambda bh, i: (bh // nh, i, bh % nh)),
        out_shape=jax.ShapeDtypeStruct((batch, seq, nh * HEAD_DIM), BF16),
        scratch_shapes=[pltpu.VMEM((SB_QTILES, tq, HEAD_DIM), F32), pltpu.VMEM((SB_QTILES, tq, 1), F32)],
        compiler_params=_cp("parallel", "arbitrary"),
        name="sb_attention",
    )(h, h, h)


DW_UNROLL = 8


def _dw_kernel(q_ref, k_ref, v_ref, o_ref, m_ref, l_ref, acc_ref, *, scale):
    nk = DW_KEYS
    t0 = pl.program_id(1) * DW_TILE
    ii = lax.broadcasted_iota(jnp.int32, (nk, 2 * nk), 0)
    jj = lax.broadcasted_iota(jnp.int32, (nk, 2 * nk), 1)
    band = jnp.logical_and(jj >= ii, jj <= ii + nk)
    band_prev = jnp.logical_and(band, jj < nk)
    band_own = jnp.logical_and(band, jj >= nk)
    neg = jnp.float32(-jnp.inf)

    for p_idx, (window, dil) in enumerate(DW_PATTERNS):
        assert window // dil == nk
        span = nk * dil
        n_steps = DW_TILE // nk

        def group(g, _, dil=dil, span=span, p_idx=p_idx):
            subs = range(DW_UNROLL)
            blk = [(g * DW_UNROLL + u) // dil for u in subs]
            res = [(g * DW_UNROLL + u) % dil for u in subs]
            off = [b * span + r for b, r in zip(blk, res)]
            rows = [pl.ds(o, nk, stride=dil) for o in off]
            cur = [pl.ds(t0 + o, nk, stride=dil) for o in off]
            has_prev = [(t0 + b * span) > 0 for b in blk]
            prev = [pl.ds(jnp.maximum(t0 + b * span - span, 0) + r, nk, stride=dil) for b, r in zip(blk, res)]
            qs = [q_ref[rw, :].astype(BF16) for rw in rows]
            keys = [jnp.concatenate([k_ref[pv, :], k_ref[cr, :]], axis=0).astype(BF16)
                    for pv, cr in zip(prev, cur)]
            sc = [_dot_nt(qq, kk) * scale for qq, kk in zip(qs, keys)]
            sc = [jnp.where(jnp.logical_or(band_own, jnp.logical_and(band_prev, hp)), s, neg)
                  for s, hp in zip(sc, has_prev)]
            m_new = [jnp.max(s, axis=1, keepdims=True) for s in sc]
            p = [jnp.exp(s - m) for s, m in zip(sc, m_new)]
            l_new = [jnp.sum(pp, axis=1, keepdims=True) for pp in p]
            vals = [jnp.concatenate([v_ref[pv, :], v_ref[cr, :]], axis=0).astype(BF16)
                    for pv, cr in zip(prev, cur)]
            num = [_dot(pp.astype(BF16), vv) for pp, vv in zip(p, vals)]
            for rw, m, l, nm in zip(rows, m_new, l_new, num):
                m_ref[p_idx, rw, :] = jnp.broadcast_to(m, (nk, HEAD_DIM))
                l_ref[p_idx, rw, :] = jnp.broadcast_to(l, (nk, HEAD_DIM))
                acc_ref[p_idx, rw, :] = nm
            return 0

        lax.fori_loop(0, n_steps // DW_UNROLL, group, 0)

    m_all = m_ref[...]
    m_tot = jnp.max(m_all, axis=0)
    w = jnp.exp(m_all - m_tot[None])
    den = jnp.sum(w * l_ref[...], axis=0)
    num = jnp.sum(w * acc_ref[...], axis=0)
    o_ref[...] = (num / den).astype(o_ref.dtype)


def _dw_attention(h, batch, seq):
    nh = N_HEADS_DW
    return pl.pallas_call(
        functools.partial(_dw_kernel, scale=HEAD_DIM ** -0.5),
        grid=(batch * nh, seq // DW_TILE),
        in_specs=[pl.BlockSpec((None, DW_TILE, HEAD_DIM), lambda bh, i: (bh // nh, i, bh % nh)),
                  pl.BlockSpec((None, seq, HEAD_DIM), lambda bh, i: (bh // nh, 0, nh + bh % nh)),
                  pl.BlockSpec((None, seq, HEAD_DIM), lambda bh, i: (bh // nh, 0, 2 * nh + bh % nh))],
        out_specs=pl.BlockSpec((None, DW_TILE, HEAD_DIM), lambda bh, i: (bh // nh, i, bh % nh)),
        out_shape=jax.ShapeDtypeStruct((batch, seq, nh * HEAD_DIM), BF16),
        scratch_shapes=[pltpu.VMEM((len(DW_PATTERNS), DW_TILE, HEAD_DIM), F32)] * 3,
        compiler_params=_cp("parallel", "arbitrary"),
        name="dw_attention",
    )(h, h, h)


def _s5_params(lam_re, lam_im, log_dt, b_re, b_im, c_re, c_im, n_scan):
    L = SSM_CHUNK
    gpl = LANES // SSM_GROUP
    nlb = SSM_GROUPS // gpl
    dt = jnp.exp(log_dt)[:, None]
    mag_log = lam_re * dt
    ang = lam_im * dt

    def power(n):
        n = jnp.asarray(n, F32)[..., None, None]
        mag = jnp.exp(mag_log * n)
        return mag * jnp.cos(ang * n), mag * jnp.sin(ang * n)

    lr, li = power(jnp.ones(()))
    den = lam_re * lam_re + lam_im * lam_im
    cr = ((lr - 1.0) * lam_re + li * lam_im) / den
    ci = (li * lam_re - (lr - 1.0) * lam_im) / den
    bbr = cr[..., None] * b_re - ci[..., None] * b_im
    bbi = cr[..., None] * b_im + ci[..., None] * b_re

    pr, pi = power(jnp.arange(L + 1, dtype=F32))
    mr = pr[:L, :, :, None] * bbr - pi[:L, :, :, None] * bbi
    mi = pr[:L, :, :, None] * bbi + pi[:L, :, :, None] * bbr
    kk = (jnp.einsum('ghp,tgpk->tghk', c_re, mr, precision=HIGHEST)
          - jnp.einsum('ghp,tgpk->tghk', c_im, mi, precision=HIGHEST))
    jj = jnp.arange(L)[:, None]
    ii = jnp.arange(L)[None, :]
    lag = ii - jj
    toe = jnp.where((lag >= 0)[:, :, None, None, None], kk[jnp.clip(lag, 0, L - 1)], 0.0)
    toe = toe.reshape(L, L, nlb, gpl, SSM_GROUP, SSM_GROUP)
    c_toe = toe.transpose(2, 0, 3, 5, 1, 4).reshape(nlb, L * LANES, L * SSM_GROUP)
    qr = pr[L - 1 - jnp.arange(L)]
    qi = pi[L - 1 - jnp.arange(L)]
    inr = (qr[..., None] * bbr - qi[..., None] * bbi).reshape(L, nlb, gpl, SSM_STATE, SSM_GROUP)
    ini = (qr[..., None] * bbi + qi[..., None] * bbr).reshape(L, nlb, gpl, SSM_STATE, SSM_GROUP)
    c_in = jnp.concatenate(
        [part.transpose(1, 0, 2, 4, 3).reshape(nlb, L * LANES, SSM_STATE) for part in (inr, ini)], axis=2)
    orr = c_re[None] * pr[1:, :, None, :] - c_im[None] * pi[1:, :, None, :]
    oii = c_re[None] * pi[1:, :, None, :] + c_im[None] * pr[1:, :, None, :]
    half = gpl * SSM_STATE
    c_out = jnp.concatenate(
        [part.reshape(L, nlb, gpl, SSM_GROUP, SSM_STATE).transpose(1, 2, 4, 0, 3)
         .reshape(nlb, half, L * SSM_GROUP) for part in (orr, -oii)], axis=1)
    ar, ai = power(float(L) * (2.0 ** jnp.arange(n_scan, dtype=F32)))
    ar = ar.reshape(n_scan, nlb, half).transpose(1, 0, 2)
    ai = ai.reshape(n_scan, nlb, half).transpose(1, 0, 2)
    return c_toe.astype(BF16), c_in.astype(BF16), c_out.astype(BF16), ar, ai


def _spread_matrix(inner, reps, period):
    rows = jnp.arange(inner * period)
    cols = jnp.arange(inner * reps * period)
    same_a = (rows[:, None] // period) == (cols[None, :] // (reps * period))
    same_c = (rows[:, None] % period) == (cols[None, :] % period)
    return jnp.logical_and(same_a, same_c).astype(BF16)


def _spread_groups(compact, spread, row_period, col_period):
    gpl = LANES // SSM_GROUP
    full = _dot(compact, spread)
    rg = (lax.broadcasted_iota(jnp.int32, full.shape, 0) // row_period) % gpl
    cg = (lax.broadcasted_iota(jnp.int32, full.shape, 1) // col_period) % gpl
    return jnp.where(rg == cg, full, 0.0).astype(BF16)


def _s5_kernel(x_ref, ct_ref, ci_ref, co_ref, et_ref, ei_ref, ar_ref, ai_ref, y_ref,
               wt_ref, wi_ref, wo_ref, sr_ref, si_ref, *, n_scan, m):
    L = SSM_CHUNK

    @pl.when(jnp.logical_and(pl.program_id(1) == 0, pl.program_id(2) == 0))
    def _():
        wt_ref[...] = _spread_groups(ct_ref[...], et_ref[...], SSM_GROUP, SSM_GROUP)
        wi_ref[...] = _spread_groups(ci_ref[...], ei_ref[...], SSM_GROUP, SSM_STATE)
        wo_ref[...] = _spread_groups(co_ref[...], et_ref[...], SSM_STATE, SSM_GROUP)

    @pl.when(pl.program_id(2) == 0)
    def _():
        sr_ref[...] = jnp.zeros_like(sr_ref)
        si_ref[...] = jnp.zeros_like(si_ref)

    xc = jnp.concatenate([x_ref[pl.ds(j, m, stride=L), :].astype(BF16) for j in range(L)], axis=1)
    yc = _dot(xc, wt_ref[...])
    z = _dot(xc, wi_ref[...])
    half = z.shape[1] // 2
    zr, zi = z[:, :half], z[:, half:]
    row = lax.broadcasted_iota(jnp.int32, (m, half), 0)
    pr, pi = sr_ref[0:1, :], si_ref[0:1, :]
    a1r, a1i = ar_ref[0:1, :], ai_ref[0:1, :]
    first = row == 0
    zr, zi = (zr + jnp.where(first, a1r * pr - a1i * pi, 0.0),
              zi + jnp.where(first, a1r * pi + a1i * pr, 0.0))
    for k in range(n_scan):
        s = 1 << k
        ar = ar_ref[k:k + 1, :]
        ai = ai_ref[k:k + 1, :]
        keep = row >= s
        tr = jnp.where(keep, pltpu.roll(zr, s, 0), 0.0)
        ti = jnp.where(keep, pltpu.roll(zi, s, 0), 0.0)
        zr, zi = zr + ar * tr - ai * ti, zi + ar * ti + ai * tr
    keep = row >= 1
    s_in = jnp.concatenate([jnp.where(keep, pltpu.roll(zr, 1, 0), pr),
                            jnp.where(keep, pltpu.roll(zi, 1, 0), pi)], axis=1)
    sr_ref[...] = jnp.broadcast_to(zr[m - 1:m, :], sr_ref.shape)
    si_ref[...] = jnp.broadcast_to(zi[m - 1:m, :], si_ref.shape)
    yc = yc + _dot(s_in.astype(BF16), wo_ref[...])
    for i in range(L):
        y_ref[pl.ds(i, m, stride=L), :] = yc[:, i * LANES:(i + 1) * LANES]


def _s5_scan(h1, lam_re, lam_im, log_dt, b_re, b_im, c_re, c_im, batch, seq, rows=2048):
    L = SSM_CHUNK
    rows = min(rows, seq)
    m = rows // L
    nt = seq // rows
    n_scan = max(1, math.ceil(math.log2(m)))
    c_toe, c_in, c_out, ar, ai = _s5_params(lam_re, lam_im, log_dt, b_re, b_im, c_re, c_im, n_scan)
    nlb, half = ar.shape[0], ar.shape[2]
    gpl = LANES // SSM_GROUP
    e_toe = _spread_matrix(L, gpl, SSM_GROUP)
    e_in = _spread_matrix(2, gpl, SSM_STATE)
    per_lb = lambda *blk: pl.BlockSpec((None,) + blk, lambda l, b, i: (l, 0, 0))
    whole = lambda a: pl.BlockSpec(a.shape, lambda l, b, i: (0, 0))
    tile = pl.BlockSpec((rows, LANES), lambda l, b, i: (b * nt + i, l))
    wide = L * LANES
    return pl.pallas_call(
        functools.partial(_s5_kernel, n_scan=n_scan, m=m),
        grid=(nlb, batch, nt),
        in_specs=[tile, per_lb(wide, c_toe.shape[2]), per_lb(wide, c_in.shape[2]),
                  per_lb(2 * half, c_out.shape[2]), whole(e_toe), whole(e_in),
                  per_lb(n_scan, half), per_lb(n_scan, half)],
        out_specs=tile,
        out_shape=jax.ShapeDtypeStruct((batch * seq, SSM_WIDTH), F32),
        scratch_shapes=[pltpu.VMEM((wide, wide), BF16), pltpu.VMEM((wide, 2 * half), BF16),
                        pltpu.VMEM((2 * half, wide), BF16)] + [pltpu.VMEM((SUBLANES, half), F32)] * 2,
        compiler_params=_cp("parallel", "arbitrary", "arbitrary"),
        name="s5_scan",
    )(h1, c_toe, c_in, c_out, e_toe, e_in, ar, ai)


def _s5_glu_kernel(y_ref, u_ref, d_ref, w_ref, b_ref, o_ref):
    y = y_ref[...] + d_ref[...] * u_ref[...]
    c = math.sqrt(2.0 / math.pi)
    z = 0.5 * y * (1.0 + jnp.tanh(c * (y + 0.044715 * (y * y * y))))
    gate = jax.nn.sigmoid(_dot(z.astype(BF16), w_ref[...]) + b_ref[...])
    o_ref[...] = (z * gate).astype(o_ref.dtype)


def _s5_glu(y, h1, d_skip, glu_w, glu_b, tm=512):
    t, w = y.shape
    row = lambda i: (i, 0)
    fixed = lambda i: (0, 0)
    return pl.pallas_call(
        _s5_glu_kernel,
        grid=(t // tm,),
        in_specs=[pl.BlockSpec((tm, w), row), pl.BlockSpec((tm, w), row),
                  pl.BlockSpec((1, w), fixed), pl.BlockSpec((w, w), fixed), pl.BlockSpec((1, w), fixed)],
        out_specs=pl.BlockSpec((tm, w), row),
        out_shape=jax.ShapeDtypeStruct((t, w), BF16),
        compiler_params=_cp("parallel"),
        name="s5_glu",
    )(y, h1, d_skip.reshape(1, w), glu_w, glu_b.reshape(1, w))


def _bdot(a, b):
    return _dot(a.astype(BF16), b.astype(BF16))


def _bdot_nt(a, b):
    return _dot_nt(a.astype(BF16), b.astype(BF16))


def _gdn_kernel(q_ref, k_ref, v_ref, gate_ref, ba_ref, wq_ref, wk_ref, wv_ref, alog_ref, dtb_ref,
                nw_ref, o_ref, state_ref, hq_ref, hk_ref, hv_ref, gct_ref, *, lt, nh, hp):
    d = GDN_D
    c = lt
    head0 = (pl.program_id(0) % (nh // hp)) * hp

    @pl.when(pl.program_id(1) == 0)
    def _():
        state_ref[...] = jnp.zeros_like(state_ref)
        for halo_ref in (hq_ref, hk_ref, hv_ref):
            halo_ref[0:SUBLANES, :] = jnp.zeros((SUBLANES, hp * d), F32)

    def conv_silu(x_ref, w_ref, xe_ref):
        w = w_ref[...]
        xe_ref[SUBLANES:, :] = x_ref[...].astype(F32)
        y = xe_ref[SUBLANES:, :] * w[GDN_CONV - 1:GDN_CONV, :]
        for back in range(1, GDN_CONV):
            y = y + xe_ref[pl.ds(SUBLANES - back, lt), :] * w[GDN_CONV - 1 - back:GDN_CONV - back, :]
        xe_ref[0:SUBLANES, :] = xe_ref[lt:, :]
        return y * jax.nn.sigmoid(y)

    def l2norm(t):
        return t * lax.rsqrt(jnp.sum(t * t, axis=-1, keepdims=True) + RMS_EPS)

    q_cs = conv_silu(q_ref, wq_ref, hq_ref)
    k_cs = conv_silu(k_ref, wk_ref, hk_ref)
    v_cs = conv_silu(v_ref, wv_ref, hv_ref)

    ba = ba_ref[...]
    lane = lax.broadcasted_iota(jnp.int32, ba.shape, 1)
    sig_ba = jax.nn.sigmoid(ba)
    g_full = -jnp.exp(alog_ref[...]) * _softplus(ba + dtb_ref[...])
    gc_full = g_full
    rows_i = lax.broadcasted_iota(jnp.int32, gc_full.shape, 0)
    step = 1
    while step < c:
        gc_full = gc_full + jnp.where(rows_i >= step, pltpu.roll(gc_full, step, 0), 0.0)
        step *= 2
    gct_ref[...] = gc_full.T

    ri = lax.broadcasted_iota(jnp.int32, (c, c), 0)
    ci = lax.broadcasted_iota(jnp.int32, (c, c), 1)
    tri = ri >= ci
    strict = ri > ci
    eye = (ri == ci).astype(F32)
    in16 = (ri // 16) == (ci // 16)
    merges = []
    width = 16
    while width < c:
        inner = (ri // width) == (ci // width)
        outer = (ri // (2 * width)) == (ci // (2 * width))
        merges.append(jnp.logical_and(outer, jnp.logical_not(inner)))
        width *= 2

    heads = range(hp)
    cols = [slice(hh * d, (hh + 1) * d) for hh in heads]
    q = [l2norm(q_cs[:, cols[hh]]) * (d ** -0.5) for hh in heads]
    k = [l2norm(k_cs[:, cols[hh]]) for hh in heads]
    v = [v_cs[:, cols[hh]] for hh in heads]
    beta = [jnp.sum(jnp.where(lane == head0 + hh, sig_ba, 0.0), axis=1, keepdims=True) for hh in heads]
    gc = [jnp.sum(jnp.where(lane == nh + head0 + hh, gc_full, 0.0), axis=1, keepdims=True)
          for hh in heads]
    gc_row = [gct_ref[pl.ds(nh + head0 + hh, 1), :] for hh in heads]
    gc_last = [gc_row[hh][:, c - 1:c] for hh in heads]
    decay = [jnp.exp(jnp.where(tri, gc[hh] - gc_row[hh], -jnp.inf)) for hh in heads]
    kb = [k[hh] * beta[hh] for hh in heads]
    kq_kt = [_bdot_nt(jnp.concatenate([kb[hh], q[hh]], axis=0), k[hh]) for hh in heads]
    lower = [jnp.where(strict, kq_kt[hh][:c] * decay[hh], 0.0) for hh in heads]
    pw = [jnp.where(in16, -lower[hh], 0.0) for hh in heads]
    inv = [eye + pw[hh] for hh in heads]
    for _ in range(3):
        pw = [_bdot(pw[hh], pw[hh]) for hh in heads]
        inv = [inv[hh] + _bdot(inv[hh], pw[hh]) for hh in heads]
    for off in merges:
        part = [_bdot(inv[hh], jnp.where(off, lower[hh], 0.0)) for hh in heads]
        inv = [inv[hh] - _bdot(part[hh], inv[hh]) for hh in heads]
    e_gc = [jnp.exp(gc[hh]) for hh in heads]
    uw = [_bdot(inv[hh], jnp.concatenate([v[hh] * beta[hh], kb[hh] * e_gc[hh]], axis=1)) for hh in heads]
    state = [state_ref[hh] for hh in heads]
    ws_qs = [_bdot(jnp.concatenate([uw[hh][:, d:], q[hh] * e_gc[hh]], axis=0), state[hh]) for hh in heads]
    v_new = [uw[hh][:, :d] - ws_qs[hh][:c] for hh in heads]
    o = [ws_qs[hh][c:] + _bdot(kq_kt[hh][c:] * decay[hh], v_new[hh]) for hh in heads]
    for hh in heads:
        k_dec = k[hh] * jnp.exp(gc_last[hh] - gc[hh])
        state_ref[hh] = (state[hh] * jnp.exp(gc_last[hh])
                         + _dot_tn(k_dec.astype(BF16), v_new[hh].astype(BF16)))
        on = o[hh] * lax.rsqrt(jnp.mean(o[hh] * o[hh], axis=-1, keepdims=True) + RMS_EPS) * nw_ref[...]
        gt = gate_ref[:, cols[hh]].astype(F32)
        o_ref[:, cols[hh]] = (on * (gt * jax.nn.sigmoid(gt))).astype(o_ref.dtype)


def _gated_deltanet(h1, ba, conv_w, a_log_pad, dt_bias_pad, norm_w, batch, seq, lt=GDN_TILE, hp=8):
    nh = N_HEADS_GDN
    d = GDN_D
    t = batch * seq
    nt = seq // lt
    ng = nh // hp
    wide = hp * d
    base = 0
    rows = lambda off: (lambda bg, i: ((bg // ng) * nt + i, off + bg % ng))
    cw = lambda off: (lambda bg, i: (0, off + bg % ng))
    fixed = lambda bg, i: (0, 0)
    return pl.pallas_call(
        functools.partial(_gdn_kernel, lt=lt, nh=nh, hp=hp),
        grid=(batch * ng, nt),
        in_specs=[pl.BlockSpec((lt, wide), rows(base)), pl.BlockSpec((lt, wide), rows(base + ng)),
                  pl.BlockSpec((lt, wide), rows(base + 2 * ng)), pl.BlockSpec((lt, wide), rows(base + 3 * ng)),
                  pl.BlockSpec((lt, LANES), lambda bg, i: ((bg // ng) * nt + i, 0)),
                  pl.BlockSpec((GDN_CONV, wide), cw(0)), pl.BlockSpec((GDN_CONV, wide), cw(ng)),
                  pl.BlockSpec((GDN_CONV, wide), cw(2 * ng)),
                  pl.BlockSpec((1, LANES), fixed), pl.BlockSpec((1, LANES), fixed),
                  pl.BlockSpec((1, d), fixed)],
        out_specs=pl.BlockSpec((lt, wide), rows(0)),
        out_shape=jax.ShapeDtypeStruct((t, nh * d), BF16),
        scratch_shapes=[pltpu.VMEM((hp, d, d), F32)] + [pltpu.VMEM((SUBLANES + lt, wide), F32)] * 3
                       + [pltpu.VMEM((LANES, lt), F32)],
        compiler_params=_cp("parallel", "arbitrary"),
        name="gated_deltanet",
    )(h1, h1, h1, h1, ba, conv_w, conv_w, conv_w, a_log_pad, dt_bias_pad, norm_w.reshape(1, d))


DMA_ISSUE_UNROLL = 8


def _moe_kernel(te_ref, nv_ref, rows_ref, src_ref, x_hbm, w1_ref, w3_ref, w2_ref, gate_ref, o_ref,
                rows_buf, xb_ref, acc_ref, sem, *, nj, tm, per_step):
    i = pl.program_id(0)
    j = pl.program_id(1)
    n_used = nv_ref[0]
    valid = i < n_used

    def start_rows(tile, first, count):
        for k in range(count):
            r = first + k
            tok = src_ref[tile * tm + jnp.minimum(r, tm - 1)]
            pltpu.make_async_copy(x_hbm.at[pl.ds(tok, 1), :], rows_buf.at[pl.ds(r, 1), :], sem).start()

    @pl.when(jnp.logical_and(i == 0, j == 0))
    def _():
        def body(s, _):
            start_rows(0, s * per_step, per_step)
            return 0
        lax.fori_loop(0, nj, body, 0)

    @pl.when(j == 0)
    def _():
        acc_ref[...] = jnp.zeros_like(acc_ref)

    @pl.when(jnp.logical_and(j == 0, i <= n_used))
    def _():
        pltpu.make_async_copy(x_hbm.at[pl.ds(0, nj * per_step), :], rows_buf, sem).wait()
        xb_ref[...] = rows_buf[0:tm, :].astype(BF16)

    half = tm // 2
    lower_only = rows_ref[i] <= half

    @pl.when(jnp.logical_and(valid, jnp.logical_not(lower_only)))
    def _():
        start_rows(i + 1, j * per_step, per_step)
        _swiglu_acc(xb_ref[...], w1_ref, w3_ref, w2_ref, acc_ref)

    @pl.when(jnp.logical_and(valid, lower_only))
    def _():
        start_rows(i + 1, j * per_step, per_step)
        _swiglu_acc(xb_ref[0:half, :], w1_ref, w3_ref, w2_ref, acc_ref.at[0:half, :])

    @pl.when(j == nj - 1)
    def _():
        o_ref[...] = acc_ref[...] * gate_ref[...]


def _moe_ffn(x, src, gates, tile_expert, n_valid, tile_rows, w1, w3, w2, tm, tf):
    p = src.shape[0]
    d, dff = w1.shape[1], w1.shape[2]
    nj = dff // tf
    per_step = -(-tm // nj)
    while (nj * per_step) % SUBLANES:
        per_step += 1
    jsel = lambda i, j, nv: jnp.where(i < nv[0], j, nj - 1)
    return pl.pallas_call(
        functools.partial(_moe_kernel, nj=nj, tm=tm, per_step=per_step),
        grid_spec=pltpu.PrefetchScalarGridSpec(
            num_scalar_prefetch=4, grid=(p // tm, nj),
            in_specs=[pl.BlockSpec(memory_space=pl.ANY),
                      pl.BlockSpec((None, d, tf), lambda i, j, te, nv, rows, src: (te[i], 0, jsel(i, j, nv))),
                      pl.BlockSpec((None, d, tf), lambda i, j, te, nv, rows, src: (te[i], 0, jsel(i, j, nv))),
                      pl.BlockSpec((None, tf, d), lambda i, j, te, nv, rows, src: (te[i], jsel(i, j, nv), 0)),
                      pl.BlockSpec((tm, 1), lambda i, j, te, nv, rows, src: (i, 0))],
            out_specs=pl.BlockSpec((tm, d), lambda i, j, te, nv, rows, src: (i, 0)),
            scratch_shapes=[pltpu.VMEM((nj * per_step, d), F32), pltpu.VMEM((tm, d), BF16),
                            pltpu.VMEM((tm, d), F32), pltpu.SemaphoreType.DMA(())]),
        out_shape=jax.ShapeDtypeStruct((p, d), F32),
        compiler_params=_cp("arbitrary", "arbitrary"),
        name="moe_ffn",
    )(tile_expert, n_valid, tile_rows, src, x, w1, w3, w2, gates)


def _combine_kernel(pos_ref, ys_hbm, res_ref, g_ref, b_ref, o_ref, buf_ref, sem, *, tm, t):
    i = pl.program_id(0)

    def issue(step, slot):
        def body(r, _):
            for choice in range(2):
                p = pos_ref[choice * t + step * tm + r]
                pltpu.make_async_copy(ys_hbm.at[pl.ds(p, 1), :], buf_ref.at[slot, choice, pl.ds(r, 1), :],
                                      sem.at[slot, choice]).start()
            return 0
        lax.fori_loop(0, tm, body, 0, unroll=DMA_ISSUE_UNROLL)

    @pl.when(i == 0)
    def _():
        issue(0, 0)

    @pl.when(i + 1 < pl.num_programs(0))
    def _():
        issue(i + 1, (i + 1) % 2)

    slot = i % 2
    for choice in range(2):
        pltpu.make_async_copy(ys_hbm.at[pl.ds(0, tm), :], buf_ref.at[slot, choice],
                              sem.at[slot, choice]).wait()
    y = buf_ref[slot, 0] + buf_ref[slot, 1]
    o_ref[...] = _layer_norm(DEEPNORM_ALPHA * res_ref[...] + y, g_ref[...], b_ref[...])


def _moe_combine_ln(ys, pos, res, g, b, tm=256):
    t, d = res.shape
    return pl.pallas_call(
        functools.partial(_combine_kernel, tm=tm, t=t),
        grid_spec=pltpu.PrefetchScalarGridSpec(
            num_scalar_prefetch=1, grid=(t // tm,),
            in_specs=[pl.BlockSpec(memory_space=pl.ANY),
                      pl.BlockSpec((tm, d), lambda i, pos: (i, 0)),
                      pl.BlockSpec((1, d), lambda i, pos: (0, 0)),
                      pl.BlockSpec((1, d), lambda i, pos: (0, 0))],
            out_specs=pl.BlockSpec((tm, d), lambda i, pos: (i, 0)),
            scratch_shapes=[pltpu.VMEM((2, 2, tm, d), F32), pltpu.SemaphoreType.DMA((2, 2))]),
        out_shape=jax.ShapeDtypeStruct((t, d), F32),
        compiler_params=_cp("arbitrary"),
        name="moe_combine_ln",
    )(pos, ys, res, g.reshape(1, d), b.reshape(1, d))


def _moe_routing(route, tm):
    t = route.shape[0]
    e = N_EXPERTS
    idx = route[:, 0:2].astype(jnp.int32)
    wts = route[:, 2:4]
    flat_e = idx.T.reshape(-1)
    onehot = (flat_e[:, None] == jnp.arange(e, dtype=jnp.int32)[None, :]).astype(jnp.int32)
    rank = jnp.cumsum(onehot, axis=0) - onehot
    counts = jnp.sum(onehot, axis=0)
    tiles = (counts + tm - 1) // tm
    tile_end = jnp.cumsum(tiles)
    start = (tile_end - tiles) * tm
    pos = jnp.sum(onehot * (start[None, :] + rank), axis=1)
    n_slots = 2 * t + e * tm
    n_tiles = n_slots // tm
    owner = jnp.full((n_slots,), -1, jnp.int32).at[pos].set(jnp.arange(2 * t, dtype=jnp.int32))
    used = owner >= 0
    src = jnp.where(used, owner % t, jnp.arange(n_slots, dtype=jnp.int32) % t)
    gates = jnp.where(used, wts.T.reshape(-1)[jnp.maximum(owner, 0)], 0.0)
    n_valid = tile_end[-1]
    tile_ids = jnp.arange(n_tiles, dtype=jnp.int32)
    tile_expert = jnp.sum((tile_ids[:, None] >= tile_end[None, :]).astype(jnp.int32), axis=1)
    last_expert = jnp.sum((n_valid - 1 >= tile_end).astype(jnp.int32))
    tile_expert = jnp.where(tile_ids < n_valid, tile_expert, last_expert).astype(jnp.int32)
    tile_rows = jnp.clip((start + counts)[tile_expert] - tile_ids * tm, 0, tm)
    tile_rows = jnp.where(tile_ids < n_valid, tile_rows, 0).astype(jnp.int32)
    return (src, gates.reshape(n_slots, 1), pos.astype(jnp.int32), tile_expert,
            n_valid.reshape(1).astype(jnp.int32), tile_rows)


def _even_layer(x, xb, batch, seq, w_in, w_out, ln_mix_g, ln_mix_b, w1, w3, w2, ln_ffn_g, ln_ffn_b):
    t, d = x.shape
    w_sb = N_HEADS_SB * HEAD_DIM
    w_in_b = w_in.astype(BF16)
    ha = _matmul(xb, w_in_b[:, :3 * w_sb], BF16, 1024, 1024, "in_proj_sb")
    hb = _matmul(xb, w_in_b[:, 3 * w_sb:], F32, 1024, 1024, "in_proj_dw")
    oa = _sb_attention(ha.reshape(batch, seq, -1), batch, seq).reshape(t, -1)
    ob = _dw_attention(hb.reshape(batch, seq, -1), batch, seq).reshape(t, -1)
    w_out_b = w_out.astype(BF16)
    x, xb = _proj_ln(oa, ob, w_out_b[:w_sb], w_out_b[w_sb:], x, ln_mix_g, ln_mix_b)
    return _ffn_ln(xb, x, w1.astype(BF16), w3.astype(BF16), w2.astype(BF16), ln_ffn_g, ln_ffn_b)


def _odd_layer(x, xb, batch, seq, w_in, lam_re, lam_im, log_dt, b_re, b_im, c_re, c_im, d_skip,
               glu_w, glu_b, conv_w, a_log, dt_bias, norm_w, w_out, ln_mix_g, ln_mix_b,
               router_w, w1, w3, w2, ln_ffn_g, ln_ffn_b):
    t, d = x.shape
    nh = N_HEADS_GDN
    wide = SSM_WIDTH + 4 * nh * GDN_D
    w_in_b = w_in[:, :wide].astype(BF16)
    u = _matmul(xb, w_in_b[:, :SSM_WIDTH], F32, 1024, 1024, "in_proj_ssm")
    h1 = _matmul(xb, w_in_b[:, SSM_WIDTH:], BF16, 1024, 1024, "in_proj_gdn")
    w_small = jnp.pad(w_in[:, wide:], ((0, 0), (0, LANES - 2 * nh)))
    ba = _matmul(x, _split_hi_lo(w_small), F32, 512, LANES, "in_proj_gates")

    y = _s5_scan(u, lam_re, lam_im, log_dt, b_re, b_im, c_re, c_im, batch, seq)
    oc = _s5_glu(y, u, d_skip, glu_w.astype(BF16), glu_b)

    pad_hi = LANES - 2 * nh
    a_log_pad = jnp.pad(a_log, (nh, pad_hi)).reshape(1, LANES)
    dt_bias_pad = jnp.pad(dt_bias, (nh, pad_hi)).reshape(1, LANES)
    od = _gated_deltanet(h1, ba, conv_w, a_log_pad, dt_bias_pad, norm_w, batch, seq)

    w_out_b = w_out.astype(BF16)
    rw = _split_hi_lo(jnp.pad(router_w, ((0, 0), (0, LANES - N_EXPERTS))))
    x, route = _proj_ln(oc, od, w_out_b[:SSM_WIDTH], w_out_b[SSM_WIDTH:], x, ln_mix_g, ln_mix_b,
                        router_w=rw)

    tm = 1024
    src, gates, pos, tile_expert, n_valid, tile_rows = _moe_routing(route, tm)
    ys = _moe_ffn(x, src, gates, tile_expert, n_valid, tile_rows, w1, w3, w2, tm, tf=256)
    return _moe_combine_ln(ys, pos, x, ln_ffn_g, ln_ffn_b)


def kernel(x, even_w_in, even_w_out, even_ln_mix_g, even_ln_mix_b, even_ffn_w1, even_ffn_w3, even_ffn_w2, even_ln_ffn_g, even_ln_ffn_b, odd_w_in, odd_ssm_lam_re, odd_ssm_lam_im, odd_ssm_log_dt, odd_ssm_b_re, odd_ssm_b_im, odd_ssm_c_re, odd_ssm_c_im, odd_ssm_d, odd_glu_w, odd_glu_b, odd_gdn_conv_w, odd_gdn_a_log, odd_gdn_dt_bias, odd_gdn_norm_w, odd_w_out, odd_ln_mix_g, odd_ln_mix_b, odd_router_w, odd_moe_w1, odd_moe_w3, odd_moe_w2, odd_ln_ffn_g, odd_ln_ffn_b):
    batch, seq, d = x.shape
    xf = x.reshape(batch * seq, d)
    xf, xb = _even_layer(xf, xf.astype(BF16), batch, seq, even_w_in[0], even_w_out[0],
                         even_ln_mix_g[0], even_ln_mix_b[0], even_ffn_w1[0], even_ffn_w3[0],
                         even_ffn_w2[0], even_ln_ffn_g[0], even_ln_ffn_b[0])
    out = _odd_layer(xf, xb, batch, seq, odd_w_in[0], odd_ssm_lam_re[0], odd_ssm_lam_im[0],
                     odd_ssm_log_dt[0], odd_ssm_b_re[0], odd_ssm_b_im[0], odd_ssm_c_re[0],
                     odd_ssm_c_im[0], odd_ssm_d[0], odd_glu_w[0], odd_glu_b[0], odd_gdn_conv_w[0],
                     odd_gdn_a_log[0], odd_gdn_dt_bias[0], odd_gdn_norm_w[0], odd_w_out[0],
                     odd_ln_mix_g[0], odd_ln_mix_b[0], odd_router_w[0], odd_moe_w1[0],
                     odd_moe_w3[0], odd_moe_w2[0], odd_ln_ffn_g[0], odd_ln_ffn_b[0])
    return out.reshape(batch, seq, d)
```

```python
import functools
import math

import jax
import jax.numpy as jnp
from jax import lax
from jax.experimental import pallas as pl
from jax.experimental.pallas import tpu as pltpu

F32 = jnp.float32
BF16 = jnp.bfloat16
HIGHEST = lax.Precision.HIGHEST

HEAD_DIM = 128
N_HEADS_SB = 8
N_HEADS_DW = 8
DW_PATTERNS = ((128, 1), (512, 4), (2048, 16))
DW_KEYS = 128
DW_TILE = 2048
SSM_WIDTH = 1024
SSM_GROUP = 16
SSM_GROUPS = 64
SSM_STATE = 64
SSM_CHUNK = 8
N_HEADS_GDN = 8
GDN_D = 128
GDN_CONV = 4
GDN_TILE = 256
N_EXPERTS = 8
DEPTH = 2
DEEPNORM_ALPHA = (2 * DEPTH) ** 0.25
LN_EPS = 1e-5
RMS_EPS = 1e-6

LANES = 128
SUBLANES = 8
VMEM_LIMIT = 56 * 1024 * 1024
SB_SKIP_LOG = -104.0


def _cp(*sem):
    return pltpu.CompilerParams(dimension_semantics=sem, vmem_limit_bytes=VMEM_LIMIT)


def _layer_norm(y, g, b):
    mu = jnp.mean(y, axis=-1, keepdims=True)
    yc = y - mu
    var = jnp.mean(yc * yc, axis=-1, keepdims=True)
    return yc * lax.rsqrt(var + LN_EPS) * g + b


def _softplus(x):
    return jnp.maximum(x, 0.0) + jnp.log1p(jnp.exp(-jnp.abs(x)))


def _dot(a, b, precision=None):
    return jnp.dot(a, b, preferred_element_type=F32, precision=precision)


def _dot_nt(a, b, precision=None):
    return lax.dot_general(a, b, (((1,), (1,)), ((), ())),
                           preferred_element_type=F32, precision=precision)


def _dot_tn(a, b, precision=None):
    return lax.dot_general(a, b, (((0,), (0,)), ((), ())),
                           preferred_element_type=F32, precision=precision)


def _split_hi_lo(w):
    hi = w.astype(BF16)
    return jnp.stack([hi, (w - hi.astype(F32)).astype(BF16)])


def _dot_split(x, w_ref):
    xh = x.astype(BF16)
    xl = (x - xh.astype(F32)).astype(BF16)
    wh = w_ref[0]
    return _dot(xh, wh) + _dot(xl, wh) + _dot(xh, w_ref[1])


def _mm_kernel(a_ref, b_ref, o_ref):
    o_ref[...] = _dot(a_ref[...], b_ref[...]).astype(o_ref.dtype)


def _mm_split_kernel(a_ref, b_ref, o_ref):
    o_ref[...] = _dot_split(a_ref[...], b_ref).astype(o_ref.dtype)


def _matmul(a, b, out_dtype, tm, tn, name):
    m, k = a.shape
    n = b.shape[-1]
    if b.ndim == 3:
        body, b_spec = _mm_split_kernel, pl.BlockSpec((2, k, tn), lambda i, j: (0, 0, j))
    else:
        body, b_spec = _mm_kernel, pl.BlockSpec((k, tn), lambda i, j: (0, j))
    return pl.pallas_call(
        body,
        grid=(m // tm, n // tn),
        in_specs=[pl.BlockSpec((tm, k), lambda i, j: (i, 0)), b_spec],
        out_specs=pl.BlockSpec((tm, tn), lambda i, j: (i, j)),
        out_shape=jax.ShapeDtypeStruct((m, n), out_dtype),
        compiler_params=_cp("parallel", "parallel"),
        name=name,
    )(a, b)


def _proj_ln_kernel(a0_ref, a1_ref, w0_ref, w1_ref, res_ref, g_ref, b_ref, *rest, with_router):
    if with_router:
        rw_ref, of_ref, rt_ref = rest
    else:
        of_ref, ob_ref = rest
    mix = _dot(a0_ref[...], w0_ref[...]) + _dot(a1_ref[...], w1_ref[...])
    xn = _layer_norm(DEEPNORM_ALPHA * res_ref[...] + mix, g_ref[...], b_ref[...])
    of_ref[...] = xn
    if with_router:
        logits = _dot_split(xn, rw_ref)
        lane = lax.broadcasted_iota(jnp.int32, logits.shape, 1).astype(F32)
        neg = jnp.float32(-jnp.inf)
        lg = jnp.where(lane < N_EXPERTS, logits, neg)
        m1 = jnp.max(lg, axis=1, keepdims=True)
        i1 = jnp.min(jnp.where(lg == m1, lane, float(LANES)), axis=1, keepdims=True)
        lg2 = jnp.where(lane == i1, neg, lg)
        m2 = jnp.max(lg2, axis=1, keepdims=True)
        i2 = jnp.min(jnp.where(lg2 == m2, lane, float(LANES)), axis=1, keepdims=True)
        e2 = jnp.exp(m2 - m1)
        p1 = 1.0 / (1.0 + e2)
        p2 = e2 * p1
        rt = jnp.where(lane == 0.0, i1, jnp.where(lane == 1.0, i2,
                       jnp.where(lane == 2.0, p1, jnp.where(lane == 3.0, p2, 0.0))))
        rt_ref[...] = rt
    else:
        ob_ref[...] = xn.astype(BF16)


def _proj_ln(a0, a1, w0, w1, res, g, b, router_w=None, tm=256):
    t, d = res.shape
    k0, k1 = a0.shape[1], a1.shape[1]
    with_router = router_w is not None
    row = lambda i: (i, 0)
    fixed = lambda i: (0, 0)
    in_specs = [pl.BlockSpec((tm, k0), row), pl.BlockSpec((tm, k1), row),
                pl.BlockSpec((k0, d), fixed), pl.BlockSpec((k1, d), fixed),
                pl.BlockSpec((tm, d), row), pl.BlockSpec((1, d), fixed), pl.BlockSpec((1, d), fixed)]
    args = [a0, a1, w0, w1, res, g.reshape(1, d), b.reshape(1, d)]
    if with_router:
        in_specs.append(pl.BlockSpec((2, d, LANES), lambda i: (0, 0, 0)))
        out_specs = [pl.BlockSpec((tm, d), row), pl.BlockSpec((tm, LANES), row)]
        out_shape = [jax.ShapeDtypeStruct((t, d), F32), jax.ShapeDtypeStruct((t, LANES), F32)]
        args.append(router_w)
    else:
        out_specs = [pl.BlockSpec((tm, d), row), pl.BlockSpec((tm, d), row)]
        out_shape = [jax.ShapeDtypeStruct((t, d), F32), jax.ShapeDtypeStruct((t, d), BF16)]
    return pl.pallas_call(
        functools.partial(_proj_ln_kernel, with_router=with_router),
        grid=(t // tm,), in_specs=in_specs, out_specs=out_specs, out_shape=out_shape,
        compiler_params=_cp("parallel"),
        name="proj_ln_router" if with_router else "proj_ln",
    )(*args)


def _swiglu_acc(x, w1_ref, w3_ref, w2_ref, acc_ref):
    h1 = _dot(x, w1_ref[...].astype(BF16))
    h3 = _dot(x, w3_ref[...].astype(BF16))
    act = (h1 * jax.nn.sigmoid(h1) * h3).astype(BF16)
    acc_ref[...] += _dot(act, w2_ref[...].astype(BF16))


def _ffn_kernel(x_ref, w1_ref, w3_ref, w2_ref, res_ref, g_ref, b_ref, of_ref, ob_ref, acc_ref, *, nj):
    j = pl.program_id(1)

    @pl.when(j == 0)
    def _():
        acc_ref[...] = jnp.zeros_like(acc_ref)

    _swiglu_acc(x_ref[...], w1_ref, w3_ref, w2_ref, acc_ref)

    @pl.when(j == nj - 1)
    def _():
        xn = _layer_norm(DEEPNORM_ALPHA * res_ref[...] + acc_ref[...], g_ref[...], b_ref[...])
        of_ref[...] = xn
        ob_ref[...] = xn.astype(BF16)


def _ffn_ln(xb, res, w1, w3, w2, g, b, tm=512, tf=512):
    t, d = res.shape
    dff = w1.shape[1]
    nj = dff // tf
    row = lambda i, j: (i, 0)
    fixed = lambda i, j: (0, 0)
    return pl.pallas_call(
        functools.partial(_ffn_kernel, nj=nj),
        grid=(t // tm, nj),
        in_specs=[pl.BlockSpec((tm, d), row),
                  pl.BlockSpec((d, tf), lambda i, j: (0, j)),
                  pl.BlockSpec((d, tf), lambda i, j: (0, j)),
                  pl.BlockSpec((tf, d), lambda i, j: (j, 0)),
                  pl.BlockSpec((tm, d), row), pl.BlockSpec((1, d), fixed), pl.BlockSpec((1, d), fixed)],
        out_specs=[pl.BlockSpec((tm, d), row), pl.BlockSpec((tm, d), row)],
        out_shape=[jax.ShapeDtypeStruct((t, d), F32), jax.ShapeDtypeStruct((t, d), BF16)],
        scratch_shapes=[pltpu.VMEM((tm, d), F32)],
        compiler_params=_cp("parallel", "arbitrary"),
        name="ffn_ln",
    )(xb, w1, w3, w2, res, g.reshape(1, d), b.reshape(1, d))


SB_QTILES = 2


def _sb_kernel(q_ref, k_ref, v_ref, o_ref, acc_ref, carry_ref, *, tq, scale):
    first_tile = pl.program_id(1) * SB_QTILES
    row = lax.broadcasted_iota(jnp.int32, (tq, tq), 0)
    col = lax.broadcasted_iota(jnp.int32, (tq, tq), 1)
    later_sum = (row > col).astype(BF16)
    past = col < row
    qs = [q_ref[t * tq:(t + 1) * tq, :] for t in range(SB_QTILES)]

    def blocks_terms(chains):
        starts = [pl.multiple_of(kb * tq, tq) for _, kb, _ in chains]
        z = [_dot_nt(q, k_ref[pl.ds(s, tq), :]) * scale for (q, _, _), s in zip(chains, starts)]
        lk = [-_softplus(zz) for zz in z]
        lk = [jnp.where(past, l, 0.0) if dg else l for l, (_, _, dg) in zip(lk, chains)]
        hi = [l.astype(BF16) for l in lk]
        lo = [(l - h.astype(F32)).astype(BF16) for l, h in zip(lk, hi)]
        later = [_dot(h, later_sum) + _dot(l, later_sum) for h, l in zip(hi, lo)]
        logw = [zz + l + lt for zz, l, lt in zip(z, lk, later)]
        tot = [jnp.sum(l, axis=1, keepdims=True) for l in lk]
        return logw, tot, [v_ref[pl.ds(s, tq), :] for s in starts]

    chains = []
    for t in range(SB_QTILES):
        chains += [(qs[t], first_tile + t, True), (qs[t], jnp.maximum(first_tile + t - 1, 0), False)]
    logw, tot, vals = blocks_terms(chains)
    for t in range(SB_QTILES):
        has_prev = first_tile + t >= 1
        w0 = jnp.where(past, jnp.exp(logw[2 * t]), 0.0)
        w1 = jnp.where(has_prev, jnp.exp(logw[2 * t + 1] + tot[2 * t]), 0.0)
        acc_ref[t] = _dot(w0.astype(BF16), vals[2 * t]) + _dot(w1.astype(BF16), vals[2 * t + 1])
        carry_ref[t] = tot[2 * t] + jnp.where(has_prev, tot[2 * t + 1], 0.0)

    for t in range(SB_QTILES):
        def cond(kb, t=t):
            return jnp.logical_and(kb >= 0, jnp.max(carry_ref[t]) > SB_SKIP_LOG)

        def body(kb, t=t):
            (lw,), (tt,), (v,) = blocks_terms([(qs[t], kb, False)])
            acc_ref[t] += _dot(jnp.exp(lw + carry_ref[t]).astype(BF16), v)
            carry_ref[t] += tt
            return kb - 1

        lax.while_loop(cond, body, first_tile + t - 2)
        o_ref[t * tq:(t + 1) * tq, :] = acc_ref[t].astype(o_ref.dtype)


def _sb_attention(h, batch, seq, tq=256):
    nh = N_HEADS_SB
    rows = SB_QTILES * tq
    return pl.pallas_call(
        functools.partial(_sb_kernel, tq=tq, scale=HEAD_DIM ** -0.5),
        grid=(batch * nh, seq // rows),
        in_specs=[pl.BlockSpec((None, rows, HEAD_DIM), lambda bh, i: (bh // nh, i, bh % nh)),
                  pl.BlockSpec((None, seq, HEAD_DIM), lambda bh, i: (bh // nh, 0, nh + bh % nh)),
                  pl.BlockSpec((None, seq, HEAD_DIM), lambda bh, i: (bh // nh, 0, 2 * nh + bh % nh))],
        out_specs=pl.BlockSpec((None, rows, HEAD_DIM), lambda bh, i: (bh // nh, i, bh % nh)),
        out_shape=jax.ShapeDtypeStruct((batch, seq, nh * HEAD_DIM), BF16),
        scratch_shapes=[pltpu.VMEM((SB_QTILES, tq, HEAD_DIM), F32), pltpu.VMEM((SB_QTILES, tq, 1), F32)],
        compiler_params=_cp("parallel", "arbitrary"),
        name="sb_attention",
    )(h, h, h)


DW_UNROLL = 8


def _dw_kernel(q_ref, k_ref, v_ref, o_ref, m_ref, l_ref, acc_ref, kp_ref, vp_ref, *, scale):
    nk = DW_KEYS
    t0 = pl.program_id(1) * DW_TILE

    @pl.when(pl.program_id(1) == 0)
    def _():
        kp_ref[...] = jnp.zeros_like(kp_ref)
        vp_ref[...] = jnp.zeros_like(vp_ref)

    ii = lax.broadcasted_iota(jnp.int32, (nk, 2 * nk), 0)
    jj = lax.broadcasted_iota(jnp.int32, (nk, 2 * nk), 1)
    band = jnp.logical_and(jj >= ii, jj <= ii + nk)
    band_prev = jnp.logical_and(band, jj < nk)
    band_own = jnp.logical_and(band, jj >= nk)
    neg = jnp.float32(-jnp.inf)

    for p_idx, (window, dil) in enumerate(DW_PATTERNS):
        assert window // dil == nk
        span = nk * dil
        n_steps = DW_TILE // nk

        def group(g, _, dil=dil, span=span, p_idx=p_idx):
            subs = range(DW_UNROLL)
            blk = [(g * DW_UNROLL + u) // dil for u in subs]
            res = [(g * DW_UNROLL + u) % dil for u in subs]
            off = [b * span + r for b, r in zip(blk, res)]
            rows = [pl.ds(o, nk, stride=dil) for o in off]
            cur = [pl.ds(t0 + o, nk, stride=dil) for o in off]
            has_prev = [(t0 + b * span) > 0 for b in blk]
            prev = [pl.ds(jnp.maximum(t0 + b * span - span, 0) + r, nk, stride=dil) for b, r in zip(blk, res)]
            qs = [q_ref[rw, :].astype(BF16) for rw in rows]
            k_cur = [k_ref[cr, :].astype(BF16) for cr in cur]
            v_cur = [v_ref[cr, :].astype(BF16) for cr in cur]
            if span == DW_TILE:
                k_prev = [kp_ref[r] for r in res]
                v_prev = [vp_ref[r] for r in res]
                for r, kc, vc in zip(res, k_cur, v_cur):
                    kp_ref[r] = kc
                    vp_ref[r] = vc
            else:
                k_prev = [k_ref[pv, :].astype(BF16) for pv in prev]
                v_prev = [v_ref[pv, :].astype(BF16) for pv in prev]
            keys = [jnp.concatenate([kp, kc], axis=0) for kp, kc in zip(k_prev, k_cur)]
            sc = [_dot_nt(qq, kk) * scale for qq, kk in zip(qs, keys)]
            sc = [jnp.where(jnp.logical_or(band_own, jnp.logical_and(band_prev, hp)), s, neg)
                  for s, hp in zip(sc, has_prev)]
            m_new = [jnp.max(s, axis=1, keepdims=True) for s in sc]
            p = [jnp.exp(s - m) for s, m in zip(sc, m_new)]
            l_new = [jnp.sum(pp, axis=1, keepdims=True) for pp in p]
            vals = [jnp.concatenate([vp, vc], axis=0) for vp, vc in zip(v_prev, v_cur)]
            num = [_dot(pp.astype(BF16), vv) for pp, vv in zip(p, vals)]
            for rw, m, l, nm in zip(rows, m_new, l_new, num):
                m_ref[p_idx, rw, :] = jnp.broadcast_to(m, (nk, HEAD_DIM))
                l_ref[p_idx, rw, :] = jnp.broadcast_to(l, (nk, HEAD_DIM))
                acc_ref[p_idx, rw, :] = nm
            return 0

        lax.fori_loop(0, n_steps // DW_UNROLL, group, 0)

    m_all = m_ref[...]
    m_tot = jnp.max(m_all, axis=0)
    w = jnp.exp(m_all - m_tot[None])
    den = jnp.sum(w * l_ref[...], axis=0)
    num = jnp.sum(w * acc_ref[...], axis=0)
    o_ref[...] = (num / den).astype(o_ref.dtype)


def _dw_attention(h, batch, seq):
    nh = N_HEADS_DW
    return pl.pallas_call(
        functools.partial(_dw_kernel, scale=HEAD_DIM ** -0.5),
        grid=(batch * nh, seq // DW_TILE),
        in_specs=[pl.BlockSpec((None, DW_TILE, HEAD_DIM), lambda bh, i: (bh // nh, i, bh % nh)),
                  pl.BlockSpec((None, seq, HEAD_DIM), lambda bh, i: (bh // nh, 0, nh + bh % nh)),
                  pl.BlockSpec((None, seq, HEAD_DIM), lambda bh, i: (bh // nh, 0, 2 * nh + bh % nh))],
        out_specs=pl.BlockSpec((None, DW_TILE, HEAD_DIM), lambda bh, i: (bh // nh, i, bh % nh)),
        out_shape=jax.ShapeDtypeStruct((batch, seq, nh * HEAD_DIM), BF16),
        scratch_shapes=[pltpu.VMEM((len(DW_PATTERNS), DW_TILE, HEAD_DIM), F32)] * 3
                       + [pltpu.VMEM((DW_TILE // DW_KEYS, DW_KEYS, HEAD_DIM), BF16)] * 2,
        compiler_params=_cp("parallel", "arbitrary"),
        name="dw_attention",
    )(h, h, h)


def _s5_params(lam_re, lam_im, log_dt, b_re, b_im, c_re, c_im, n_scan):
    L = SSM_CHUNK
    gpl = LANES // SSM_GROUP
    nlb = SSM_GROUPS // gpl
    dt = jnp.exp(log_dt)[:, None]
    mag_log = lam_re * dt
    ang = lam_im * dt

    def power(n):
        n = jnp.asarray(n, F32)[..., None, None]
        mag = jnp.exp(mag_log * n)
        return mag * jnp.cos(ang * n), mag * jnp.sin(ang * n)

    lr, li = power(jnp.ones(()))
    den = lam_re * lam_re + lam_im * lam_im
    cr = ((lr - 1.0) * lam_re + li * lam_im) / den
    ci = (li * lam_re - (lr - 1.0) * lam_im) / den
    bbr = cr[..., None] * b_re - ci[..., None] * b_im
    bbi = cr[..., None] * b_im + ci[..., None] * b_re

    pr, pi = power(jnp.arange(L + 1, dtype=F32))
    mr = pr[:L, :, :, None] * bbr - pi[:L, :, :, None] * bbi
    mi = pr[:L, :, :, None] * bbi + pi[:L, :, :, None] * bbr
    kk = (jnp.einsum('ghp,tgpk->tghk', c_re, mr, precision=HIGHEST)
          - jnp.einsum('ghp,tgpk->tghk', c_im, mi, precision=HIGHEST))
    jj = jnp.arange(L)[:, None]
    ii = jnp.arange(L)[None, :]
    lag = ii - jj
    toe = jnp.where((lag >= 0)[:, :, None, None, None], kk[jnp.clip(lag, 0, L - 1)], 0.0)
    toe = toe.reshape(L, L, nlb, gpl, SSM_GROUP, SSM_GROUP)
    c_toe = toe.transpose(2, 0, 3, 5, 1, 4).reshape(nlb, L * LANES, L * SSM_GROUP)
    qr = pr[L - 1 - jnp.arange(L)]
    qi = pi[L - 1 - jnp.arange(L)]
    inr = (qr[..., None] * bbr - qi[..., None] * bbi).reshape(L, nlb, gpl, SSM_STATE, SSM_GROUP)
    ini = (qr[..., None] * bbi + qi[..., None] * bbr).reshape(L, nlb, gpl, SSM_STATE, SSM_GROUP)
    c_in = jnp.concatenate(
        [part.transpose(1, 0, 2, 4, 3).reshape(nlb, L * LANES, SSM_STATE) for part in (inr, ini)], axis=2)
    orr = c_re[None] * pr[1:, :, None, :] - c_im[None] * pi[1:, :, None, :]
    oii = c_re[None] * pi[1:, :, None, :] + c_im[None] * pr[1:, :, None, :]
    half = gpl * SSM_STATE
    c_out = jnp.concatenate(
        [part.reshape(L, nlb, gpl, SSM_GROUP, SSM_STATE).transpose(1, 2, 4, 0, 3)
         .reshape(nlb, half, L * SSM_GROUP) for part in (orr, -oii)], axis=1)
    ar, ai = power(float(L) * (2.0 ** jnp.arange(n_scan, dtype=F32)))
    ar = ar.reshape(n_scan, nlb, half).transpose(1, 0, 2)
    ai = ai.reshape(n_scan, nlb, half).transpose(1, 0, 2)
    return c_toe.astype(BF16), c_in.astype(BF16), c_out.astype(BF16), ar, ai


def _spread_matrix(inner, reps, period):
    rows = jnp.arange(inner * period)
    cols = jnp.arange(inner * reps * period)
    same_a = (rows[:, None] // period) == (cols[None, :] // (reps * period))
    same_c = (rows[:, None] % period) == (cols[None, :] % period)
    return jnp.logical_and(same_a, same_c).astype(BF16)


def _spread_groups(compact, spread, row_period, col_period):
    gpl = LANES // SSM_GROUP
    full = _dot(compact, spread)
    rg = (lax.broadcasted_iota(jnp.int32, full.shape, 0) // row_period) % gpl
    cg = (lax.broadcasted_iota(jnp.int32, full.shape, 1) // col_period) % gpl
    return jnp.where(rg == cg, full, 0.0).astype(BF16)


def _s5_kernel(x_ref, ct_ref, ci_ref, co_ref, et_ref, ei_ref, ar_ref, ai_ref, y_ref,
               wt_ref, wi_ref, wo_ref, sr_ref, si_ref, *, n_scan, m):
    L = SSM_CHUNK

    @pl.when(jnp.logical_and(pl.program_id(1) == 0, pl.program_id(2) == 0))
    def _():
        wt_ref[...] = _spread_groups(ct_ref[...], et_ref[...], SSM_GROUP, SSM_GROUP)
        wi_ref[...] = _spread_groups(ci_ref[...], ei_ref[...], SSM_GROUP, SSM_STATE)
        wo_ref[...] = _spread_groups(co_ref[...], et_ref[...], SSM_STATE, SSM_GROUP)

    @pl.when(pl.program_id(2) == 0)
    def _():
        sr_ref[...] = jnp.zeros_like(sr_ref)
        si_ref[...] = jnp.zeros_like(si_ref)

    xc = jnp.concatenate([x_ref[pl.ds(j, m, stride=L), :].astype(BF16) for j in range(L)], axis=1)
    yc = _dot(xc, wt_ref[...])
    z = _dot(xc, wi_ref[...])
    half = z.shape[1] // 2
    zr, zi = z[:, :half], z[:, half:]
    row = lax.broadcasted_iota(jnp.int32, (m, half), 0)
    pr, pi = sr_ref[0:1, :], si_ref[0:1, :]
    a1r, a1i = ar_ref[0:1, :], ai_ref[0:1, :]
    first = row == 0
    zr, zi = (zr + jnp.where(first, a1r * pr - a1i * pi, 0.0),
              zi + jnp.where(first, a1r * pi + a1i * pr, 0.0))
    for k in range(n_scan):
        s = 1 << k
        ar = ar_ref[k:k + 1, :]
        ai = ai_ref[k:k + 1, :]
        keep = row >= s
        tr = jnp.where(keep, pltpu.roll(zr, s, 0), 0.0)
        ti = jnp.where(keep, pltpu.roll(zi, s, 0), 0.0)
        zr, zi = zr + ar * tr - ai * ti, zi + ar * ti + ai * tr
    keep = row >= 1
    s_in = jnp.concatenate([jnp.where(keep, pltpu.roll(zr, 1, 0), pr),
                            jnp.where(keep, pltpu.roll(zi, 1, 0), pi)], axis=1)
    sr_ref[...] = jnp.broadcast_to(zr[m - 1:m, :], sr_ref.shape)
    si_ref[...] = jnp.broadcast_to(zi[m - 1:m, :], si_ref.shape)
    yc = yc + _dot(s_in.astype(BF16), wo_ref[...])
    for i in range(L):
        y_ref[pl.ds(i, m, stride=L), :] = yc[:, i * LANES:(i + 1) * LANES]


def _s5_scan(h1, lam_re, lam_im, log_dt, b_re, b_im, c_re, c_im, batch, seq, rows=2048):
    L = SSM_CHUNK
    rows = min(rows, seq)
    m = rows // L
    nt = seq // rows
    n_scan = max(1, math.ceil(math.log2(m)))
    c_toe, c_in, c_out, ar, ai = _s5_params(lam_re, lam_im, log_dt, b_re, b_im, c_re, c_im, n_scan)
    nlb, half = ar.shape[0], ar.shape[2]
    gpl = LANES // SSM_GROUP
    e_toe = _spread_matrix(L, gpl, SSM_GROUP)
    e_in = _spread_matrix(2, gpl, SSM_STATE)
    per_lb = lambda *blk: pl.BlockSpec((None,) + blk, lambda l, b, i: (l, 0, 0))
    whole = lambda a: pl.BlockSpec(a.shape, lambda l, b, i: (0, 0))
    tile = pl.BlockSpec((rows, LANES), lambda l, b, i: (b * nt + i, l))
    wide = L * LANES
    return pl.pallas_call(
        functools.partial(_s5_kernel, n_scan=n_scan, m=m),
        grid=(nlb, batch, nt),
        in_specs=[tile, per_lb(wide, c_toe.shape[2]), per_lb(wide, c_in.shape[2]),
                  per_lb(2 * half, c_out.shape[2]), whole(e_toe), whole(e_in),
                  per_lb(n_scan, half), per_lb(n_scan, half)],
        out_specs=tile,
        out_shape=jax.ShapeDtypeStruct((batch * seq, SSM_WIDTH), F32),
        scratch_shapes=[pltpu.VMEM((wide, wide), BF16), pltpu.VMEM((wide, 2 * half), BF16),
                        pltpu.VMEM((2 * half, wide), BF16)] + [pltpu.VMEM((SUBLANES, half), F32)] * 2,
        compiler_params=_cp("parallel", "arbitrary", "arbitrary"),
        name="s5_scan",
    )(h1, c_toe, c_in, c_out, e_toe, e_in, ar, ai)


def _s5_glu_kernel(y_ref, u_ref, d_ref, w_ref, b_ref, o_ref):
    y = y_ref[...] + d_ref[...] * u_ref[...]
    c = math.sqrt(2.0 / math.pi)
    z = 0.5 * y * (1.0 + jnp.tanh(c * (y + 0.044715 * (y * y * y))))
    gate = jax.nn.sigmoid(_dot(z.astype(BF16), w_ref[...]) + b_ref[...])
    o_ref[...] = (z * gate).astype(o_ref.dtype)


def _s5_glu(y, h1, d_skip, glu_w, glu_b, tm=512):
    t, w = y.shape
    row = lambda i: (i, 0)
    fixed = lambda i: (0, 0)
    return pl.pallas_call(
        _s5_glu_kernel,
        grid=(t // tm,),
        in_specs=[pl.BlockSpec((tm, w), row), pl.BlockSpec((tm, w), row),
                  pl.BlockSpec((1, w), fixed), pl.BlockSpec((w, w), fixed), pl.BlockSpec((1, w), fixed)],
        out_specs=pl.BlockSpec((tm, w), row),
        out_shape=jax.ShapeDtypeStruct((t, w), BF16),
        compiler_params=_cp("parallel"),
        name="s5_glu",
    )(y, h1, d_skip.reshape(1, w), glu_w, glu_b.reshape(1, w))


def _bdot(a, b):
    return _dot(a.astype(BF16), b.astype(BF16))


def _bdot_nt(a, b):
    return _dot_nt(a.astype(BF16), b.astype(BF16))


def _gdn_kernel(q_ref, k_ref, v_ref, gate_ref, ba_ref, wq_ref, wk_ref, wv_ref, alog_ref, dtb_ref,
                nw_ref, o_ref, state_ref, hq_ref, hk_ref, hv_ref, gct_ref, *, lt, nh, hp):
    d = GDN_D
    c = lt
    head0 = (pl.program_id(0) % (nh // hp)) * hp

    @pl.when(pl.program_id(1) == 0)
    def _():
        state_ref[...] = jnp.zeros_like(state_ref)
        for halo_ref in (hq_ref, hk_ref, hv_ref):
            halo_ref[0:SUBLANES, :] = jnp.zeros((SUBLANES, hp * d), F32)

    def conv_silu(x_ref, w_ref, xe_ref):
        w = w_ref[...]
        xe_ref[SUBLANES:, :] = x_ref[...].astype(F32)
        y = xe_ref[SUBLANES:, :] * w[GDN_CONV - 1:GDN_CONV, :]
        for back in range(1, GDN_CONV):
            y = y + xe_ref[pl.ds(SUBLANES - back, lt), :] * w[GDN_CONV - 1 - back:GDN_CONV - back, :]
        xe_ref[0:SUBLANES, :] = xe_ref[lt:, :]
        return y * jax.nn.sigmoid(y)

    def l2norm(t):
        return t * lax.rsqrt(jnp.sum(t * t, axis=-1, keepdims=True) + RMS_EPS)

    q_cs = conv_silu(q_ref, wq_ref, hq_ref)
    k_cs = conv_silu(k_ref, wk_ref, hk_ref)
    v_cs = conv_silu(v_ref, wv_ref, hv_ref)

    ba = ba_ref[...]
    lane = lax.broadcasted_iota(jnp.int32, ba.shape, 1)
    sig_ba = jax.nn.sigmoid(ba)
    g_full = -jnp.exp(alog_ref[...]) * _softplus(ba + dtb_ref[...])
    gc_full = g_full
    rows_i = lax.broadcasted_iota(jnp.int32, gc_full.shape, 0)
    step = 1
    while step < c:
        gc_full = gc_full + jnp.where(rows_i >= step, pltpu.roll(gc_full, step, 0), 0.0)
        step *= 2
    gct_ref[...] = gc_full.T

    ri = lax.broadcasted_iota(jnp.int32, (c, c), 0)
    ci = lax.broadcasted_iota(jnp.int32, (c, c), 1)
    tri = ri >= ci
    strict = ri > ci
    eye = (ri == ci).astype(F32)
    in16 = (ri // 16) == (ci // 16)
    merges = []
    width = 16
    while width < c:
        inner = (ri // width) == (ci // width)
        outer = (ri // (2 * width)) == (ci // (2 * width))
        merges.append(jnp.logical_and(outer, jnp.logical_not(inner)))
        width *= 2

    heads = range(hp)
    cols = [slice(hh * d, (hh + 1) * d) for hh in heads]
    q = [l2norm(q_cs[:, cols[hh]]) * (d ** -0.5) for hh in heads]
    k = [l2norm(k_cs[:, cols[hh]]) for hh in heads]
    v = [v_cs[:, cols[hh]] for hh in heads]
    beta = [jnp.sum(jnp.where(lane == head0 + hh, sig_ba, 0.0), axis=1, keepdims=True) for hh in heads]
    gc = [jnp.sum(jnp.where(lane == nh + head0 + hh, gc_full, 0.0), axis=1, keepdims=True)
          for hh in heads]
    gc_row = [gct_ref[pl.ds(nh + head0 + hh, 1), :] for hh in heads]
    gc_last = [gc_row[hh][:, c - 1:c] for hh in heads]
    decay = [jnp.exp(jnp.where(tri, gc[hh] - gc_row[hh], -jnp.inf)) for hh in heads]
    kb = [k[hh] * beta[hh] for hh in heads]
    kq_kt = [_bdot_nt(jnp.concatenate([kb[hh], q[hh]], axis=0), k[hh]) for hh in heads]
    lower = [jnp.where(strict, kq_kt[hh][:c] * decay[hh], 0.0) for hh in heads]
    pw = [jnp.where(in16, -lower[hh], 0.0) for hh in heads]
    inv = [eye + pw[hh] for hh in heads]
    for _ in range(3):
        pw = [_bdot(pw[hh], pw[hh]) for hh in heads]
        inv = [inv[hh] + _bdot(inv[hh], pw[hh]) for hh in heads]
    for off in merges:
        part = [_bdot(inv[hh], jnp.where(off, lower[hh], 0.0)) for hh in heads]
        inv = [inv[hh] - _bdot(part[hh], inv[hh]) for hh in heads]
    e_gc = [jnp.exp(gc[hh]) for hh in heads]
    uw = [_bdot(inv[hh], jnp.concatenate([v[hh] * beta[hh], kb[hh] * e_gc[hh]], axis=1)) for hh in heads]
    state = [state_ref[hh] for hh in heads]
    ws_qs = [_bdot(jnp.concatenate([uw[hh][:, d:], q[hh] * e_gc[hh]], axis=0), state[hh]) for hh in heads]
    v_new = [uw[hh][:, :d] - ws_qs[hh][:c] for hh in heads]
    o = [ws_qs[hh][c:] + _bdot(kq_kt[hh][c:] * decay[hh], v_new[hh]) for hh in heads]
    for hh in heads:
        k_dec = k[hh] * jnp.exp(gc_last[hh] - gc[hh])
        state_ref[hh] = (state[hh] * jnp.exp(gc_last[hh])
                         + _dot_tn(k_dec.astype(BF16), v_new[hh].astype(BF16)))
        on = o[hh] * lax.rsqrt(jnp.mean(o[hh] * o[hh], axis=-1, keepdims=True) + RMS_EPS) * nw_ref[...]
        gt = gate_ref[:, cols[hh]].astype(F32)
        o_ref[:, cols[hh]] = (on * (gt * jax.nn.sigmoid(gt))).astype(o_ref.dtype)


def _gated_deltanet(h1, ba, conv_w, a_log_pad, dt_bias_pad, norm_w, batch, seq, lt=GDN_TILE, hp=8):
    nh = N_HEADS_GDN
    d = GDN_D
    t = batch * seq
    nt = seq // lt
    ng = nh // hp
    wide = hp * d
    base = 0
    rows = lambda off: (lambda bg, i: ((bg // ng) * nt + i, off + bg % ng))
    cw = lambda off: (lambda bg, i: (0, off + bg % ng))
    fixed = lambda bg, i: (0, 0)
    return pl.pallas_call(
        functools.partial(_gdn_kernel, lt=lt, nh=nh, hp=hp),
        grid=(batch * ng, nt),
        in_specs=[pl.BlockSpec((lt, wide), rows(base)), pl.BlockSpec((lt, wide), rows(base + ng)),
                  pl.BlockSpec((lt, wide), rows(base + 2 * ng)), pl.BlockSpec((lt, wide), rows(base + 3 * ng)),
                  pl.BlockSpec((lt, LANES), lambda bg, i: ((bg // ng) * nt + i, 0)),
                  pl.BlockSpec((GDN_CONV, wide), cw(0)), pl.BlockSpec((GDN_CONV, wide), cw(ng)),
                  pl.BlockSpec((GDN_CONV, wide), cw(2 * ng)),
                  pl.BlockSpec((1, LANES), fixed), pl.BlockSpec((1, LANES), fixed),
                  pl.BlockSpec((1, d), fixed)],
        out_specs=pl.BlockSpec((lt, wide), rows(0)),
        out_shape=jax.ShapeDtypeStruct((t, nh * d), BF16),
        scratch_shapes=[pltpu.VMEM((hp, d, d), F32)] + [pltpu.VMEM((SUBLANES + lt, wide), F32)] * 3
                       + [pltpu.VMEM((LANES, lt), F32)],
        compiler_params=_cp("parallel", "arbitrary"),
        name="gated_deltanet",
    )(h1, h1, h1, h1, ba, conv_w, conv_w, conv_w, a_log_pad, dt_bias_pad, norm_w.reshape(1, d))


DMA_ISSUE_UNROLL = 8


def _moe_kernel(te_ref, nv_ref, rows_ref, src_ref, x_hbm, w1_ref, w3_ref, w2_ref, gate_ref, o_ref,
                rows_buf, xb_ref, sem, *, nj, tm, per_step):
    i = pl.program_id(0)
    j = pl.program_id(1)
    n_used = nv_ref[0]
    valid = i < n_used

    def start_rows(tile, first, count):
        for k in range(count):
            r = first + k
            tok = src_ref[tile * tm + jnp.minimum(r, tm - 1)]
            pltpu.make_async_copy(x_hbm.at[pl.ds(tok, 1), :], rows_buf.at[pl.ds(r, 1), :], sem).start()

    @pl.when(jnp.logical_and(i == 0, j == 0))
    def _():
        def body(s, _):
            start_rows(0, s * per_step, per_step)
            return 0
        lax.fori_loop(0, nj, body, 0)

    @pl.when(j == 0)
    def _():
        o_ref[...] = jnp.zeros_like(o_ref)

    @pl.when(jnp.logical_and(j == 0, i <= n_used))
    def _():
        pltpu.make_async_copy(x_hbm.at[pl.ds(0, nj * per_step), :], rows_buf, sem).wait()
        xb_ref[...] = rows_buf[0:tm, :].astype(BF16)

    half = tm // 2
    lower_only = rows_ref[i] <= half

    @pl.when(jnp.logical_and(valid, jnp.logical_not(lower_only)))
    def _():
        start_rows(i + 1, j * per_step, per_step)
        _swiglu_acc(xb_ref[...], w1_ref, w3_ref, w2_ref, o_ref)

    @pl.when(jnp.logical_and(valid, lower_only))
    def _():
        start_rows(i + 1, j * per_step, per_step)
        _swiglu_acc(xb_ref[0:half, :], w1_ref, w3_ref, w2_ref, o_ref.at[0:half, :])

    @pl.when(j == nj - 1)
    def _():
        o_ref[...] = o_ref[...] * gate_ref[...]


def _moe_ffn(x, src, gates, tile_expert, n_valid, tile_rows, w1, w3, w2, tm, tf):
    p = src.shape[0]
    d, dff = w1.shape[1], w1.shape[2]
    nj = dff // tf
    per_step = -(-tm // nj)
    while (nj * per_step) % SUBLANES:
        per_step += 1
    jsel = lambda i, j, nv: jnp.where(i < nv[0], j, nj - 1)
    return pl.pallas_call(
        functools.partial(_moe_kernel, nj=nj, tm=tm, per_step=per_step),
        grid_spec=pltpu.PrefetchScalarGridSpec(
            num_scalar_prefetch=4, grid=(p // tm, nj),
            in_specs=[pl.BlockSpec(memory_space=pl.ANY),
                      pl.BlockSpec((None, d, tf), lambda i, j, te, nv, rows, src: (te[i], 0, jsel(i, j, nv))),
                      pl.BlockSpec((None, d, tf), lambda i, j, te, nv, rows, src: (te[i], 0, jsel(i, j, nv))),
                      pl.BlockSpec((None, tf, d), lambda i, j, te, nv, rows, src: (te[i], jsel(i, j, nv), 0)),
                      pl.BlockSpec((tm, 1), lambda i, j, te, nv, rows, src: (i, 0))],
            out_specs=pl.BlockSpec((tm, d), lambda i, j, te, nv, rows, src: (i, 0)),
            scratch_shapes=[pltpu.VMEM((nj * per_step, d), F32), pltpu.VMEM((tm, d), BF16),
                            pltpu.SemaphoreType.DMA(())]),
        out_shape=jax.ShapeDtypeStruct((p, d), F32),
        compiler_params=_cp("arbitrary", "arbitrary"),
        name="moe_ffn",
    )(tile_expert, n_valid, tile_rows, src, x, w1, w3, w2, gates)


def _combine_kernel(pos_ref, ys_hbm, res_ref, g_ref, b_ref, o_ref, buf_ref, sem, *, tm, t):
    i = pl.program_id(0)

    def issue(step, slot):
        def body(r, _):
            for choice in range(2):
                p = pos_ref[choice * t + step * tm + r]
                pltpu.make_async_copy(ys_hbm.at[pl.ds(p, 1), :], buf_ref.at[slot, choice, pl.ds(r, 1), :],
                                      sem.at[slot, choice]).start()
            return 0
        lax.fori_loop(0, tm, body, 0, unroll=DMA_ISSUE_UNROLL)

    @pl.when(i == 0)
    def _():
        issue(0, 0)

    @pl.when(i + 1 < pl.num_programs(0))
    def _():
        issue(i + 1, (i + 1) % 2)

    slot = i % 2
    for choice in range(2):
        pltpu.make_async_copy(ys_hbm.at[pl.ds(0, tm), :], buf_ref.at[slot, choice],
                              sem.at[slot, choice]).wait()
    y = buf_ref[slot, 0] + buf_ref[slot, 1]
    o_ref[...] = _layer_norm(DEEPNORM_ALPHA * res_ref[...] + y, g_ref[...], b_ref[...])


def _moe_combine_ln(ys, pos, res, g, b, tm=256):
    t, d = res.shape
    return pl.pallas_call(
        functools.partial(_combine_kernel, tm=tm, t=t),
        grid_spec=pltpu.PrefetchScalarGridSpec(
            num_scalar_prefetch=1, grid=(t // tm,),
            in_specs=[pl.BlockSpec(memory_space=pl.ANY),
                      pl.BlockSpec((tm, d), lambda i, pos: (i, 0)),
                      pl.BlockSpec((1, d), lambda i, pos: (0, 0)),
                      pl.BlockSpec((1, d), lambda i, pos: (0, 0))],
            out_specs=pl.BlockSpec((tm, d), lambda i, pos: (i, 0)),
            scratch_shapes=[pltpu.VMEM((2, 2, tm, d), F32), pltpu.SemaphoreType.DMA((2, 2))]),
        out_shape=jax.ShapeDtypeStruct((t, d), F32),
        compiler_params=_cp("arbitrary"),
        name="moe_combine_ln",
    )(pos, ys, res, g.reshape(1, d), b.reshape(1, d))


def _moe_routing(route, tm):
    t = route.shape[0]
    e = N_EXPERTS
    idx = route[:, 0:2].astype(jnp.int32)
    wts = route[:, 2:4]
    flat_e = idx.T.reshape(-1)
    onehot = (flat_e[:, None] == jnp.arange(e, dtype=jnp.int32)[None, :]).astype(jnp.int32)
    rank = jnp.cumsum(onehot, axis=0) - onehot
    counts = jnp.sum(onehot, axis=0)
    tiles = (counts + tm - 1) // tm
    tile_end = jnp.cumsum(tiles)
    start = (tile_end - tiles) * tm
    pos = jnp.sum(onehot * (start[None, :] + rank), axis=1)
    n_slots = 2 * t + e * tm
    n_tiles = n_slots // tm
    owner = jnp.full((n_slots,), -1, jnp.int32).at[pos].set(jnp.arange(2 * t, dtype=jnp.int32))
    used = owner >= 0
    src = jnp.where(used, owner % t, jnp.arange(n_slots, dtype=jnp.int32) % t)
    gates = jnp.where(used, wts.T.reshape(-1)[jnp.maximum(owner, 0)], 0.0)
    n_valid = tile_end[-1]
    tile_ids = jnp.arange(n_tiles, dtype=jnp.int32)
    tile_expert = jnp.sum((tile_ids[:, None] >= tile_end[None, :]).astype(jnp.int32), axis=1)
    last_expert = jnp.sum((n_valid - 1 >= tile_end).astype(jnp.int32))
    tile_expert = jnp.where(tile_ids < n_valid, tile_expert, last_expert).astype(jnp.int32)
    tile_rows = jnp.clip((start + counts)[tile_expert] - tile_ids * tm, 0, tm)
    tile_rows = jnp.where(tile_ids < n_valid, tile_rows, 0).astype(jnp.int32)
    return (src, gates.reshape(n_slots, 1), pos.astype(jnp.int32), tile_expert,
            n_valid.reshape(1).astype(jnp.int32), tile_rows)


def _even_layer(x, xb, batch, seq, w_in, w_out, ln_mix_g, ln_mix_b, w1, w3, w2, ln_ffn_g, ln_ffn_b):
    t, d = x.shape
    w_sb = N_HEADS_SB * HEAD_DIM
    w_in_b = w_in.astype(BF16)
    ha = _matmul(xb, w_in_b[:, :3 * w_sb], BF16, 1024, 1024, "in_proj_sb")
    hb = _matmul(xb, w_in_b[:, 3 * w_sb:], F32, 1024, 1024, "in_proj_dw")
    oa = _sb_attention(ha.reshape(batch, seq, -1), batch, seq).reshape(t, -1)
    ob = _dw_attention(hb.reshape(batch, seq, -1), batch, seq).reshape(t, -1)
    w_out_b = w_out.astype(BF16)
    x, xb = _proj_ln(oa, ob, w_out_b[:w_sb], w_out_b[w_sb:], x, ln_mix_g, ln_mix_b)
    return _ffn_ln(xb, x, w1.astype(BF16), w3.astype(BF16), w2.astype(BF16), ln_ffn_g, ln_ffn_b)


def _odd_layer(x, xb, batch, seq, w_in, lam_re, lam_im, log_dt, b_re, b_im, c_re, c_im, d_skip,
               glu_w, glu_b, conv_w, a_log, dt_bias, norm_w, w_out, ln_mix_g, ln_mix_b,
               router_w, w1, w3, w2, ln_ffn_g, ln_ffn_b):
    t, d = x.shape
    nh = N_HEADS_GDN
    wide = SSM_WIDTH + 4 * nh * GDN_D
    w_in_b = w_in[:, :wide].astype(BF16)
    u = _matmul(xb, w_in_b[:, :SSM_WIDTH], F32, 1024, 1024, "in_proj_ssm")
    h1 = _matmul(xb, w_in_b[:, SSM_WIDTH:], BF16, 1024, 1024, "in_proj_gdn")
    w_small = jnp.pad(w_in[:, wide:], ((0, 0), (0, LANES - 2 * nh)))
    ba = _matmul(x, _split_hi_lo(w_small), F32, 512, LANES, "in_proj_gates")

    y = _s5_scan(u, lam_re, lam_im, log_dt, b_re, b_im, c_re, c_im, batch, seq)
    oc = _s5_glu(y, u, d_skip, glu_w.astype(BF16), glu_b)

    pad_hi = LANES - 2 * nh
    a_log_pad = jnp.pad(a_log, (nh, pad_hi)).reshape(1, LANES)
    dt_bias_pad = jnp.pad(dt_bias, (nh, pad_hi)).reshape(1, LANES)
    od = _gated_deltanet(h1, ba, conv_w, a_log_pad, dt_bias_pad, norm_w, batch, seq)

    w_out_b = w_out.astype(BF16)
    rw = _split_hi_lo(jnp.pad(router_w, ((0, 0), (0, LANES - N_EXPERTS))))
    x, route = _proj_ln(oc, od, w_out_b[:SSM_WIDTH], w_out_b[SSM_WIDTH:], x, ln_mix_g, ln_mix_b,
                        router_w=rw)

    tm = 1024
    src, gates, pos, tile_expert, n_valid, tile_rows = _moe_routing(route, tm)
    ys = _moe_ffn(x, src, gates, tile_expert, n_valid, tile_rows, w1, w3, w2, tm, tf=256)
    return _moe_combine_ln(ys, pos, x, ln_ffn_g, ln_ffn_b)


def kernel(x, even_w_in, even_w_out, even_ln_mix_g, even_ln_mix_b, even_ffn_w1, even_ffn_w3, even_ffn_w2, even_ln_ffn_g, even_ln_ffn_b, odd_w_in, odd_ssm_lam_re, odd_ssm_lam_im, odd_ssm_log_dt, odd_ssm_b_re, odd_ssm_b_im, odd_ssm_c_re, odd_ssm_c_im, odd_ssm_d, odd_glu_w, odd_glu_b, odd_gdn_conv_w, odd_gdn_a_log, odd_gdn_dt_bias, odd_gdn_norm_w, odd_w_out, odd_ln_mix_g, odd_ln_mix_b, odd_router_w, odd_moe_w1, odd_moe_w3, odd_moe_w2, odd_ln_ffn_g, odd_ln_ffn_b):
    batch, seq, d = x.shape
    xf = x.reshape(batch * seq, d)
    xf, xb = _even_layer(xf, xf.astype(BF16), batch, seq, even_w_in[0], even_w_out[0],
                         even_ln_mix_g[0], even_ln_mix_b[0], even_ffn_w1[0], even_ffn_w3[0],
                         even_ffn_w2[0], even_ln_ffn_g[0], even_ln_ffn_b[0])
    out = _odd_layer(xf, xb, batch, seq, odd_w_in[0], odd_ssm_lam_re[0], odd_ssm_lam_im[0],
                     odd_ssm_log_dt[0], odd_ssm_b_re[0], odd_ssm_b_im[0], odd_ssm_c_re[0],
                     odd_ssm_c_im[0], odd_ssm_d[0], odd_glu_w[0], odd_glu_b[0], odd_gdn_conv_w[0],
                     odd_gdn_a_log[0], odd_gdn_dt_bias[0], odd_gdn_norm_w[0], odd_w_out[0],
                     odd_ln_mix_g[0], odd_ln_mix_b[0], odd_router_w[0], odd_moe_w1[0],
                     odd_moe_w3[0], odd_moe_w2[0], odd_ln_ffn_g[0], odd_ln_ffn_b[0])
    return out.reshape(batch, seq, d)
```

```python
import functools
import math

import jax
import jax.numpy as jnp
from jax import lax
from jax.experimental import pallas as pl
from jax.experimental.pallas import tpu as pltpu

F32 = jnp.float32
BF16 = jnp.bfloat16
HIGHEST = lax.Precision.HIGHEST

HEAD_DIM = 128
N_HEADS_SB = 8
N_HEADS_DW = 8
DW_PATTERNS = ((128, 1), (512, 4), (2048, 16))
DW_KEYS = 128
DW_TILE = 2048
SSM_WIDTH = 1024
SSM_GROUP = 16
SSM_GROUPS = 64
SSM_STATE = 64
SSM_CHUNK = 8
N_HEADS_GDN = 8
GDN_D = 128
GDN_CONV = 4
GDN_TILE = 256
N_EXPERTS = 8
DEPTH = 2
DEEPNORM_ALPHA = (2 * DEPTH) ** 0.25
LN_EPS = 1e-5
RMS_EPS = 1e-6

LANES = 128
SUBLANES = 8
VMEM_LIMIT = 56 * 1024 * 1024
MM_TILE = (1024, 1024)
MOE_ROWS = 1024
MOE_FF_CHUNK = 256
SB_SKIP_LOG = -104.0


def _cp(*sem):
    return pltpu.CompilerParams(dimension_semantics=sem, vmem_limit_bytes=VMEM_LIMIT)


def _layer_norm(y, g, b):
    mu = jnp.mean(y, axis=-1, keepdims=True)
    yc = y - mu
    var = jnp.mean(yc * yc, axis=-1, keepdims=True)
    return yc * lax.rsqrt(var + LN_EPS) * g + b


def _softplus(x):
    return jnp.maximum(x, 0.0) + jnp.log1p(jnp.exp(-jnp.abs(x)))


def _dot(a, b, precision=None):
    return jnp.dot(a, b, preferred_element_type=F32, precision=precision)


def _dot_nt(a, b, precision=None):
    return lax.dot_general(a, b, (((1,), (1,)), ((), ())),
                           preferred_element_type=F32, precision=precision)


def _dot_tn(a, b, precision=None):
    return lax.dot_general(a, b, (((0,), (0,)), ((), ())),
                           preferred_element_type=F32, precision=precision)


def _split_hi_lo(w):
    hi = w.astype(BF16)
    return jnp.stack([hi, (w - hi.astype(F32)).astype(BF16)])


def _dot_split(x, w_ref):
    xh = x.astype(BF16)
    xl = (x - xh.astype(F32)).astype(BF16)
    wh = w_ref[0]
    return _dot(xh, wh) + _dot(xl, wh) + _dot(xh, w_ref[1])


def _mm_kernel(a_ref, b_ref, o_ref):
    o_ref[...] = _dot(a_ref[...].astype(BF16), b_ref[...]).astype(o_ref.dtype)


def _mm_split_kernel(a_ref, b_ref, o_ref):
    o_ref[...] = _dot_split(a_ref[...], b_ref).astype(o_ref.dtype)


def _matmul(a, b, out_dtype, tm, tn, name):
    m, k = a.shape
    n = b.shape[-1]
    if b.ndim == 3:
        body, b_spec = _mm_split_kernel, pl.BlockSpec((2, k, tn), lambda i, j: (0, 0, j))
    else:
        body, b_spec = _mm_kernel, pl.BlockSpec((k, tn), lambda i, j: (0, j))
    return pl.pallas_call(
        body,
        grid=(m // tm, n // tn),
        in_specs=[pl.BlockSpec((tm, k), lambda i, j: (i, 0)), b_spec],
        out_specs=pl.BlockSpec((tm, tn), lambda i, j: (i, j)),
        out_shape=jax.ShapeDtypeStruct((m, n), out_dtype),
        compiler_params=_cp("parallel", "parallel"),
        name=name,
    )(a, b)


def _proj_ln_kernel(a0_ref, a1_ref, w0_ref, w1_ref, res_ref, g_ref, b_ref, *rest, with_router):
    if with_router:
        rw_ref, of_ref, rt_ref = rest
    else:
        of_ref, ob_ref = rest
    mix = _dot(a0_ref[...], w0_ref[...]) + _dot(a1_ref[...], w1_ref[...])
    xn = _layer_norm(DEEPNORM_ALPHA * res_ref[...] + mix, g_ref[...], b_ref[...])
    of_ref[...] = xn
    if with_router:
        logits = _dot_split(xn, rw_ref)
        lane = lax.broadcasted_iota(jnp.int32, logits.shape, 1).astype(F32)
        neg = jnp.float32(-jnp.inf)
        lg = jnp.where(lane < N_EXPERTS, logits, neg)
        m1 = jnp.max(lg, axis=1, keepdims=True)
        i1 = jnp.min(jnp.where(lg == m1, lane, float(LANES)), axis=1, keepdims=True)
        lg2 = jnp.where(lane == i1, neg, lg)
        m2 = jnp.max(lg2, axis=1, keepdims=True)
        i2 = jnp.min(jnp.where(lg2 == m2, lane, float(LANES)), axis=1, keepdims=True)
        e2 = jnp.exp(m2 - m1)
        p1 = 1.0 / (1.0 + e2)
        p2 = e2 * p1
        rt = jnp.where(lane == 0.0, i1, jnp.where(lane == 1.0, i2,
                       jnp.where(lane == 2.0, p1, jnp.where(lane == 3.0, p2, 0.0))))
        rt_ref[...] = rt
    else:
        ob_ref[...] = xn.astype(BF16)


def _proj_ln(a0, a1, w0, w1, res, g, b, router_w=None, tm=256):
    t, d = res.shape
    k0, k1 = a0.shape[1], a1.shape[1]
    with_router = router_w is not None
    row = lambda i: (i, 0)
    fixed = lambda i: (0, 0)
    in_specs = [pl.BlockSpec((tm, k0), row), pl.BlockSpec((tm, k1), row),
                pl.BlockSpec((k0, d), fixed), pl.BlockSpec((k1, d), fixed),
                pl.BlockSpec((tm, d), row), pl.BlockSpec((1, d), fixed), pl.BlockSpec((1, d), fixed)]
    args = [a0, a1, w0, w1, res, g.reshape(1, d), b.reshape(1, d)]
    if with_router:
        in_specs.append(pl.BlockSpec((2, d, LANES), lambda i: (0, 0, 0)))
        out_specs = [pl.BlockSpec((tm, d), row), pl.BlockSpec((tm, LANES), row)]
        out_shape = [jax.ShapeDtypeStruct((t, d), F32), jax.ShapeDtypeStruct((t, LANES), F32)]
        args.append(router_w)
    else:
        out_specs = [pl.BlockSpec((tm, d), row), pl.BlockSpec((tm, d), row)]
        out_shape = [jax.ShapeDtypeStruct((t, d), F32), jax.ShapeDtypeStruct((t, d), BF16)]
    return pl.pallas_call(
        functools.partial(_proj_ln_kernel, with_router=with_router),
        grid=(t // tm,), in_specs=in_specs, out_specs=out_specs, out_shape=out_shape,
        compiler_params=_cp("parallel"),
        name="proj_ln_router" if with_router else "proj_ln",
    )(*args)


def _swiglu_acc(x, w1_ref, w3_ref, w2_ref, acc_ref):
    h1 = _dot(x, w1_ref[...].astype(BF16))
    h3 = _dot(x, w3_ref[...].astype(BF16))
    act = (h1 * jax.nn.sigmoid(h1) * h3).astype(BF16)
    acc_ref[...] += _dot(act, w2_ref[...].astype(BF16))


def _ffn_kernel(x_ref, w1_ref, w3_ref, w2_ref, res_ref, g_ref, b_ref, of_ref, ob_ref, acc_ref, *, nj):
    j = pl.program_id(1)

    @pl.when(j == 0)
    def _():
        acc_ref[...] = jnp.zeros_like(acc_ref)

    _swiglu_acc(x_ref[...], w1_ref, w3_ref, w2_ref, acc_ref)

    @pl.when(j == nj - 1)
    def _():
        xn = _layer_norm(DEEPNORM_ALPHA * res_ref[...] + acc_ref[...], g_ref[...], b_ref[...])
        of_ref[...] = xn
        ob_ref[...] = xn.astype(BF16)


def _ffn_ln(xb, res, w1, w3, w2, g, b, tm=512, tf=512):
    t, d = res.shape
    dff = w1.shape[1]
    nj = dff // tf
    row = lambda i, j: (i, 0)
    fixed = lambda i, j: (0, 0)
    return pl.pallas_call(
        functools.partial(_ffn_kernel, nj=nj),
        grid=(t // tm, nj),
        in_specs=[pl.BlockSpec((tm, d), row),
                  pl.BlockSpec((d, tf), lambda i, j: (0, j)),
                  pl.BlockSpec((d, tf), lambda i, j: (0, j)),
                  pl.BlockSpec((tf, d), lambda i, j: (j, 0)),
                  pl.BlockSpec((tm, d), row), pl.BlockSpec((1, d), fixed), pl.BlockSpec((1, d), fixed)],
        out_specs=[pl.BlockSpec((tm, d), row), pl.BlockSpec((tm, d), row)],
        out_shape=[jax.ShapeDtypeStruct((t, d), F32), jax.ShapeDtypeStruct((t, d), BF16)],
        scratch_shapes=[pltpu.VMEM((tm, d), F32)],
        compiler_params=_cp("parallel", "arbitrary"),
        name="ffn_ln",
    )(xb, w1, w3, w2, res, g.reshape(1, d), b.reshape(1, d))


SB_QTILES = 4


def _sb_kernel(q_ref, k_ref, v_ref, o_ref, acc_ref, carry_ref, *, tq, scale):
    first_tile = pl.program_id(1) * SB_QTILES
    row = lax.broadcasted_iota(jnp.int32, (tq, tq), 0)
    col = lax.broadcasted_iota(jnp.int32, (tq, tq), 1)
    later_sum = (row > col).astype(BF16)
    past = col < row
    qs = [q_ref[t * tq:(t + 1) * tq, :] for t in range(SB_QTILES)]

    def blocks_terms(chains):
        starts = [pl.multiple_of(kb * tq, tq) for _, kb, _ in chains]
        z = [_dot_nt(q, k_ref[pl.ds(s, tq), :]) * scale for (q, _, _), s in zip(chains, starts)]
        lk = [-_softplus(zz) for zz in z]
        lk = [jnp.where(past, l, 0.0) if dg else l for l, (_, _, dg) in zip(lk, chains)]
        hi = [l.astype(BF16) for l in lk]
        lo = [(l - h.astype(F32)).astype(BF16) for l, h in zip(lk, hi)]
        later = [_dot(h, later_sum) + _dot(l, later_sum) for h, l in zip(hi, lo)]
        logw = [zz + l + lt for zz, l, lt in zip(z, lk, later)]
        tot = [jnp.sum(l, axis=1, keepdims=True) for l in lk]
        return logw, tot, [v_ref[pl.ds(s, tq), :] for s in starts]

    chains = []
    for t in range(SB_QTILES):
        chains += [(qs[t], first_tile + t, True), (qs[t], jnp.maximum(first_tile + t - 1, 0), False)]
    logw, tot, vals = blocks_terms(chains)
    for t in range(SB_QTILES):
        has_prev = first_tile + t >= 1
        w0 = jnp.where(past, jnp.exp(logw[2 * t]), 0.0)
        w1 = jnp.where(has_prev, jnp.exp(logw[2 * t + 1] + tot[2 * t]), 0.0)
        acc_ref[t] = _dot(w0.astype(BF16), vals[2 * t]) + _dot(w1.astype(BF16), vals[2 * t + 1])
        carry_ref[t] = tot[2 * t] + jnp.where(has_prev, tot[2 * t + 1], 0.0)

    for t in range(SB_QTILES):
        def cond(kb, t=t):
            return jnp.logical_and(kb >= 0, jnp.max(carry_ref[t]) > SB_SKIP_LOG)

        def body(kb, t=t):
            (lw,), (tt,), (v,) = blocks_terms([(qs[t], kb, False)])
            acc_ref[t] += _dot(jnp.exp(lw + carry_ref[t]).astype(BF16), v)
            carry_ref[t] += tt
            return kb - 1

        lax.while_loop(cond, body, first_tile + t - 2)
        o_ref[t * tq:(t + 1) * tq, :] = acc_ref[t].astype(o_ref.dtype)


def _sb_attention(h, batch, seq, tq=256):
    nh = N_HEADS_SB
    rows = SB_QTILES * tq
    return pl.pallas_call(
        functools.partial(_sb_kernel, tq=tq, scale=HEAD_DIM ** -0.5),
        grid=(batch * nh, seq // rows),
        in_specs=[pl.BlockSpec((None, rows, HEAD_DIM), lambda bh, i: (bh // nh, i, bh % nh)),
                  pl.BlockSpec((None, seq, HEAD_DIM), lambda bh, i: (bh // nh, 0, nh + bh % nh)),
                  pl.BlockSpec((None, seq, HEAD_DIM), lambda bh, i: (bh // nh, 0, 2 * nh + bh % nh))],
        out_specs=pl.BlockSpec((None, rows, HEAD_DIM), lambda bh, i: (bh // nh, i, bh % nh)),
        out_shape=jax.ShapeDtypeStruct((batch, seq, nh * HEAD_DIM), BF16),
        scratch_shapes=[pltpu.VMEM((SB_QTILES, tq, HEAD_DIM), F32), pltpu.VMEM((SB_QTILES, tq, 1), F32)],
        compiler_params=_cp("parallel", "arbitrary"),
        name="sb_attention",
    )(h, h, h)


DW_UNROLL = 8


def _dw_kernel(q_ref, k_ref, v_ref, o_ref, m_ref, l_ref, acc_ref, kp_ref, vp_ref, *, scale):
    nk = DW_KEYS
    t0 = pl.program_id(1) * DW_TILE

    @pl.when(pl.program_id(1) == 0)
    def _():
        kp_ref[...] = jnp.zeros_like(kp_ref)
        vp_ref[...] = jnp.zeros_like(vp_ref)

    ii = lax.broadcasted_iota(jnp.int32, (nk, 2 * nk), 0)
    jj = lax.broadcasted_iota(jnp.int32, (nk, 2 * nk), 1)
    band = jnp.logical_and(jj >= ii, jj <= ii + nk)
    band_prev = jnp.logical_and(band, jj < nk)
    band_own = jnp.logical_and(band, jj >= nk)
    neg = jnp.float32(-jnp.inf)

    for p_idx, (window, dil) in enumerate(DW_PATTERNS):
        assert window // dil == nk
        span = nk * dil
        n_steps = DW_TILE // nk

        def group(g, _, dil=dil, span=span, p_idx=p_idx):
            subs = range(DW_UNROLL)
            blk = [(g * DW_UNROLL + u) // dil for u in subs]
            res = [(g * DW_UNROLL + u) % dil for u in subs]
            off = [b * span + r for b, r in zip(blk, res)]
            rows = [pl.ds(o, nk, stride=dil) for o in off]
            cur = [pl.ds(t0 + o, nk, stride=dil) for o in off]
            has_prev = [(t0 + b * span) > 0 for b in blk]
            prev = [pl.ds(jnp.maximum(t0 + b * span - span, 0) + r, nk, stride=dil) for b, r in zip(blk, res)]
            qs = [q_ref[rw, :].astype(BF16) for rw in rows]
            k_cur = [k_ref[cr, :].astype(BF16) for cr in cur]
            v_cur = [v_ref[cr, :].astype(BF16) for cr in cur]
            if span == DW_TILE:
                k_prev = [kp_ref[r] for r in res]
                v_prev = [vp_ref[r] for r in res]
                for r, kc, vc in zip(res, k_cur, v_cur):
                    kp_ref[r] = kc
                    vp_ref[r] = vc
            else:
                k_prev = [k_ref[pv, :].astype(BF16) for pv in prev]
                v_prev = [v_ref[pv, :].astype(BF16) for pv in prev]
            keys = [jnp.concatenate([kp, kc], axis=0) for kp, kc in zip(k_prev, k_cur)]
            sc = [_dot_nt(qq, kk) * scale for qq, kk in zip(qs, keys)]
            sc = [jnp.where(jnp.logical_or(band_own, jnp.logical_and(band_prev, hp)), s, neg)
                  for s, hp in zip(sc, has_prev)]
            m_new = [jnp.max(s, axis=1, keepdims=True) for s in sc]
            p = [jnp.exp(s - m) for s, m in zip(sc, m_new)]
            l_new = [jnp.sum(pp, axis=1, keepdims=True) for pp in p]
            vals = [jnp.concatenate([vp, vc], axis=0) for vp, vc in zip(v_prev, v_cur)]
            num = [_dot(pp.astype(BF16), vv) for pp, vv in zip(p, vals)]
            for rw, m, l, nm in zip(rows, m_new, l_new, num):
                m_ref[p_idx, rw, :] = jnp.broadcast_to(m, (nk, HEAD_DIM))
                l_ref[p_idx, rw, :] = jnp.broadcast_to(l, (nk, HEAD_DIM))
                acc_ref[p_idx, rw, :] = nm
            return 0

        lax.fori_loop(0, n_steps // DW_UNROLL, group, 0)

    m_all = m_ref[...]
    m_tot = jnp.max(m_all, axis=0)
    w = jnp.exp(m_all - m_tot[None])
    den = jnp.sum(w * l_ref[...], axis=0)
    num = jnp.sum(w * acc_ref[...], axis=0)
    o_ref[...] = (num / den).astype(o_ref.dtype)


def _dw_attention(h, batch, seq):
    nh = N_HEADS_DW
    return pl.pallas_call(
        functools.partial(_dw_kernel, scale=HEAD_DIM ** -0.5),
        grid=(batch * nh, seq // DW_TILE),
        in_specs=[pl.BlockSpec((None, DW_TILE, HEAD_DIM), lambda bh, i: (bh // nh, i, bh % nh)),
                  pl.BlockSpec((None, seq, HEAD_DIM), lambda bh, i: (bh // nh, 0, nh + bh % nh)),
                  pl.BlockSpec((None, seq, HEAD_DIM), lambda bh, i: (bh // nh, 0, 2 * nh + bh % nh))],
        out_specs=pl.BlockSpec((None, DW_TILE, HEAD_DIM), lambda bh, i: (bh // nh, i, bh % nh)),
        out_shape=jax.ShapeDtypeStruct((batch, seq, nh * HEAD_DIM), BF16),
        scratch_shapes=[pltpu.VMEM((len(DW_PATTERNS), DW_TILE, HEAD_DIM), F32)] * 3
                       + [pltpu.VMEM((DW_TILE // DW_KEYS, DW_KEYS, HEAD_DIM), BF16)] * 2,
        compiler_params=_cp("parallel", "arbitrary"),
        name="dw_attention",
    )(h, h, h)


def _s5_params(lam_re, lam_im, log_dt, b_re, b_im, c_re, c_im, n_scan):
    L = SSM_CHUNK
    gpl = LANES // SSM_GROUP
    nlb = SSM_GROUPS // gpl
    dt = jnp.exp(log_dt)[:, None]
    mag_log = lam_re * dt
    ang = lam_im * dt

    def power(n):
        n = jnp.asarray(n, F32)[..., None, None]
        mag = jnp.exp(mag_log * n)
        return mag * jnp.cos(ang * n), mag * jnp.sin(ang * n)

    lr, li = power(jnp.ones(()))
    den = lam_re * lam_re + lam_im * lam_im
    cr = ((lr - 1.0) * lam_re + li * lam_im) / den
    ci = (li * lam_re - (lr - 1.0) * lam_im) / den
    bbr = cr[..., None] * b_re - ci[..., None] * b_im
    bbi = cr[..., None] * b_im + ci[..., None] * b_re

    pr, pi = power(jnp.arange(L + 1, dtype=F32))
    mr = pr[:L, :, :, None] * bbr - pi[:L, :, :, None] * bbi
    mi = pr[:L, :, :, None] * bbi + pi[:L, :, :, None] * bbr
    kk = (jnp.einsum('ghp,tgpk->tghk', c_re, mr, precision=HIGHEST)
          - jnp.einsum('ghp,tgpk->tghk', c_im, mi, precision=HIGHEST))
    jj = jnp.arange(L)[:, None]
    ii = jnp.arange(L)[None, :]
    lag = ii - jj
    toe = jnp.where((lag >= 0)[:, :, None, None, None], kk[jnp.clip(lag, 0, L - 1)], 0.0)
    toe = toe.reshape(L, L, nlb, gpl, SSM_GROUP, SSM_GROUP)
    c_toe = toe.transpose(2, 0, 3, 5, 1, 4).reshape(nlb, L * LANES, L * SSM_GROUP)
    qr = pr[L - 1 - jnp.arange(L)]
    qi = pi[L - 1 - jnp.arange(L)]
    inr = (qr[..., None] * bbr - qi[..., None] * bbi).reshape(L, nlb, gpl, SSM_STATE, SSM_GROUP)
    ini = (qr[..., None] * bbi + qi[..., None] * bbr).reshape(L, nlb, gpl, SSM_STATE, SSM_GROUP)
    c_in = jnp.concatenate(
        [part.transpose(1, 0, 2, 4, 3).reshape(nlb, L * LANES, SSM_STATE) for part in (inr, ini)], axis=2)
    orr = c_re[None] * pr[1:, :, None, :] - c_im[None] * pi[1:, :, None, :]
    oii = c_re[None] * pi[1:, :, None, :] + c_im[None] * pr[1:, :, None, :]
    half = gpl * SSM_STATE
    c_out = jnp.concatenate(
        [part.reshape(L, nlb, gpl, SSM_GROUP, SSM_STATE).transpose(1, 2, 4, 0, 3)
         .reshape(nlb, half, L * SSM_GROUP) for part in (orr, -oii)], axis=1)
    ar, ai = power(float(L) * (2.0 ** jnp.arange(n_scan, dtype=F32)))
    ar = ar.reshape(n_scan, nlb, half).transpose(1, 0, 2)
    ai = ai.reshape(n_scan, nlb, half).transpose(1, 0, 2)
    return c_toe.astype(BF16), c_in.astype(BF16), c_out.astype(BF16), ar, ai


def _spread_matrix(inner, reps, period):
    rows = jnp.arange(inner * period)
    cols = jnp.arange(inner * reps * period)
    same_a = (rows[:, None] // period) == (cols[None, :] // (reps * period))
    same_c = (rows[:, None] % period) == (cols[None, :] % period)
    return jnp.logical_and(same_a, same_c).astype(BF16)


def _spread_groups(compact, spread, row_period, col_period):
    gpl = LANES // SSM_GROUP
    full = _dot(compact, spread)
    rg = (lax.broadcasted_iota(jnp.int32, full.shape, 0) // row_period) % gpl
    cg = (lax.broadcasted_iota(jnp.int32, full.shape, 1) // col_period) % gpl
    return jnp.where(rg == cg, full, 0.0).astype(BF16)


def _s5_kernel(x_ref, ct_ref, ci_ref, co_ref, et_ref, ei_ref, ar_ref, ai_ref, y_ref,
               wt_ref, wi_ref, wo_ref, sr_ref, si_ref, *, n_scan, m):
    L = SSM_CHUNK

    @pl.when(jnp.logical_and(pl.program_id(1) == 0, pl.program_id(2) == 0))
    def _():
        wt_ref[...] = _spread_groups(ct_ref[...], et_ref[...], SSM_GROUP, SSM_GROUP)
        wi_ref[...] = _spread_groups(ci_ref[...], ei_ref[...], SSM_GROUP, SSM_STATE)
        wo_ref[...] = _spread_groups(co_ref[...], et_ref[...], SSM_STATE, SSM_GROUP)

    @pl.when(pl.program_id(2) == 0)
    def _():
        sr_ref[...] = jnp.zeros_like(sr_ref)
        si_ref[...] = jnp.zeros_like(si_ref)

    xc = jnp.concatenate([x_ref[pl.ds(j, m, stride=L), :].astype(BF16) for j in range(L)], axis=1)
    yc = _dot(xc, wt_ref[...])
    z = _dot(xc, wi_ref[...])
    half = z.shape[1] // 2
    zr, zi = z[:, :half], z[:, half:]
    row = lax.broadcasted_iota(jnp.int32, (m, half), 0)
    pr, pi = sr_ref[0:1, :], si_ref[0:1, :]
    a1r, a1i = ar_ref[0:1, :], ai_ref[0:1, :]
    first = row == 0
    zr, zi = (zr + jnp.where(first, a1r * pr - a1i * pi, 0.0),
              zi + jnp.where(first, a1r * pi + a1i * pr, 0.0))
    for k in range(n_scan):
        s = 1 << k
        ar = ar_ref[k:k + 1, :]
        ai = ai_ref[k:k + 1, :]
        keep = row >= s
        tr = jnp.where(keep, pltpu.roll(zr, s, 0), 0.0)
        ti = jnp.where(keep, pltpu.roll(zi, s, 0), 0.0)
        zr, zi = zr + ar * tr - ai * ti, zi + ar * ti + ai * tr
    keep = row >= 1
    s_in = jnp.concatenate([jnp.where(keep, pltpu.roll(zr, 1, 0), pr),
                            jnp.where(keep, pltpu.roll(zi, 1, 0), pi)], axis=1)
    sr_ref[...] = jnp.broadcast_to(zr[m - 1:m, :], sr_ref.shape)
    si_ref[...] = jnp.broadcast_to(zi[m - 1:m, :], si_ref.shape)
    yc = yc + _dot(s_in.astype(BF16), wo_ref[...])
    for i in range(L):
        y_ref[pl.ds(i, m, stride=L), :] = yc[:, i * LANES:(i + 1) * LANES]


def _s5_scan(h1, lam_re, lam_im, log_dt, b_re, b_im, c_re, c_im, batch, seq, rows=2048):
    L = SSM_CHUNK
    rows = min(rows, seq)
    m = rows // L
    nt = seq // rows
    n_scan = max(1, math.ceil(math.log2(m)))
    c_toe, c_in, c_out, ar, ai = _s5_params(lam_re, lam_im, log_dt, b_re, b_im, c_re, c_im, n_scan)
    nlb, half = ar.shape[0], ar.shape[2]
    gpl = LANES // SSM_GROUP
    e_toe = _spread_matrix(L, gpl, SSM_GROUP)
    e_in = _spread_matrix(2, gpl, SSM_STATE)
    per_lb = lambda *blk: pl.BlockSpec((None,) + blk, lambda l, b, i: (l, 0, 0))
    whole = lambda a: pl.BlockSpec(a.shape, lambda l, b, i: (0, 0))
    tile = pl.BlockSpec((rows, LANES), lambda l, b, i: (b * nt + i, l))
    wide = L * LANES
    return pl.pallas_call(
        functools.partial(_s5_kernel, n_scan=n_scan, m=m),
        grid=(nlb, batch, nt),
        in_specs=[tile, per_lb(wide, c_toe.shape[2]), per_lb(wide, c_in.shape[2]),
                  per_lb(2 * half, c_out.shape[2]), whole(e_toe), whole(e_in),
                  per_lb(n_scan, half), per_lb(n_scan, half)],
        out_specs=tile,
        out_shape=jax.ShapeDtypeStruct((batch * seq, SSM_WIDTH), F32),
        scratch_shapes=[pltpu.VMEM((wide, wide), BF16), pltpu.VMEM((wide, 2 * half), BF16),
                        pltpu.VMEM((2 * half, wide), BF16)] + [pltpu.VMEM((SUBLANES, half), F32)] * 2,
        compiler_params=_cp("parallel", "arbitrary", "arbitrary"),
        name="s5_scan",
    )(h1, c_toe, c_in, c_out, e_toe, e_in, ar, ai)


def _s5_glu_kernel(y_ref, u_ref, d_ref, w_ref, b_ref, o_ref):
    y = y_ref[...] + d_ref[...] * u_ref[...]
    c = math.sqrt(2.0 / math.pi)
    z = 0.5 * y * (1.0 + jnp.tanh(c * (y + 0.044715 * (y * y * y))))
    gate = jax.nn.sigmoid(_dot(z.astype(BF16), w_ref[...]) + b_ref[...])
    o_ref[...] = (z * gate).astype(o_ref.dtype)


def _s5_glu(y, h1, d_skip, glu_w, glu_b, tm=512):
    t, w = y.shape
    row = lambda i: (i, 0)
    fixed = lambda i: (0, 0)
    return pl.pallas_call(
        _s5_glu_kernel,
        grid=(t // tm,),
        in_specs=[pl.BlockSpec((tm, w), row), pl.BlockSpec((tm, w), row),
                  pl.BlockSpec((1, w), fixed), pl.BlockSpec((w, w), fixed), pl.BlockSpec((1, w), fixed)],
        out_specs=pl.BlockSpec((tm, w), row),
        out_shape=jax.ShapeDtypeStruct((t, w), BF16),
        compiler_params=_cp("parallel"),
        name="s5_glu",
    )(y, h1, d_skip.reshape(1, w), glu_w, glu_b.reshape(1, w))


def _bdot(a, b):
    return _dot(a.astype(BF16), b.astype(BF16))


def _bdot_nt(a, b):
    return _dot_nt(a.astype(BF16), b.astype(BF16))


def _gdn_kernel(q_ref, k_ref, v_ref, gate_ref, ba_ref, wq_ref, wk_ref, wv_ref, alog_ref, dtb_ref,
                nw_ref, o_ref, state_ref, hq_ref, hk_ref, hv_ref, gct_ref, *, lt, nh, hp):
    d = GDN_D
    c = lt
    head0 = (pl.program_id(0) % (nh // hp)) * hp

    @pl.when(pl.program_id(1) == 0)
    def _():
        state_ref[...] = jnp.zeros_like(state_ref)
        for halo_ref in (hq_ref, hk_ref, hv_ref):
            halo_ref[0:SUBLANES, :] = jnp.zeros((SUBLANES, hp * d), F32)

    def conv_silu(x_ref, w_ref, xe_ref):
        w = w_ref[...]
        xe_ref[SUBLANES:, :] = x_ref[...].astype(F32)
        y = xe_ref[SUBLANES:, :] * w[GDN_CONV - 1:GDN_CONV, :]
        for back in range(1, GDN_CONV):
            y = y + xe_ref[pl.ds(SUBLANES - back, lt), :] * w[GDN_CONV - 1 - back:GDN_CONV - back, :]
        xe_ref[0:SUBLANES, :] = xe_ref[lt:, :]
        return y * jax.nn.sigmoid(y)

    def l2norm(t):
        return t * lax.rsqrt(jnp.sum(t * t, axis=-1, keepdims=True) + RMS_EPS)

    q_cs = conv_silu(q_ref, wq_ref, hq_ref)
    k_cs = conv_silu(k_ref, wk_ref, hk_ref)
    v_cs = conv_silu(v_ref, wv_ref, hv_ref)

    ba = ba_ref[...]
    lane = lax.broadcasted_iota(jnp.int32, ba.shape, 1)
    sig_ba = jax.nn.sigmoid(ba)
    g_full = -jnp.exp(alog_ref[...]) * _softplus(ba + dtb_ref[...])
    gc_full = g_full
    rows_i = lax.broadcasted_iota(jnp.int32, gc_full.shape, 0)
    step = 1
    while step < c:
        gc_full = gc_full + jnp.where(rows_i >= step, pltpu.roll(gc_full, step, 0), 0.0)
        step *= 2
    gct_ref[...] = gc_full.T

    ri = lax.broadcasted_iota(jnp.int32, (c, c), 0)
    ci = lax.broadcasted_iota(jnp.int32, (c, c), 1)
    tri = ri >= ci
    strict = ri > ci
    eye = (ri == ci).astype(F32)
    in16 = (ri // 16) == (ci // 16)
    merges = []
    width = 16
    while width < c:
        inner = (ri // width) == (ci // width)
        outer = (ri // (2 * width)) == (ci // (2 * width))
        merges.append(jnp.logical_and(outer, jnp.logical_not(inner)))
        width *= 2

    heads = range(hp)
    cols = [slice(hh * d, (hh + 1) * d) for hh in heads]
    q = [l2norm(q_cs[:, cols[hh]]) * (d ** -0.5) for hh in heads]
    k = [l2norm(k_cs[:, cols[hh]]) for hh in heads]
    v = [v_cs[:, cols[hh]] for hh in heads]
    beta = [jnp.sum(jnp.where(lane == head0 + hh, sig_ba, 0.0), axis=1, keepdims=True) for hh in heads]
    gc = [jnp.sum(jnp.where(lane == nh + head0 + hh, gc_full, 0.0), axis=1, keepdims=True)
          for hh in heads]
    gc_row = [gct_ref[pl.ds(nh + head0 + hh, 1), :] for hh in heads]
    gc_last = [gc_row[hh][:, c - 1:c] for hh in heads]
    decay = [jnp.exp(jnp.where(tri, gc[hh] - gc_row[hh], -jnp.inf)) for hh in heads]
    kb = [k[hh] * beta[hh] for hh in heads]
    kq_kt = [_bdot_nt(jnp.concatenate([kb[hh], q[hh]], axis=0), k[hh]) for hh in heads]
    lower = [jnp.where(strict, kq_kt[hh][:c] * decay[hh], 0.0) for hh in heads]
    pw = [jnp.where(in16, -lower[hh], 0.0) for hh in heads]
    inv = [eye + pw[hh] for hh in heads]
    for _ in range(3):
        pw = [_bdot(pw[hh], pw[hh]) for hh in heads]
        inv = [inv[hh] + _bdot(inv[hh], pw[hh]) for hh in heads]
    for off in merges:
        part = [_bdot(inv[hh], jnp.where(off, lower[hh], 0.0)) for hh in heads]
        inv = [inv[hh] - _bdot(part[hh], inv[hh]) for hh in heads]
    e_gc = [jnp.exp(gc[hh]) for hh in heads]
    uw = [_bdot(inv[hh], jnp.concatenate([v[hh] * beta[hh], kb[hh] * e_gc[hh]], axis=1)) for hh in heads]
    state = [state_ref[hh] for hh in heads]
    ws_qs = [_bdot(jnp.concatenate([uw[hh][:, d:], q[hh] * e_gc[hh]], axis=0), state[hh]) for hh in heads]
    v_new = [uw[hh][:, :d] - ws_qs[hh][:c] for hh in heads]
    o = [ws_qs[hh][c:] + _bdot(kq_kt[hh][c:] * decay[hh], v_new[hh]) for hh in heads]
    for hh in heads:
        k_dec = k[hh] * jnp.exp(gc_last[hh] - gc[hh])
        state_ref[hh] = (state[hh] * jnp.exp(gc_last[hh])
                         + _dot_tn(k_dec.astype(BF16), v_new[hh].astype(BF16)))
        on = o[hh] * lax.rsqrt(jnp.mean(o[hh] * o[hh], axis=-1, keepdims=True) + RMS_EPS) * nw_ref[...]
        gt = gate_ref[:, cols[hh]].astype(F32)
        o_ref[:, cols[hh]] = (on * (gt * jax.nn.sigmoid(gt))).astype(o_ref.dtype)


def _gated_deltanet(h1, ba, conv_w, a_log_pad, dt_bias_pad, norm_w, batch, seq, lt=GDN_TILE, hp=8):
    nh = N_HEADS_GDN
    d = GDN_D
    t = batch * seq
    nt = seq // lt
    ng = nh // hp
    wide = hp * d
    base = 0
    rows = lambda off: (lambda bg, i: ((bg // ng) * nt + i, off + bg % ng))
    cw = lambda off: (lambda bg, i: (0, off + bg % ng))
    fixed = lambda bg, i: (0, 0)
    return pl.pallas_call(
        functools.partial(_gdn_kernel, lt=lt, nh=nh, hp=hp),
        grid=(batch * ng, nt),
        in_specs=[pl.BlockSpec((lt, wide), rows(base)), pl.BlockSpec((lt, wide), rows(base + ng)),
                  pl.BlockSpec((lt, wide), rows(base + 2 * ng)), pl.BlockSpec((lt, wide), rows(base + 3 * ng)),
                  pl.BlockSpec((lt, LANES), lambda bg, i: ((bg // ng) * nt + i, 0)),
                  pl.BlockSpec((GDN_CONV, wide), cw(0)), pl.BlockSpec((GDN_CONV, wide), cw(ng)),
                  pl.BlockSpec((GDN_CONV, wide), cw(2 * ng)),
                  pl.BlockSpec((1, LANES), fixed), pl.BlockSpec((1, LANES), fixed),
                  pl.BlockSpec((1, d), fixed)],
        out_specs=pl.BlockSpec((lt, wide), rows(0)),
        out_shape=jax.ShapeDtypeStruct((t, nh * d), BF16),
        scratch_shapes=[pltpu.VMEM((hp, d, d), F32)] + [pltpu.VMEM((SUBLANES + lt, wide), F32)] * 3
                       + [pltpu.VMEM((LANES, lt), F32)],
        compiler_params=_cp("parallel", "arbitrary"),
        name="gated_deltanet",
    )(h1, h1, h1, h1, ba, conv_w, conv_w, conv_w, a_log_pad, dt_bias_pad, norm_w.reshape(1, d))


DMA_ISSUE_UNROLL = 8


def _moe_kernel(te_ref, nv_ref, rows_ref, src_ref, x_hbm, w1_ref, w3_ref, w2_ref, gate_ref, o_ref,
                rows_buf, xb_ref, sem, *, nj, tm, per_step):
    i = pl.program_id(0)
    j = pl.program_id(1)
    n_used = nv_ref[0]
    valid = i < n_used

    def start_rows(tile, first, count):
        for k in range(count):
            r = first + k
            tok = src_ref[tile * tm + jnp.minimum(r, tm - 1)]
            pltpu.make_async_copy(x_hbm.at[pl.ds(tok, 1), :], rows_buf.at[pl.ds(r, 1), :], sem).start()

    @pl.when(jnp.logical_and(i == 0, j == 0))
    def _():
        def body(s, _):
            start_rows(0, s * per_step, per_step)
            return 0
        lax.fori_loop(0, nj, body, 0)

    @pl.when(j == 0)
    def _():
        o_ref[...] = jnp.zeros_like(o_ref)

    @pl.when(jnp.logical_and(j == 0, i <= n_used))
    def _():
        pltpu.make_async_copy(x_hbm.at[pl.ds(0, nj * per_step), :], rows_buf, sem).wait()
        xb_ref[...] = rows_buf[0:tm, :].astype(BF16)

    half = tm // 2
    lower_only = rows_ref[i] <= half

    @pl.when(jnp.logical_and(valid, jnp.logical_not(lower_only)))
    def _():
        start_rows(i + 1, j * per_step, per_step)
        _swiglu_acc(xb_ref[...], w1_ref, w3_ref, w2_ref, o_ref)

    @pl.when(jnp.logical_and(valid, lower_only))
    def _():
        start_rows(i + 1, j * per_step, per_step)
        _swiglu_acc(xb_ref[0:half, :], w1_ref, w3_ref, w2_ref, o_ref.at[0:half, :])

    @pl.when(j == nj - 1)
    def _():
        o_ref[...] = o_ref[...] * gate_ref[...]


def _moe_ffn(x, src, gates, tile_expert, n_valid, tile_rows, w1, w3, w2, tm, tf):
    p = src.shape[0]
    d, dff = w1.shape[1], w1.shape[2]
    nj = dff // tf
    per_step = -(-tm // nj)
    while (nj * per_step) % SUBLANES:
        per_step += 1
    jsel = lambda i, j, nv: jnp.where(i < nv[0], j, nj - 1)
    return pl.pallas_call(
        functools.partial(_moe_kernel, nj=nj, tm=tm, per_step=per_step),
        grid_spec=pltpu.PrefetchScalarGridSpec(
            num_scalar_prefetch=4, grid=(p // tm, nj),
            in_specs=[pl.BlockSpec(memory_space=pl.ANY),
                      pl.BlockSpec((None, d, tf), lambda i, j, te, nv, rows, src: (te[i], 0, jsel(i, j, nv))),
                      pl.BlockSpec((None, d, tf), lambda i, j, te, nv, rows, src: (te[i], 0, jsel(i, j, nv))),
                      pl.BlockSpec((None, tf, d), lambda i, j, te, nv, rows, src: (te[i], jsel(i, j, nv), 0)),
                      pl.BlockSpec((tm, 1), lambda i, j, te, nv, rows, src: (i, 0))],
            out_specs=pl.BlockSpec((tm, d), lambda i, j, te, nv, rows, src: (i, 0)),
            scratch_shapes=[pltpu.VMEM((nj * per_step, d), F32), pltpu.VMEM((tm, d), BF16),
                            pltpu.SemaphoreType.DMA(())]),
        out_shape=jax.ShapeDtypeStruct((p, d), F32),
        compiler_params=_cp("arbitrary", "arbitrary"),
        name="moe_ffn",
    )(tile_expert, n_valid, tile_rows, src, x, w1, w3, w2, gates)


def _combine_kernel(pos_ref, ys_hbm, res_ref, g_ref, b_ref, o_ref, buf_ref, sem, *, tm, t):
    i = pl.program_id(0)

    def issue(step, slot):
        def body(r, _):
            for choice in range(2):
                p = pos_ref[choice * t + step * tm + r]
                pltpu.make_async_copy(ys_hbm.at[pl.ds(p, 1), :], buf_ref.at[slot, choice, pl.ds(r, 1), :],
                                      sem.at[slot, choice]).start()
            return 0
        lax.fori_loop(0, tm, body, 0, unroll=DMA_ISSUE_UNROLL)

    @pl.when(i == 0)
    def _():
        issue(0, 0)

    @pl.when(i + 1 < pl.num_programs(0))
    def _():
        issue(i + 1, (i + 1) % 2)

    slot = i % 2
    for choice in range(2):
        pltpu.make_async_copy(ys_hbm.at[pl.ds(0, tm), :], buf_ref.at[slot, choice],
                              sem.at[slot, choice]).wait()
    y = buf_ref[slot, 0] + buf_ref[slot, 1]
    o_ref[...] = _layer_norm(DEEPNORM_ALPHA * res_ref[...] + y, g_ref[...], b_ref[...])


def _moe_combine_ln(ys, pos, res, g, b, tm=256):
    t, d = res.shape
    return pl.pallas_call(
        functools.partial(_combine_kernel, tm=tm, t=t),
        grid_spec=pltpu.PrefetchScalarGridSpec(
            num_scalar_prefetch=1, grid=(t // tm,),
            in_specs=[pl.BlockSpec(memory_space=pl.ANY),
                      pl.BlockSpec((tm, d), lambda i, pos: (i, 0)),
                      pl.BlockSpec((1, d), lambda i, pos: (0, 0)),
                      pl.BlockSpec((1, d), lambda i, pos: (0, 0))],
            out_specs=pl.BlockSpec((tm, d), lambda i, pos: (i, 0)),
            scratch_shapes=[pltpu.VMEM((2, 2, tm, d), F32), pltpu.SemaphoreType.DMA((2, 2))]),
        out_shape=jax.ShapeDtypeStruct((t, d), F32),
        compiler_params=_cp("arbitrary"),
        name="moe_combine_ln",
    )(pos, ys, res, g.reshape(1, d), b.reshape(1, d))


def _moe_routing(route, tm):
    t = route.shape[0]
    e = N_EXPERTS
    idx = route[:, 0:2].astype(jnp.int32)
    wts = route[:, 2:4]
    flat_e = idx.T.reshape(-1)
    onehot = (flat_e[:, None] == jnp.arange(e, dtype=jnp.int32)[None, :]).astype(jnp.int32)
    rank = jnp.cumsum(onehot, axis=0) - onehot
    counts = jnp.sum(onehot, axis=0)
    tiles = (counts + tm - 1) // tm
    tile_end = jnp.cumsum(tiles)
    start = (tile_end - tiles) * tm
    pos = jnp.sum(onehot * (start[None, :] + rank), axis=1)
    n_slots = 2 * t + e * tm
    n_tiles = n_slots // tm
    owner = jnp.full((n_slots,), -1, jnp.int32).at[pos].set(jnp.arange(2 * t, dtype=jnp.int32))
    used = owner >= 0
    src = jnp.where(used, owner % t, jnp.arange(n_slots, dtype=jnp.int32) % t)
    gates = jnp.where(used, wts.T.reshape(-1)[jnp.maximum(owner, 0)], 0.0)
    n_valid = tile_end[-1]
    tile_ids = jnp.arange(n_tiles, dtype=jnp.int32)
    tile_expert = jnp.sum((tile_ids[:, None] >= tile_end[None, :]).astype(jnp.int32), axis=1)
    last_expert = jnp.sum((n_valid - 1 >= tile_end).astype(jnp.int32))
    tile_expert = jnp.where(tile_ids < n_valid, tile_expert, last_expert).astype(jnp.int32)
    tile_rows = jnp.clip((start + counts)[tile_expert] - tile_ids * tm, 0, tm)
    tile_rows = jnp.where(tile_ids < n_valid, tile_rows, 0).astype(jnp.int32)
    return (src, gates.reshape(n_slots, 1), pos.astype(jnp.int32), tile_expert,
            n_valid.reshape(1).astype(jnp.int32), tile_rows)


def _even_layer(x, batch, seq, w_in, w_out, ln_mix_g, ln_mix_b, w1, w3, w2, ln_ffn_g, ln_ffn_b):
    t, d = x.shape
    w_sb = N_HEADS_SB * HEAD_DIM
    w_in_b = w_in.astype(BF16)
    ha = _matmul(x, w_in_b[:, :3 * w_sb], BF16, *MM_TILE, "in_proj_sb")
    hb = _matmul(x, w_in_b[:, 3 * w_sb:], F32, *MM_TILE, "in_proj_dw")
    oa = _sb_attention(ha.reshape(batch, seq, -1), batch, seq).reshape(t, -1)
    ob = _dw_attention(hb.reshape(batch, seq, -1), batch, seq).reshape(t, -1)
    w_out_b = w_out.astype(BF16)
    x, xb = _proj_ln(oa, ob, w_out_b[:w_sb], w_out_b[w_sb:], x, ln_mix_g, ln_mix_b)
    return _ffn_ln(xb, x, w1.astype(BF16), w3.astype(BF16), w2.astype(BF16), ln_ffn_g, ln_ffn_b)


def _odd_layer(x, xb, batch, seq, w_in, lam_re, lam_im, log_dt, b_re, b_im, c_re, c_im, d_skip,
               glu_w, glu_b, conv_w, a_log, dt_bias, norm_w, w_out, ln_mix_g, ln_mix_b,
               router_w, w1, w3, w2, ln_ffn_g, ln_ffn_b):
    t, d = x.shape
    nh = N_HEADS_GDN
    wide = SSM_WIDTH + 4 * nh * GDN_D
    w_in_b = w_in[:, :wide].astype(BF16)
    u = _matmul(xb, w_in_b[:, :SSM_WIDTH], F32, *MM_TILE, "in_proj_ssm")
    h1 = _matmul(xb, w_in_b[:, SSM_WIDTH:], BF16, *MM_TILE, "in_proj_gdn")
    w_small = jnp.pad(w_in[:, wide:], ((0, 0), (0, LANES - 2 * nh)))
    ba = _matmul(x, _split_hi_lo(w_small), F32, 512, LANES, "in_proj_gates")

    y = _s5_scan(u, lam_re, lam_im, log_dt, b_re, b_im, c_re, c_im, batch, seq)
    oc = _s5_glu(y, u, d_skip, glu_w.astype(BF16), glu_b)

    pad_hi = LANES - 2 * nh
    a_log_pad = jnp.pad(a_log, (nh, pad_hi)).reshape(1, LANES)
    dt_bias_pad = jnp.pad(dt_bias, (nh, pad_hi)).reshape(1, LANES)
    od = _gated_deltanet(h1, ba, conv_w, a_log_pad, dt_bias_pad, norm_w, batch, seq)

    w_out_b = w_out.astype(BF16)
    rw = _split_hi_lo(jnp.pad(router_w, ((0, 0), (0, LANES - N_EXPERTS))))
    x, route = _proj_ln(oc, od, w_out_b[:SSM_WIDTH], w_out_b[SSM_WIDTH:], x, ln_mix_g, ln_mix_b,
                        router_w=rw)

    src, gates, pos, tile_expert, n_valid, tile_rows = _moe_routing(route, MOE_ROWS)
    ys = _moe_ffn(x, src, gates, tile_expert, n_valid, tile_rows, w1, w3, w2, MOE_ROWS, MOE_FF_CHUNK)
    return _moe_combine_ln(ys, pos, x, ln_ffn_g, ln_ffn_b)


def kernel(x, even_w_in, even_w_out, even_ln_mix_g, even_ln_mix_b, even_ffn_w1, even_ffn_w3, even_ffn_w2, even_ln_ffn_g, even_ln_ffn_b, odd_w_in, odd_ssm_lam_re, odd_ssm_lam_im, odd_ssm_log_dt, odd_ssm_b_re, odd_ssm_b_im, odd_ssm_c_re, odd_ssm_c_im, odd_ssm_d, odd_glu_w, odd_glu_b, odd_gdn_conv_w, odd_gdn_a_log, odd_gdn_dt_bias, odd_gdn_norm_w, odd_w_out, odd_ln_mix_g, odd_ln_mix_b, odd_router_w, odd_moe_w1, odd_moe_w3, odd_moe_w2, odd_ln_ffn_g, odd_ln_ffn_b):
    batch, seq, d = x.shape
    xf = x.reshape(batch * seq, d)
    xf, xb = _even_layer(xf, batch, seq, even_w_in[0], even_w_out[0],
                         even_ln_mix_g[0], even_ln_mix_b[0], even_ffn_w1[0], even_ffn_w3[0],
                         even_ffn_w2[0], even_ln_ffn_g[0], even_ln_ffn_b[0])
    out = _odd_layer(xf, xb, batch, seq, odd_w_in[0], odd_ssm_lam_re[0], odd_ssm_lam_im[0],
                     odd_ssm_log_dt[0], odd_ssm_b_re[0], odd_ssm_b_im[0], odd_ssm_c_re[0],
                     odd_ssm_c_im[0], odd_ssm_d[0], odd_glu_w[0], odd_glu_b[0], odd_gdn_conv_w[0],
                     odd_gdn_a_log[0], odd_gdn_dt_bias[0], odd_gdn_norm_w[0], odd_w_out[0],
                     odd_ln_mix_g[0], odd_ln_mix_b[0], odd_router_w[0], odd_moe_w1[0],
                     odd_moe_w3[0], odd_moe_w2[0], odd_ln_ffn_g[0], odd_ln_ffn_b[0])
    return out.reshape(batch, seq, d)
```

```python
import functools
import math

import jax
import jax.numpy as jnp
from jax import lax
from jax.experimental import pallas as pl
from jax.experimental.pallas import tpu as pltpu

F32 = jnp.float32
BF16 = jnp.bfloat16
HIGHEST = lax.Precision.HIGHEST

HEAD_DIM = 128
N_HEADS_SB = 8
N_HEADS_DW = 8
DW_PATTERNS = ((128, 1), (512, 4), (2048, 16))
DW_KEYS = 128
DW_TILE = 2048
SSM_WIDTH = 1024
SSM_GROUP = 16
SSM_GROUPS = 64
SSM_STATE = 64
SSM_CHUNK = 8
N_HEADS_GDN = 8
GDN_D = 128
GDN_CONV = 4
GDN_TILE = 256
N_EXPERTS = 8
DEPTH = 2
DEEPNORM_ALPHA = (2 * DEPTH) ** 0.25
LN_EPS = 1e-5
RMS_EPS = 1e-6

LANES = 128
SUBLANES = 8
VMEM_LIMIT = 56 * 1024 * 1024
MM_TILE = (1024, 1024)
MOE_ROWS = 1024
MOE_FF_CHUNK = 512
SB_SKIP_LOG = -104.0


def _cp(*sem):
    return pltpu.CompilerParams(dimension_semantics=sem, vmem_limit_bytes=VMEM_LIMIT)


def _layer_norm(y, g, b):
    mu = jnp.mean(y, axis=-1, keepdims=True)
    yc = y - mu
    var = jnp.mean(yc * yc, axis=-1, keepdims=True)
    return yc * lax.rsqrt(var + LN_EPS) * g + b


def _softplus(x):
    return jnp.maximum(x, 0.0) + jnp.log1p(jnp.exp(-jnp.abs(x)))


def _dot(a, b, precision=None):
    return jnp.dot(a, b, preferred_element_type=F32, precision=precision)


def _dot_nt(a, b, precision=None):
    return lax.dot_general(a, b, (((1,), (1,)), ((), ())),
                           preferred_element_type=F32, precision=precision)


def _dot_tn(a, b, precision=None):
    return lax.dot_general(a, b, (((0,), (0,)), ((), ())),
                           preferred_element_type=F32, precision=precision)


def _split_hi_lo(w):
    hi = w.astype(BF16)
    return jnp.stack([hi, (w - hi.astype(F32)).astype(BF16)])


def _dot_split(x, w_ref):
    xh = x.astype(BF16)
    xl = (x - xh.astype(F32)).astype(BF16)
    wh = w_ref[0]
    return _dot(xh, wh) + _dot(xl, wh) + _dot(xh, w_ref[1])


def _mm_kernel(a_ref, b_ref, o_ref):
    o_ref[...] = _dot(a_ref[...].astype(BF16), b_ref[...]).astype(o_ref.dtype)


def _mm_split_kernel(a_ref, b_ref, o_ref):
    o_ref[...] = _dot_split(a_ref[...], b_ref).astype(o_ref.dtype)


def _matmul(a, b, out_dtype, tm, tn, name):
    m, k = a.shape
    n = b.shape[-1]
    if b.ndim == 3:
        body, b_spec = _mm_split_kernel, pl.BlockSpec((2, k, tn), lambda i, j: (0, 0, j))
    else:
        body, b_spec = _mm_kernel, pl.BlockSpec((k, tn), lambda i, j: (0, j))
    return pl.pallas_call(
        body,
        grid=(m // tm, n // tn),
        in_specs=[pl.BlockSpec((tm, k), lambda i, j: (i, 0)), b_spec],
        out_specs=pl.BlockSpec((tm, tn), lambda i, j: (i, j)),
        out_shape=jax.ShapeDtypeStruct((m, n), out_dtype),
        compiler_params=_cp("parallel", "parallel"),
        name=name,
    )(a, b)


def _proj_ln_kernel(a0_ref, a1_ref, w0_ref, w1_ref, res_ref, g_ref, b_ref, *rest, with_router):
    if with_router:
        rw_ref, of_ref, rt_ref = rest
    else:
        of_ref, ob_ref = rest
    mix = _dot(a0_ref[...], w0_ref[...]) + _dot(a1_ref[...], w1_ref[...])
    xn = _layer_norm(DEEPNORM_ALPHA * res_ref[...] + mix, g_ref[...], b_ref[...])
    of_ref[...] = xn
    if with_router:
        logits = _dot_split(xn, rw_ref)
        lane = lax.broadcasted_iota(jnp.int32, logits.shape, 1).astype(F32)
        neg = jnp.float32(-jnp.inf)
        lg = jnp.where(lane < N_EXPERTS, logits, neg)
        m1 = jnp.max(lg, axis=1, keepdims=True)
        i1 = jnp.min(jnp.where(lg == m1, lane, float(LANES)), axis=1, keepdims=True)
        lg2 = jnp.where(lane == i1, neg, lg)
        m2 = jnp.max(lg2, axis=1, keepdims=True)
        i2 = jnp.min(jnp.where(lg2 == m2, lane, float(LANES)), axis=1, keepdims=True)
        e2 = jnp.exp(m2 - m1)
        p1 = 1.0 / (1.0 + e2)
        p2 = e2 * p1
        rt = jnp.where(lane == 0.0, i1, jnp.where(lane == 1.0, i2,
                       jnp.where(lane == 2.0, p1, jnp.where(lane == 3.0, p2, 0.0))))
        rt_ref[...] = rt
    else:
        ob_ref[...] = xn.astype(BF16)


def _proj_ln(a0, a1, w0, w1, res, g, b, router_w=None, tm=256):
    t, d = res.shape
    k0, k1 = a0.shape[1], a1.shape[1]
    with_router = router_w is not None
    row = lambda i: (i, 0)
    fixed = lambda i: (0, 0)
    in_specs = [pl.BlockSpec((tm, k0), row), pl.BlockSpec((tm, k1), row),
                pl.BlockSpec((k0, d), fixed), pl.BlockSpec((k1, d), fixed),
                pl.BlockSpec((tm, d), row), pl.BlockSpec((1, d), fixed), pl.BlockSpec((1, d), fixed)]
    args = [a0, a1, w0, w1, res, g.reshape(1, d), b.reshape(1, d)]
    if with_router:
        in_specs.append(pl.BlockSpec((2, d, LANES), lambda i: (0, 0, 0)))
        out_specs = [pl.BlockSpec((tm, d), row), pl.BlockSpec((tm, LANES), row)]
        out_shape = [jax.ShapeDtypeStruct((t, d), F32), jax.ShapeDtypeStruct((t, LANES), F32)]
        args.append(router_w)
    else:
        out_specs = [pl.BlockSpec((tm, d), row), pl.BlockSpec((tm, d), row)]
        out_shape = [jax.ShapeDtypeStruct((t, d), F32), jax.ShapeDtypeStruct((t, d), BF16)]
    return pl.pallas_call(
        functools.partial(_proj_ln_kernel, with_router=with_router),
        grid=(t // tm,), in_specs=in_specs, out_specs=out_specs, out_shape=out_shape,
        compiler_params=_cp("parallel"),
        name="proj_ln_router" if with_router else "proj_ln",
    )(*args)


def _swiglu_acc(x, w1_ref, w3_ref, w2_ref, acc_ref):
    h1 = _dot(x, w1_ref[...].astype(BF16))
    h3 = _dot(x, w3_ref[...].astype(BF16))
    act = (h1 * jax.nn.sigmoid(h1) * h3).astype(BF16)
    acc_ref[...] += _dot(act, w2_ref[...].astype(BF16))


def _ffn_kernel(x_ref, w1_ref, w3_ref, w2_ref, res_ref, g_ref, b_ref, of_ref, ob_ref, acc_ref, *, nj):
    j = pl.program_id(1)

    @pl.when(j == 0)
    def _():
        acc_ref[...] = jnp.zeros_like(acc_ref)

    _swiglu_acc(x_ref[...], w1_ref, w3_ref, w2_ref, acc_ref)

    @pl.when(j == nj - 1)
    def _():
        xn = _layer_norm(DEEPNORM_ALPHA * res_ref[...] + acc_ref[...], g_ref[...], b_ref[...])
        of_ref[...] = xn
        ob_ref[...] = xn.astype(BF16)


def _ffn_ln(xb, res, w1, w3, w2, g, b, tm=512, tf=512):
    t, d = res.shape
    dff = w1.shape[1]
    nj = dff // tf
    row = lambda i, j: (i, 0)
    fixed = lambda i, j: (0, 0)
    return pl.pallas_call(
        functools.partial(_ffn_kernel, nj=nj),
        grid=(t // tm, nj),
        in_specs=[pl.BlockSpec((tm, d), row),
                  pl.BlockSpec((d, tf), lambda i, j: (0, j)),
                  pl.BlockSpec((d, tf), lambda i, j: (0, j)),
                  pl.BlockSpec((tf, d), lambda i, j: (j, 0)),
                  pl.BlockSpec((tm, d), row), pl.BlockSpec((1, d), fixed), pl.BlockSpec((1, d), fixed)],
        out_specs=[pl.BlockSpec((tm, d), row), pl.BlockSpec((tm, d), row)],
        out_shape=[jax.ShapeDtypeStruct((t, d), F32), jax.ShapeDtypeStruct((t, d), BF16)],
        scratch_shapes=[pltpu.VMEM((tm, d), F32)],
        compiler_params=_cp("parallel", "arbitrary"),
        name="ffn_ln",
    )(xb, w1, w3, w2, res, g.reshape(1, d), b.reshape(1, d))


SB_QTILES = 4


def _sb_kernel(q_ref, k_ref, v_ref, o_ref, acc_ref, carry_ref, *, tq, scale):
    first_tile = pl.program_id(1) * SB_QTILES
    row = lax.broadcasted_iota(jnp.int32, (tq, tq), 0)
    col = lax.broadcasted_iota(jnp.int32, (tq, tq), 1)
    later_sum = (row > col).astype(BF16)
    past = col < row
    qs = [q_ref[t * tq:(t + 1) * tq, :] for t in range(SB_QTILES)]

    def blocks_terms(chains):
        starts = [pl.multiple_of(kb * tq, tq) for _, kb, _ in chains]
        z = [_dot_nt(q, k_ref[pl.ds(s, tq), :]) * scale for (q, _, _), s in zip(chains, starts)]
        lk = [-_softplus(zz) for zz in z]
        lk = [jnp.where(past, l, 0.0) if dg else l for l, (_, _, dg) in zip(lk, chains)]
        hi = [l.astype(BF16) for l in lk]
        lo = [(l - h.astype(F32)).astype(BF16) for l, h in zip(lk, hi)]
        later = [_dot(h, later_sum) + _dot(l, later_sum) for h, l in zip(hi, lo)]
        logw = [zz + l + lt for zz, l, lt in zip(z, lk, later)]
        tot = [jnp.sum(l, axis=1, keepdims=True) for l in lk]
        return logw, tot, [v_ref[pl.ds(s, tq), :] for s in starts]

    chains = []
    for t in range(SB_QTILES):
        chains += [(qs[t], first_tile + t, True), (qs[t], jnp.maximum(first_tile + t - 1, 0), False)]
    logw, tot, vals = blocks_terms(chains)
    for t in range(SB_QTILES):
        has_prev = first_tile + t >= 1
        w0 = jnp.where(past, jnp.exp(logw[2 * t]), 0.0)
        w1 = jnp.where(has_prev, jnp.exp(logw[2 * t + 1] + tot[2 * t]), 0.0)
        acc_ref[t] = _dot(w0.astype(BF16), vals[2 * t]) + _dot(w1.astype(BF16), vals[2 * t + 1])
        carry_ref[t] = tot[2 * t] + jnp.where(has_prev, tot[2 * t + 1], 0.0)

    for t in range(SB_QTILES):
        def cond(kb, t=t):
            return jnp.logical_and(kb >= 0, jnp.max(carry_ref[t]) > SB_SKIP_LOG)

        def body(kb, t=t):
            (lw,), (tt,), (v,) = blocks_terms([(qs[t], kb, False)])
            acc_ref[t] += _dot(jnp.exp(lw + carry_ref[t]).astype(BF16), v)
            carry_ref[t] += tt
            return kb - 1

        lax.while_loop(cond, body, first_tile + t - 2)
        o_ref[t * tq:(t + 1) * tq, :] = acc_ref[t].astype(o_ref.dtype)


def _sb_attention(h, batch, seq, tq=256):
    nh = N_HEADS_SB
    rows = SB_QTILES * tq
    return pl.pallas_call(
        functools.partial(_sb_kernel, tq=tq, scale=HEAD_DIM ** -0.5),
        grid=(batch * nh, seq // rows),
        in_specs=[pl.BlockSpec((None, rows, HEAD_DIM), lambda bh, i: (bh // nh, i, bh % nh)),
                  pl.BlockSpec((None, seq, HEAD_DIM), lambda bh, i: (bh // nh, 0, nh + bh % nh)),
                  pl.BlockSpec((None, seq, HEAD_DIM), lambda bh, i: (bh // nh, 0, 2 * nh + bh % nh))],
        out_specs=pl.BlockSpec((None, rows, HEAD_DIM), lambda bh, i: (bh // nh, i, bh % nh)),
        out_shape=jax.ShapeDtypeStruct((batch, seq, nh * HEAD_DIM), BF16),
        scratch_shapes=[pltpu.VMEM((SB_QTILES, tq, HEAD_DIM), F32), pltpu.VMEM((SB_QTILES, tq, 1), F32)],
        compiler_params=_cp("parallel", "arbitrary"),
        name="sb_attention",
    )(h, h, h)


DW_UNROLL = 8


def _dw_kernel(q_ref, k_ref, v_ref, o_ref, m_ref, l_ref, acc_ref, kp_ref, vp_ref, *, scale):
    nk = DW_KEYS
    t0 = pl.program_id(1) * DW_TILE

    @pl.when(pl.program_id(1) == 0)
    def _():
        kp_ref[...] = jnp.zeros_like(kp_ref)
        vp_ref[...] = jnp.zeros_like(vp_ref)

    ii = lax.broadcasted_iota(jnp.int32, (nk, 2 * nk), 0)
    jj = lax.broadcasted_iota(jnp.int32, (nk, 2 * nk), 1)
    band = jnp.logical_and(jj >= ii, jj <= ii + nk)
    band_prev = jnp.logical_and(band, jj < nk)
    band_own = jnp.logical_and(band, jj >= nk)
    neg = jnp.float32(-jnp.inf)

    for p_idx, (window, dil) in enumerate(DW_PATTERNS):
        assert window // dil == nk
        span = nk * dil
        n_steps = DW_TILE // nk

        def group(g, _, dil=dil, span=span, p_idx=p_idx):
            subs = range(DW_UNROLL)
            blk = [(g * DW_UNROLL + u) // dil for u in subs]
            res = [(g * DW_UNROLL + u) % dil for u in subs]
            off = [b * span + r for b, r in zip(blk, res)]
            rows = [pl.ds(o, nk, stride=dil) for o in off]
            cur = [pl.ds(t0 + o, nk, stride=dil) for o in off]
            has_prev = [(t0 + b * span) > 0 for b in blk]
            prev = [pl.ds(jnp.maximum(t0 + b * span - span, 0) + r, nk, stride=dil) for b, r in zip(blk, res)]
            qs = [q_ref[rw, :].astype(BF16) for rw in rows]
            k_cur = [k_ref[cr, :].astype(BF16) for cr in cur]
            v_cur = [v_ref[cr, :].astype(BF16) for cr in cur]
            if span == DW_TILE:
                k_prev = [kp_ref[r] for r in res]
                v_prev = [vp_ref[r] for r in res]
                for r, kc, vc in zip(res, k_cur, v_cur):
                    kp_ref[r] = kc
                    vp_ref[r] = vc
            else:
                k_prev = [k_ref[pv, :].astype(BF16) for pv in prev]
                v_prev = [v_ref[pv, :].astype(BF16) for pv in prev]
            keys = [jnp.concatenate([kp, kc], axis=0) for kp, kc in zip(k_prev, k_cur)]
            sc = [_dot_nt(qq, kk) * scale for qq, kk in zip(qs, keys)]
            sc = [jnp.where(jnp.logical_or(band_own, jnp.logical_and(band_prev, hp)), s, neg)
                  for s, hp in zip(sc, has_prev)]
            m_new = [jnp.max(s, axis=1, keepdims=True) for s in sc]
            p = [jnp.exp(s - m) for s, m in zip(sc, m_new)]
            l_new = [jnp.sum(pp, axis=1, keepdims=True) for pp in p]
            vals = [jnp.concatenate([vp, vc], axis=0) for vp, vc in zip(v_prev, v_cur)]
            num = [_dot(pp.astype(BF16), vv) for pp, vv in zip(p, vals)]
            for rw, m, l, nm in zip(rows, m_new, l_new, num):
                m_ref[p_idx, rw, :] = jnp.broadcast_to(m, (nk, HEAD_DIM))
                l_ref[p_idx, rw, :] = jnp.broadcast_to(l, (nk, HEAD_DIM))
                acc_ref[p_idx, rw, :] = nm
            return 0

        lax.fori_loop(0, n_steps // DW_UNROLL, group, 0)

    m_all = m_ref[...]
    m_tot = jnp.max(m_all, axis=0)
    w = jnp.exp(m_all - m_tot[None])
    den = jnp.sum(w * l_ref[...], axis=0)
    num = jnp.sum(w * acc_ref[...], axis=0)
    o_ref[...] = (num / den).astype(o_ref.dtype)


def _dw_attention(h, batch, seq):
    nh = N_HEADS_DW
    return pl.pallas_call(
        functools.partial(_dw_kernel, scale=HEAD_DIM ** -0.5),
        grid=(batch * nh, seq // DW_TILE),
        in_specs=[pl.BlockSpec((None, DW_TILE, HEAD_DIM), lambda bh, i: (bh // nh, i, bh % nh)),
                  pl.BlockSpec((None, seq, HEAD_DIM), lambda bh, i: (bh // nh, 0, nh + bh % nh)),
                  pl.BlockSpec((None, seq, HEAD_DIM), lambda bh, i: (bh // nh, 0, 2 * nh + bh % nh))],
        out_specs=pl.BlockSpec((None, DW_TILE, HEAD_DIM), lambda bh, i: (bh // nh, i, bh % nh)),
        out_shape=jax.ShapeDtypeStruct((batch, seq, nh * HEAD_DIM), BF16),
        scratch_shapes=[pltpu.VMEM((len(DW_PATTERNS), DW_TILE, HEAD_DIM), F32)] * 3
                       + [pltpu.VMEM((DW_TILE // DW_KEYS, DW_KEYS, HEAD_DIM), BF16)] * 2,
        compiler_params=_cp("parallel", "arbitrary"),
        name="dw_attention",
    )(h, h, h)


def _s5_params(lam_re, lam_im, log_dt, b_re, b_im, c_re, c_im, n_scan):
    L = SSM_CHUNK
    gpl = LANES // SSM_GROUP
    nlb = SSM_GROUPS // gpl
    dt = jnp.exp(log_dt)[:, None]
    mag_log = lam_re * dt
    ang = lam_im * dt

    def power(n):
        n = jnp.asarray(n, F32)[..., None, None]
        mag = jnp.exp(mag_log * n)
        return mag * jnp.cos(ang * n), mag * jnp.sin(ang * n)

    lr, li = power(jnp.ones(()))
    den = lam_re * lam_re + lam_im * lam_im
    cr = ((lr - 1.0) * lam_re + li * lam_im) / den
    ci = (li * lam_re - (lr - 1.0) * lam_im) / den
    bbr = cr[..., None] * b_re - ci[..., None] * b_im
    bbi = cr[..., None] * b_im + ci[..., None] * b_re

    pr, pi = power(jnp.arange(L + 1, dtype=F32))
    mr = pr[:L, :, :, None] * bbr - pi[:L, :, :, None] * bbi
    mi = pr[:L, :, :, None] * bbi + pi[:L, :, :, None] * bbr
    kk = (jnp.einsum('ghp,tgpk->tghk', c_re, mr, precision=HIGHEST)
          - jnp.einsum('ghp,tgpk->tghk', c_im, mi, precision=HIGHEST))
    jj = jnp.arange(L)[:, None]
    ii = jnp.arange(L)[None, :]
    lag = ii - jj
    toe = jnp.where((lag >= 0)[:, :, None, None, None], kk[jnp.clip(lag, 0, L - 1)], 0.0)
    toe = toe.reshape(L, L, nlb, gpl, SSM_GROUP, SSM_GROUP)
    c_toe = toe.transpose(2, 0, 3, 5, 1, 4).reshape(nlb, L * LANES, L * SSM_GROUP)
    qr = pr[L - 1 - jnp.arange(L)]
    qi = pi[L - 1 - jnp.arange(L)]
    inr = (qr[..., None] * bbr - qi[..., None] * bbi).reshape(L, nlb, gpl, SSM_STATE, SSM_GROUP)
    ini = (qr[..., None] * bbi + qi[..., None] * bbr).reshape(L, nlb, gpl, SSM_STATE, SSM_GROUP)
    c_in = jnp.concatenate(
        [part.transpose(1, 0, 2, 4, 3).reshape(nlb, L * LANES, SSM_STATE) for part in (inr, ini)], axis=2)
    orr = c_re[None] * pr[1:, :, None, :] - c_im[None] * pi[1:, :, None, :]
    oii = c_re[None] * pi[1:, :, None, :] + c_im[None] * pr[1:, :, None, :]
    half = gpl * SSM_STATE
    c_out = jnp.concatenate(
        [part.reshape(L, nlb, gpl, SSM_GROUP, SSM_STATE).transpose(1, 2, 4, 0, 3)
         .reshape(nlb, half, L * SSM_GROUP) for part in (orr, -oii)], axis=1)
    ar, ai = power(float(L) * (2.0 ** jnp.arange(n_scan, dtype=F32)))
    ar = ar.reshape(n_scan, nlb, half).transpose(1, 0, 2)
    ai = ai.reshape(n_scan, nlb, half).transpose(1, 0, 2)
    return c_toe.astype(BF16), c_in.astype(BF16), c_out.astype(BF16), ar, ai


def _spread_matrix(inner, reps, period):
    rows = jnp.arange(inner * period)
    cols = jnp.arange(inner * reps * period)
    same_a = (rows[:, None] // period) == (cols[None, :] // (reps * period))
    same_c = (rows[:, None] % period) == (cols[None, :] % period)
    return jnp.logical_and(same_a, same_c).astype(BF16)


def _spread_groups(compact, spread, row_period, col_period):
    gpl = LANES // SSM_GROUP
    full = _dot(compact, spread)
    rg = (lax.broadcasted_iota(jnp.int32, full.shape, 0) // row_period) % gpl
    cg = (lax.broadcasted_iota(jnp.int32, full.shape, 1) // col_period) % gpl
    return jnp.where(rg == cg, full, 0.0).astype(BF16)


def _s5_kernel(x_ref, ct_ref, ci_ref, co_ref, et_ref, ei_ref, ar_ref, ai_ref, y_ref,
               wt_ref, wi_ref, wo_ref, sr_ref, si_ref, *, n_scan, m):
    L = SSM_CHUNK

    @pl.when(jnp.logical_and(pl.program_id(1) == 0, pl.program_id(2) == 0))
    def _():
        wt_ref[...] = _spread_groups(ct_ref[...], et_ref[...], SSM_GROUP, SSM_GROUP)
        wi_ref[...] = _spread_groups(ci_ref[...], ei_ref[...], SSM_GROUP, SSM_STATE)
        wo_ref[...] = _spread_groups(co_ref[...], et_ref[...], SSM_STATE, SSM_GROUP)

    @pl.when(pl.program_id(2) == 0)
    def _():
        sr_ref[...] = jnp.zeros_like(sr_ref)
        si_ref[...] = jnp.zeros_like(si_ref)

    xc = jnp.concatenate([x_ref[pl.ds(j, m, stride=L), :].astype(BF16) for j in range(L)], axis=1)
    yc = _dot(xc, wt_ref[...])
    z = _dot(xc, wi_ref[...])
    half = z.shape[1] // 2
    zr, zi = z[:, :half], z[:, half:]
    row = lax.broadcasted_iota(jnp.int32, (m, half), 0)
    pr, pi = sr_ref[0:1, :], si_ref[0:1, :]
    a1r, a1i = ar_ref[0:1, :], ai_ref[0:1, :]
    first = row == 0
    zr, zi = (zr + jnp.where(first, a1r * pr - a1i * pi, 0.0),
              zi + jnp.where(first, a1r * pi + a1i * pr, 0.0))
    for k in range(n_scan):
        s = 1 << k
        ar = ar_ref[k:k + 1, :]
        ai = ai_ref[k:k + 1, :]
        keep = row >= s
        tr = jnp.where(keep, pltpu.roll(zr, s, 0), 0.0)
        ti = jnp.where(keep, pltpu.roll(zi, s, 0), 0.0)
        zr, zi = zr + ar * tr - ai * ti, zi + ar * ti + ai * tr
    keep = row >= 1
    s_in = jnp.concatenate([jnp.where(keep, pltpu.roll(zr, 1, 0), pr),
                            jnp.where(keep, pltpu.roll(zi, 1, 0), pi)], axis=1)
    sr_ref[...] = jnp.broadcast_to(zr[m - 1:m, :], sr_ref.shape)
    si_ref[...] = jnp.broadcast_to(zi[m - 1:m, :], si_ref.shape)
    yc = yc + _dot(s_in.astype(BF16), wo_ref[...])
    for i in range(L):
        y_ref[pl.ds(i, m, stride=L), :] = yc[:, i * LANES:(i + 1) * LANES]


def _s5_scan(h1, lam_re, lam_im, log_dt, b_re, b_im, c_re, c_im, batch, seq, rows=2048):
    L = SSM_CHUNK
    rows = min(rows, seq)
    m = rows // L
    nt = seq // rows
    n_scan = max(1, math.ceil(math.log2(m)))
    c_toe, c_in, c_out, ar, ai = _s5_params(lam_re, lam_im, log_dt, b_re, b_im, c_re, c_im, n_scan)
    nlb, half = ar.shape[0], ar.shape[2]
    gpl = LANES // SSM_GROUP
    e_toe = _spread_matrix(L, gpl, SSM_GROUP)
    e_in = _spread_matrix(2, gpl, SSM_STATE)
    per_lb = lambda *blk: pl.BlockSpec((None,) + blk, lambda l, b, i: (l, 0, 0))
    whole = lambda a: pl.BlockSpec(a.shape, lambda l, b, i: (0, 0))
    tile = pl.BlockSpec((rows, LANES), lambda l, b, i: (b * nt + i, l))
    wide = L * LANES
    return pl.pallas_call(
        functools.partial(_s5_kernel, n_scan=n_scan, m=m),
        grid=(nlb, batch, nt),
        in_specs=[tile, per_lb(wide, c_toe.shape[2]), per_lb(wide, c_in.shape[2]),
                  per_lb(2 * half, c_out.shape[2]), whole(e_toe), whole(e_in),
                  per_lb(n_scan, half), per_lb(n_scan, half)],
        out_specs=tile,
        out_shape=jax.ShapeDtypeStruct((batch * seq, SSM_WIDTH), F32),
        scratch_shapes=[pltpu.VMEM((wide, wide), BF16), pltpu.VMEM((wide, 2 * half), BF16),
                        pltpu.VMEM((2 * half, wide), BF16)] + [pltpu.VMEM((SUBLANES, half), F32)] * 2,
        compiler_params=_cp("parallel", "arbitrary", "arbitrary"),
        name="s5_scan",
    )(h1, c_toe, c_in, c_out, e_toe, e_in, ar, ai)


def _s5_glu_kernel(y_ref, u_ref, d_ref, w_ref, b_ref, o_ref):
    y = y_ref[...] + d_ref[...] * u_ref[...]
    c = math.sqrt(2.0 / math.pi)
    z = 0.5 * y * (1.0 + jnp.tanh(c * (y + 0.044715 * (y * y * y))))
    gate = jax.nn.sigmoid(_dot(z.astype(BF16), w_ref[...]) + b_ref[...])
    o_ref[...] = (z * gate).astype(o_ref.dtype)


def _s5_glu(y, h1, d_skip, glu_w, glu_b, tm=512):
    t, w = y.shape
    row = lambda i: (i, 0)
    fixed = lambda i: (0, 0)
    return pl.pallas_call(
        _s5_glu_kernel,
        grid=(t // tm,),
        in_specs=[pl.BlockSpec((tm, w), row), pl.BlockSpec((tm, w), row),
                  pl.BlockSpec((1, w), fixed), pl.BlockSpec((w, w), fixed), pl.BlockSpec((1, w), fixed)],
        out_specs=pl.BlockSpec((tm, w), row),
        out_shape=jax.ShapeDtypeStruct((t, w), BF16),
        compiler_params=_cp("parallel"),
        name="s5_glu",
    )(y, h1, d_skip.reshape(1, w), glu_w, glu_b.reshape(1, w))


def _bdot(a, b):
    return _dot(a.astype(BF16), b.astype(BF16))


def _bdot_nt(a, b):
    return _dot_nt(a.astype(BF16), b.astype(BF16))


def _gdn_kernel(q_ref, k_ref, v_ref, gate_ref, ba_ref, wq_ref, wk_ref, wv_ref, alog_ref, dtb_ref,
                nw_ref, o_ref, state_ref, hq_ref, hk_ref, hv_ref, gct_ref, *, lt, nh, hp):
    d = GDN_D
    c = lt
    head0 = (pl.program_id(0) % (nh // hp)) * hp

    @pl.when(pl.program_id(1) == 0)
    def _():
        state_ref[...] = jnp.zeros_like(state_ref)
        for halo_ref in (hq_ref, hk_ref, hv_ref):
            halo_ref[0:SUBLANES, :] = jnp.zeros((SUBLANES, hp * d), F32)

    def conv_silu(x_ref, w_ref, xe_ref):
        w = w_ref[...]
        xe_ref[SUBLANES:, :] = x_ref[...].astype(F32)
        y = xe_ref[SUBLANES:, :] * w[GDN_CONV - 1:GDN_CONV, :]
        for back in range(1, GDN_CONV):
            y = y + xe_ref[pl.ds(SUBLANES - back, lt), :] * w[GDN_CONV - 1 - back:GDN_CONV - back, :]
        xe_ref[0:SUBLANES, :] = xe_ref[lt:, :]
        return y * jax.nn.sigmoid(y)

    def l2norm(t):
        return t * lax.rsqrt(jnp.sum(t * t, axis=-1, keepdims=True) + RMS_EPS)

    q_cs = conv_silu(q_ref, wq_ref, hq_ref)
    k_cs = conv_silu(k_ref, wk_ref, hk_ref)
    v_cs = conv_silu(v_ref, wv_ref, hv_ref)

    ba = ba_ref[...]
    lane = lax.broadcasted_iota(jnp.int32, ba.shape, 1)
    sig_ba = jax.nn.sigmoid(ba)
    g_full = -jnp.exp(alog_ref[...]) * _softplus(ba + dtb_ref[...])
    gc_full = g_full
    rows_i = lax.broadcasted_iota(jnp.int32, gc_full.shape, 0)
    step = 1
    while step < c:
        gc_full = gc_full + jnp.where(rows_i >= step, pltpu.roll(gc_full, step, 0), 0.0)
        step *= 2
    gct_ref[...] = gc_full.T

    ri = lax.broadcasted_iota(jnp.int32, (c, c), 0)
    ci = lax.broadcasted_iota(jnp.int32, (c, c), 1)
    tri = ri >= ci
    strict = ri > ci
    eye = (ri == ci).astype(F32)
    in16 = (ri // 16) == (ci // 16)
    merges = []
    width = 16
    while width < c:
        inner = (ri // width) == (ci // width)
        outer = (ri // (2 * width)) == (ci // (2 * width))
        merges.append(jnp.logical_and(outer, jnp.logical_not(inner)))
        width *= 2

    heads = range(hp)
    cols = [slice(hh * d, (hh + 1) * d) for hh in heads]
    q = [l2norm(q_cs[:, cols[hh]]) * (d ** -0.5) for hh in heads]
    k = [l2norm(k_cs[:, cols[hh]]) for hh in heads]
    v = [v_cs[:, cols[hh]] for hh in heads]
    beta = [jnp.sum(jnp.where(lane == head0 + hh, sig_ba, 0.0), axis=1, keepdims=True) for hh in heads]
    gc = [jnp.sum(jnp.where(lane == nh + head0 + hh, gc_full, 0.0), axis=1, keepdims=True)
          for hh in heads]
    gc_row = [gct_ref[pl.ds(nh + head0 + hh, 1), :] for hh in heads]
    gc_last = [gc_row[hh][:, c - 1:c] for hh in heads]
    decay = [jnp.exp(jnp.where(tri, gc[hh] - gc_row[hh], -jnp.inf)) for hh in heads]
    kb = [k[hh] * beta[hh] for hh in heads]
    kq_kt = [_bdot_nt(jnp.concatenate([kb[hh], q[hh]], axis=0), k[hh]) for hh in heads]
    lower = [jnp.where(strict, kq_kt[hh][:c] * decay[hh], 0.0) for hh in heads]
    pw = [jnp.where(in16, -lower[hh], 0.0) for hh in heads]
    inv = [eye + pw[hh] for hh in heads]
    for _ in range(3):
        pw = [_bdot(pw[hh], pw[hh]) for hh in heads]
        inv = [inv[hh] + _bdot(inv[hh], pw[hh]) for hh in heads]
    for off in merges:
        part = [_bdot(inv[hh], jnp.where(off, lower[hh], 0.0)) for hh in heads]
        inv = [inv[hh] - _bdot(part[hh], inv[hh]) for hh in heads]
    e_gc = [jnp.exp(gc[hh]) for hh in heads]
    uw = [_bdot(inv[hh], jnp.concatenate([v[hh] * beta[hh], kb[hh] * e_gc[hh]], axis=1)) for hh in heads]
    state = [state_ref[hh] for hh in heads]
    ws_qs = [_bdot(jnp.concatenate([uw[hh][:, d:], q[hh] * e_gc[hh]], axis=0), state[hh]) for hh in heads]
    v_new = [uw[hh][:, :d] - ws_qs[hh][:c] for hh in heads]
    o = [ws_qs[hh][c:] + _bdot(kq_kt[hh][c:] * decay[hh], v_new[hh]) for hh in heads]
    for hh in heads:
        k_dec = k[hh] * jnp.exp(gc_last[hh] - gc[hh])
        state_ref[hh] = (state[hh] * jnp.exp(gc_last[hh])
                         + _dot_tn(k_dec.astype(BF16), v_new[hh].astype(BF16)))
        on = o[hh] * lax.rsqrt(jnp.mean(o[hh] * o[hh], axis=-1, keepdims=True) + RMS_EPS) * nw_ref[...]
        gt = gate_ref[:, cols[hh]].astype(F32)
        o_ref[:, cols[hh]] = (on * (gt * jax.nn.sigmoid(gt))).astype(o_ref.dtype)


def _gated_deltanet(h1, ba, conv_w, a_log_pad, dt_bias_pad, norm_w, batch, seq, lt=GDN_TILE, hp=8):
    nh = N_HEADS_GDN
    d = GDN_D
    t = batch * seq
    nt = seq // lt
    ng = nh // hp
    wide = hp * d
    base = 0
    rows = lambda off: (lambda bg, i: ((bg // ng) * nt + i, off + bg % ng))
    cw = lambda off: (lambda bg, i: (0, off + bg % ng))
    fixed = lambda bg, i: (0, 0)
    return pl.pallas_call(
        functools.partial(_gdn_kernel, lt=lt, nh=nh, hp=hp),
        grid=(batch * ng, nt),
        in_specs=[pl.BlockSpec((lt, wide), rows(base)), pl.BlockSpec((lt, wide), rows(base + ng)),
                  pl.BlockSpec((lt, wide), rows(base + 2 * ng)), pl.BlockSpec((lt, wide), rows(base + 3 * ng)),
                  pl.BlockSpec((lt, LANES), lambda bg, i: ((bg // ng) * nt + i, 0)),
                  pl.BlockSpec((GDN_CONV, wide), cw(0)), pl.BlockSpec((GDN_CONV, wide), cw(ng)),
                  pl.BlockSpec((GDN_CONV, wide), cw(2 * ng)),
                  pl.BlockSpec((1, LANES), fixed), pl.BlockSpec((1, LANES), fixed),
                  pl.BlockSpec((1, d), fixed)],
        out_specs=pl.BlockSpec((lt, wide), rows(0)),
        out_shape=jax.ShapeDtypeStruct((t, nh * d), BF16),
        scratch_shapes=[pltpu.VMEM((hp, d, d), F32)] + [pltpu.VMEM((SUBLANES + lt, wide), F32)] * 3
                       + [pltpu.VMEM((LANES, lt), F32)],
        compiler_params=_cp("parallel", "arbitrary"),
        name="gated_deltanet",
    )(h1, h1, h1, h1, ba, conv_w, conv_w, conv_w, a_log_pad, dt_bias_pad, norm_w.reshape(1, d))


DMA_ISSUE_UNROLL = 8


def _moe_kernel(te_ref, nv_ref, rows_ref, src_ref, x_hbm, w1_ref, w3_ref, w2_ref, gate_ref, o_ref,
                rows_buf, xb_ref, sem, *, nj, tm, per_step):
    i = pl.program_id(0)
    j = pl.program_id(1)
    n_used = nv_ref[0]
    valid = i < n_used

    def start_rows(tile, first, count):
        for k in range(count):
            r = first + k
            tok = src_ref[tile * tm + jnp.minimum(r, tm - 1)]
            pltpu.make_async_copy(x_hbm.at[pl.ds(tok, 1), :], rows_buf.at[pl.ds(r, 1), :], sem).start()

    @pl.when(jnp.logical_and(i == 0, j == 0))
    def _():
        def body(s, _):
            start_rows(0, s * per_step, per_step)
            return 0
        lax.fori_loop(0, nj, body, 0)

    @pl.when(j == 0)
    def _():
        o_ref[...] = jnp.zeros_like(o_ref)

    @pl.when(jnp.logical_and(j == 0, i <= n_used))
    def _():
        pltpu.make_async_copy(x_hbm.at[pl.ds(0, nj * per_step), :], rows_buf, sem).wait()
        xb_ref[...] = rows_buf[0:tm, :].astype(BF16)

    half = tm // 2
    lower_only = rows_ref[i] <= half

    @pl.when(jnp.logical_and(valid, jnp.logical_not(lower_only)))
    def _():
        start_rows(i + 1, j * per_step, per_step)
        _swiglu_acc(xb_ref[...], w1_ref, w3_ref, w2_ref, o_ref)

    @pl.when(jnp.logical_and(valid, lower_only))
    def _():
        start_rows(i + 1, j * per_step, per_step)
        _swiglu_acc(xb_ref[0:half, :], w1_ref, w3_ref, w2_ref, o_ref.at[0:half, :])

    @pl.when(j == nj - 1)
    def _():
        o_ref[...] = o_ref[...] * gate_ref[...]


def _moe_ffn(x, src, gates, tile_expert, n_valid, tile_rows, w1, w3, w2, tm, tf):
    p = src.shape[0]
    d, dff = w1.shape[1], w1.shape[2]
    nj = dff // tf
    per_step = -(-tm // nj)
    while (nj * per_step) % SUBLANES:
        per_step += 1
    jsel = lambda i, j, nv: jnp.where(i < nv[0], j, nj - 1)
    return pl.pallas_call(
        functools.partial(_moe_kernel, nj=nj, tm=tm, per_step=per_step),
        grid_spec=pltpu.PrefetchScalarGridSpec(
            num_scalar_prefetch=4, grid=(p // tm, nj),
            in_specs=[pl.BlockSpec(memory_space=pl.ANY),
                      pl.BlockSpec((None, d, tf), lambda i, j, te, nv, rows, src: (te[i], 0, jsel(i, j, nv))),
                      pl.BlockSpec((None, d, tf), lambda i, j, te, nv, rows, src: (te[i], 0, jsel(i, j, nv))),
                      pl.BlockSpec((None, tf, d), lambda i, j, te, nv, rows, src: (te[i], jsel(i, j, nv), 0)),
                      pl.BlockSpec((tm, 1), lambda i, j, te, nv, rows, src: (i, 0))],
            out_specs=pl.BlockSpec((tm, d), lambda i, j, te, nv, rows, src: (i, 0),
                                   pipeline_mode=pl.Buffered(1)),
            scratch_shapes=[pltpu.VMEM((nj * per_step, d), F32), pltpu.VMEM((tm, d), BF16),
                            pltpu.SemaphoreType.DMA(())]),
        out_shape=jax.ShapeDtypeStruct((p, d), F32),
        compiler_params=_cp("arbitrary", "arbitrary"),
        name="moe_ffn",
    )(tile_expert, n_valid, tile_rows, src, x, w1, w3, w2, gates)


def _combine_kernel(pos_ref, ys_hbm, res_ref, g_ref, b_ref, o_ref, buf_ref, sem, *, tm, t):
    i = pl.program_id(0)

    def issue(step, slot):
        def body(r, _):
            for choice in range(2):
                p = pos_ref[choice * t + step * tm + r]
                pltpu.make_async_copy(ys_hbm.at[pl.ds(p, 1), :], buf_ref.at[slot, choice, pl.ds(r, 1), :],
                                      sem.at[slot, choice]).start()
            return 0
        lax.fori_loop(0, tm, body, 0, unroll=DMA_ISSUE_UNROLL)

    @pl.when(i == 0)
    def _():
        issue(0, 0)

    @pl.when(i + 1 < pl.num_programs(0))
    def _():
        issue(i + 1, (i + 1) % 2)

    slot = i % 2
    for choice in range(2):
        pltpu.make_async_copy(ys_hbm.at[pl.ds(0, tm), :], buf_ref.at[slot, choice],
                              sem.at[slot, choice]).wait()
    y = buf_ref[slot, 0] + buf_ref[slot, 1]
    o_ref[...] = _layer_norm(DEEPNORM_ALPHA * res_ref[...] + y, g_ref[...], b_ref[...])


def _moe_combine_ln(ys, pos, res, g, b, tm=256):
    t, d = res.shape
    return pl.pallas_call(
        functools.partial(_combine_kernel, tm=tm, t=t),
        grid_spec=pltpu.PrefetchScalarGridSpec(
            num_scalar_prefetch=1, grid=(t // tm,),
            in_specs=[pl.BlockSpec(memory_space=pl.ANY),
                      pl.BlockSpec((tm, d), lambda i, pos: (i, 0)),
                      pl.BlockSpec((1, d), lambda i, pos: (0, 0)),
                      pl.BlockSpec((1, d), lambda i, pos: (0, 0))],
            out_specs=pl.BlockSpec((tm, d), lambda i, pos: (i, 0)),
            scratch_shapes=[pltpu.VMEM((2, 2, tm, d), F32), pltpu.SemaphoreType.DMA((2, 2))]),
        out_shape=jax.ShapeDtypeStruct((t, d), F32),
        compiler_params=_cp("arbitrary"),
        name="moe_combine_ln",
    )(pos, ys, res, g.reshape(1, d), b.reshape(1, d))


def _moe_routing(route, tm):
    t = route.shape[0]
    e = N_EXPERTS
    idx = route[:, 0:2].astype(jnp.int32)
    wts = route[:, 2:4]
    flat_e = idx.T.reshape(-1)
    onehot = (flat_e[:, None] == jnp.arange(e, dtype=jnp.int32)[None, :]).astype(jnp.int32)
    rank = jnp.cumsum(onehot, axis=0) - onehot
    counts = jnp.sum(onehot, axis=0)
    tiles = (counts + tm - 1) // tm
    tile_end = jnp.cumsum(tiles)
    start = (tile_end - tiles) * tm
    pos = jnp.sum(onehot * (start[None, :] + rank), axis=1)
    n_slots = 2 * t + e * tm
    n_tiles = n_slots // tm
    owner = jnp.full((n_slots,), -1, jnp.int32).at[pos].set(jnp.arange(2 * t, dtype=jnp.int32))
    used = owner >= 0
    src = jnp.where(used, owner % t, jnp.arange(n_slots, dtype=jnp.int32) % t)
    gates = jnp.where(used, wts.T.reshape(-1)[jnp.maximum(owner, 0)], 0.0)
    n_valid = tile_end[-1]
    tile_ids = jnp.arange(n_tiles, dtype=jnp.int32)
    tile_expert = jnp.sum((tile_ids[:, None] >= tile_end[None, :]).astype(jnp.int32), axis=1)
    last_expert = jnp.sum((n_valid - 1 >= tile_end).astype(jnp.int32))
    tile_expert = jnp.where(tile_ids < n_valid, tile_expert, last_expert).astype(jnp.int32)
    tile_rows = jnp.clip((start + counts)[tile_expert] - tile_ids * tm, 0, tm)
    tile_rows = jnp.where(tile_ids < n_valid, tile_rows, 0).astype(jnp.int32)
    return (src, gates.reshape(n_slots, 1), pos.astype(jnp.int32), tile_expert,
            n_valid.reshape(1).astype(jnp.int32), tile_rows)


def _even_layer(x, batch, seq, w_in, w_out, ln_mix_g, ln_mix_b, w1, w3, w2, ln_ffn_g, ln_ffn_b):
    t, d = x.shape
    w_sb = N_HEADS_SB * HEAD_DIM
    w_in_b = w_in.astype(BF16)
    ha = _matmul(x, w_in_b[:, :3 * w_sb], BF16, *MM_TILE, "in_proj_sb")
    hb = _matmul(x, w_in_b[:, 3 * w_sb:], F32, *MM_TILE, "in_proj_dw")
    oa = _sb_attention(ha.reshape(batch, seq, -1), batch, seq).reshape(t, -1)
    ob = _dw_attention(hb.reshape(batch, seq, -1), batch, seq).reshape(t, -1)
    w_out_b = w_out.astype(BF16)
    x, xb = _proj_ln(oa, ob, w_out_b[:w_sb], w_out_b[w_sb:], x, ln_mix_g, ln_mix_b)
    return _ffn_ln(xb, x, w1.astype(BF16), w3.astype(BF16), w2.astype(BF16), ln_ffn_g, ln_ffn_b)


def _odd_layer(x, xb, batch, seq, w_in, lam_re, lam_im, log_dt, b_re, b_im, c_re, c_im, d_skip,
               glu_w, glu_b, conv_w, a_log, dt_bias, norm_w, w_out, ln_mix_g, ln_mix_b,
               router_w, w1, w3, w2, ln_ffn_g, ln_ffn_b):
    t, d = x.shape
    nh = N_HEADS_GDN
    wide = SSM_WIDTH + 4 * nh * GDN_D
    w_in_b = w_in[:, :wide].astype(BF16)
    u = _matmul(xb, w_in_b[:, :SSM_WIDTH], F32, *MM_TILE, "in_proj_ssm")
    h1 = _matmul(xb, w_in_b[:, SSM_WIDTH:], BF16, *MM_TILE, "in_proj_gdn")
    w_small = jnp.pad(w_in[:, wide:], ((0, 0), (0, LANES - 2 * nh)))
    ba = _matmul(x, _split_hi_lo(w_small), F32, 512, LANES, "in_proj_gates")

    y = _s5_scan(u, lam_re, lam_im, log_dt, b_re, b_im, c_re, c_im, batch, seq)
    oc = _s5_glu(y, u, d_skip, glu_w.astype(BF16), glu_b)

    pad_hi = LANES - 2 * nh
    a_log_pad = jnp.pad(a_log, (nh, pad_hi)).reshape(1, LANES)
    dt_bias_pad = jnp.pad(dt_bias, (nh, pad_hi)).reshape(1, LANES)
    od = _gated_deltanet(h1, ba, conv_w, a_log_pad, dt_bias_pad, norm_w, batch, seq)

    w_out_b = w_out.astype(BF16)
    rw = _split_hi_lo(jnp.pad(router_w, ((0, 0), (0, LANES - N_EXPERTS))))
    x, route = _proj_ln(oc, od, w_out_b[:SSM_WIDTH], w_out_b[SSM_WIDTH:], x, ln_mix_g, ln_mix_b,
                        router_w=rw)

    src, gates, pos, tile_expert, n_valid, tile_rows = _moe_routing(route, MOE_ROWS)
    ys = _moe_ffn(x, src, gates, tile_expert, n_valid, tile_rows, w1, w3, w2, MOE_ROWS, MOE_FF_CHUNK)
    return _moe_combine_ln(ys, pos, x, ln_ffn_g, ln_ffn_b)


def kernel(x, even_w_in, even_w_out, even_ln_mix_g, even_ln_mix_b, even_ffn_w1, even_ffn_w3, even_ffn_w2, even_ln_ffn_g, even_ln_ffn_b, odd_w_in, odd_ssm_lam_re, odd_ssm_lam_im, odd_ssm_log_dt, odd_ssm_b_re, odd_ssm_b_im, odd_ssm_c_re, odd_ssm_c_im, odd_ssm_d, odd_glu_w, odd_glu_b, odd_gdn_conv_w, odd_gdn_a_log, odd_gdn_dt_bias, odd_gdn_norm_w, odd_w_out, odd_ln_mix_g, odd_ln_mix_b, odd_router_w, odd_moe_w1, odd_moe_w3, odd_moe_w2, odd_ln_ffn_g, odd_ln_ffn_b):
    batch, seq, d = x.shape
    xf = x.reshape(batch * seq, d)
    xf, xb = _even_layer(xf, batch, seq, even_w_in[0], even_w_out[0],
                         even_ln_mix_g[0], even_ln_mix_b[0], even_ffn_w1[0], even_ffn_w3[0],
                         even_ffn_w2[0], even_ln_ffn_g[0], even_ln_ffn_b[0])
    out = _odd_layer(xf, xb, batch, seq, odd_w_in[0], odd_ssm_lam_re[0], odd_ssm_lam_im[0],
                     odd_ssm_log_dt[0], odd_ssm_b_re[0], odd_ssm_b_im[0], odd_ssm_c_re[0],
                     odd_ssm_c_im[0], odd_ssm_d[0], odd_glu_w[0], odd_glu_b[0], odd_gdn_conv_w[0],
                     odd_gdn_a_log[0], odd_gdn_dt_bias[0], odd_gdn_norm_w[0], odd_w_out[0],
                     odd_ln_mix_g[0], odd_ln_mix_b[0], odd_router_w[0], odd_moe_w1[0],
                     odd_moe_w3[0], odd_moe_w2[0], odd_ln_ffn_g[0], odd_ln_ffn_b[0])
    return out.reshape(batch, seq, d)
```

```python
import functools
import math

import jax
import jax.numpy as jnp
from jax import lax
from jax.experimental import pallas as pl
from jax.experimental.pallas import tpu as pltpu

F32 = jnp.float32
BF16 = jnp.bfloat16
HIGHEST = lax.Precision.HIGHEST

HEAD_DIM = 128
N_HEADS_SB = 8
N_HEADS_DW = 8
DW_PATTERNS = ((128, 1), (512, 4), (2048, 16))
DW_KEYS = 128
DW_TILE = 2048
SSM_WIDTH = 1024
SSM_GROUP = 16
SSM_GROUPS = 64
SSM_STATE = 64
SSM_CHUNK = 8
N_HEADS_GDN = 8
GDN_D = 128
GDN_CONV = 4
GDN_TILE = 256
N_EXPERTS = 8
DEPTH = 2
DEEPNORM_ALPHA = (2 * DEPTH) ** 0.25
LN_EPS = 1e-5
RMS_EPS = 1e-6

LANES = 128
SUBLANES = 8
VMEM_LIMIT = 56 * 1024 * 1024
MM_TILE = (1024, 1024)
MOE_ROWS = 1024
MOE_FF_CHUNK = 512
SB_SKIP_LOG = -104.0


def _cp(*sem):
    return pltpu.CompilerParams(dimension_semantics=sem, vmem_limit_bytes=VMEM_LIMIT)


def _layer_norm(y, g, b):
    mu = jnp.mean(y, axis=-1, keepdims=True)
    yc = y - mu
    var = jnp.mean(yc * yc, axis=-1, keepdims=True)
    return yc * lax.rsqrt(var + LN_EPS) * g + b


def _softplus(x):
    return jnp.maximum(x, 0.0) + jnp.log1p(jnp.exp(-jnp.abs(x)))


def _dot(a, b, precision=None):
    return jnp.dot(a, b, preferred_element_type=F32, precision=precision)


def _dot_nt(a, b, precision=None):
    return lax.dot_general(a, b, (((1,), (1,)), ((), ())),
                           preferred_element_type=F32, precision=precision)


def _dot_tn(a, b, precision=None):
    return lax.dot_general(a, b, (((0,), (0,)), ((), ())),
                           preferred_element_type=F32, precision=precision)


def _split_hi_lo(w):
    hi = w.astype(BF16)
    return jnp.stack([hi, (w - hi.astype(F32)).astype(BF16)])


def _dot_split(x, w_ref):
    xh = x.astype(BF16)
    xl = (x - xh.astype(F32)).astype(BF16)
    wh = w_ref[0]
    return _dot(xh, wh) + _dot(xl, wh) + _dot(xh, w_ref[1])


def _mm_kernel(a_ref, b_ref, o_ref):
    o_ref[...] = _dot(a_ref[...].astype(BF16), b_ref[...]).astype(o_ref.dtype)


def _mm_split_kernel(a_ref, b_ref, o_ref):
    o_ref[...] = _dot_split(a_ref[...], b_ref).astype(o_ref.dtype)


def _matmul(a, b, out_dtype, tm, tn, name):
    m, k = a.shape
    n = b.shape[-1]
    if b.ndim == 3:
        body, b_spec = _mm_split_kernel, pl.BlockSpec((2, k, tn), lambda i, j: (0, 0, j))
    else:
        body, b_spec = _mm_kernel, pl.BlockSpec((k, tn), lambda i, j: (0, j))
    return pl.pallas_call(
        body,
        grid=(m // tm, n // tn),
        in_specs=[pl.BlockSpec((tm, k), lambda i, j: (i, 0)), b_spec],
        out_specs=pl.BlockSpec((tm, tn), lambda i, j: (i, j)),
        out_shape=jax.ShapeDtypeStruct((m, n), out_dtype),
        compiler_params=_cp("parallel", "parallel"),
        name=name,
    )(a, b)


def _proj_ln_kernel(a0_ref, a1_ref, w0_ref, w1_ref, res_ref, g_ref, b_ref, *rest, with_router):
    if with_router:
        rw_ref, of_ref, rt_ref = rest
    else:
        of_ref, ob_ref = rest
    mix = _dot(a0_ref[...], w0_ref[...]) + _dot(a1_ref[...], w1_ref[...])
    xn = _layer_norm(DEEPNORM_ALPHA * res_ref[...] + mix, g_ref[...], b_ref[...])
    of_ref[...] = xn
    if with_router:
        logits = _dot_split(xn, rw_ref)
        lane = lax.broadcasted_iota(jnp.int32, logits.shape, 1).astype(F32)
        neg = jnp.float32(-jnp.inf)
        lg = jnp.where(lane < N_EXPERTS, logits, neg)
        m1 = jnp.max(lg, axis=1, keepdims=True)
        i1 = jnp.min(jnp.where(lg == m1, lane, float(LANES)), axis=1, keepdims=True)
        lg2 = jnp.where(lane == i1, neg, lg)
        m2 = jnp.max(lg2, axis=1, keepdims=True)
        i2 = jnp.min(jnp.where(lg2 == m2, lane, float(LANES)), axis=1, keepdims=True)
        e2 = jnp.exp(m2 - m1)
        p1 = 1.0 / (1.0 + e2)
        p2 = e2 * p1
        rt = jnp.where(lane == 0.0, i1, jnp.where(lane == 1.0, i2,
                       jnp.where(lane == 2.0, p1, jnp.where(lane == 3.0, p2, 0.0))))
        rt_ref[...] = rt
    else:
        ob_ref[...] = xn.astype(BF16)


def _proj_ln(a0, a1, w0, w1, res, g, b, router_w=None, tm=512):
    t, d = res.shape
    k0, k1 = a0.shape[1], a1.shape[1]
    with_router = router_w is not None
    row = lambda i: (i, 0)
    fixed = lambda i: (0, 0)
    in_specs = [pl.BlockSpec((tm, k0), row), pl.BlockSpec((tm, k1), row),
                pl.BlockSpec((k0, d), fixed), pl.BlockSpec((k1, d), fixed),
                pl.BlockSpec((tm, d), row), pl.BlockSpec((1, d), fixed), pl.BlockSpec((1, d), fixed)]
    args = [a0, a1, w0, w1, res, g.reshape(1, d), b.reshape(1, d)]
    if with_router:
        in_specs.append(pl.BlockSpec((2, d, LANES), lambda i: (0, 0, 0)))
        out_specs = [pl.BlockSpec((tm, d), row), pl.BlockSpec((tm, LANES), row)]
        out_shape = [jax.ShapeDtypeStruct((t, d), F32), jax.ShapeDtypeStruct((t, LANES), F32)]
        args.append(router_w)
    else:
        out_specs = [pl.BlockSpec((tm, d), row), pl.BlockSpec((tm, d), row)]
        out_shape = [jax.ShapeDtypeStruct((t, d), F32), jax.ShapeDtypeStruct((t, d), BF16)]
    return pl.pallas_call(
        functools.partial(_proj_ln_kernel, with_router=with_router),
        grid=(t // tm,), in_specs=in_specs, out_specs=out_specs, out_shape=out_shape,
        compiler_params=_cp("parallel"),
        name="proj_ln_router" if with_router else "proj_ln",
    )(*args)


def _swiglu_acc(x, w1_ref, w3_ref, w2_ref, acc_ref):
    h1 = _dot(x, w1_ref[...].astype(BF16))
    h3 = _dot(x, w3_ref[...].astype(BF16))
    act = (h1 * jax.nn.sigmoid(h1) * h3).astype(BF16)
    acc_ref[...] += _dot(act, w2_ref[...].astype(BF16))


def _ffn_kernel(x_ref, w1_ref, w3_ref, w2_ref, res_ref, g_ref, b_ref, of_ref, ob_ref, acc_ref, *, nj):
    j = pl.program_id(1)

    @pl.when(j == 0)
    def _():
        acc_ref[...] = jnp.zeros_like(acc_ref)

    _swiglu_acc(x_ref[...], w1_ref, w3_ref, w2_ref, acc_ref)

    @pl.when(j == nj - 1)
    def _():
        xn = _layer_norm(DEEPNORM_ALPHA * res_ref[...] + acc_ref[...], g_ref[...], b_ref[...])
        of_ref[...] = xn
        ob_ref[...] = xn.astype(BF16)


def _ffn_ln(xb, res, w1, w3, w2, g, b, tm=512, tf=512):
    t, d = res.shape
    dff = w1.shape[1]
    nj = dff // tf
    row = lambda i, j: (i, 0)
    fixed = lambda i, j: (0, 0)
    return pl.pallas_call(
        functools.partial(_ffn_kernel, nj=nj),
        grid=(t // tm, nj),
        in_specs=[pl.BlockSpec((tm, d), row),
                  pl.BlockSpec((d, tf), lambda i, j: (0, j)),
                  pl.BlockSpec((d, tf), lambda i, j: (0, j)),
                  pl.BlockSpec((tf, d), lambda i, j: (j, 0)),
                  pl.BlockSpec((tm, d), row), pl.BlockSpec((1, d), fixed), pl.BlockSpec((1, d), fixed)],
        out_specs=[pl.BlockSpec((tm, d), row), pl.BlockSpec((tm, d), row)],
        out_shape=[jax.ShapeDtypeStruct((t, d), F32), jax.ShapeDtypeStruct((t, d), BF16)],
        scratch_shapes=[pltpu.VMEM((tm, d), F32)],
        compiler_params=_cp("parallel", "arbitrary"),
        name="ffn_ln",
    )(xb, w1, w3, w2, res, g.reshape(1, d), b.reshape(1, d))


SB_QTILES = 4


def _sb_kernel(q_ref, k_ref, v_ref, o_ref, acc_ref, carry_ref, *, tq, scale):
    first_tile = pl.program_id(1) * SB_QTILES
    row = lax.broadcasted_iota(jnp.int32, (tq, tq), 0)
    col = lax.broadcasted_iota(jnp.int32, (tq, tq), 1)
    later_sum = (row > col).astype(BF16)
    past = col < row
    qs = [q_ref[t * tq:(t + 1) * tq, :] for t in range(SB_QTILES)]

    def blocks_terms(chains):
        starts = [pl.multiple_of(kb * tq, tq) for _, kb, _ in chains]
        z = [_dot_nt(q, k_ref[pl.ds(s, tq), :]) * scale for (q, _, _), s in zip(chains, starts)]
        lk = [-_softplus(zz) for zz in z]
        lk = [jnp.where(past, l, 0.0) if dg else l for l, (_, _, dg) in zip(lk, chains)]
        hi = [l.astype(BF16) for l in lk]
        lo = [(l - h.astype(F32)).astype(BF16) for l, h in zip(lk, hi)]
        later = [_dot(h, later_sum) + _dot(l, later_sum) for h, l in zip(hi, lo)]
        logw = [zz + l + lt for zz, l, lt in zip(z, lk, later)]
        tot = [jnp.sum(l, axis=1, keepdims=True) for l in lk]
        return logw, tot, [v_ref[pl.ds(s, tq), :] for s in starts]

    chains = []
    for t in range(SB_QTILES):
        chains += [(qs[t], first_tile + t, True), (qs[t], jnp.maximum(first_tile + t - 1, 0), False)]
    logw, tot, vals = blocks_terms(chains)
    for t in range(SB_QTILES):
        has_prev = first_tile + t >= 1
        w0 = jnp.where(past, jnp.exp(logw[2 * t]), 0.0)
        w1 = jnp.where(has_prev, jnp.exp(logw[2 * t + 1] + tot[2 * t]), 0.0)
        acc_ref[t] = _dot(w0.astype(BF16), vals[2 * t]) + _dot(w1.astype(BF16), vals[2 * t + 1])
        carry_ref[t] = tot[2 * t] + jnp.where(has_prev, tot[2 * t + 1], 0.0)

    for t in range(SB_QTILES):
        def cond(kb, t=t):
            return jnp.logical_and(kb >= 0, jnp.max(carry_ref[t]) > SB_SKIP_LOG)

        def body(kb, t=t):
            (lw,), (tt,), (v,) = blocks_terms([(qs[t], kb, False)])
            acc_ref[t] += _dot(jnp.exp(lw + carry_ref[t]).astype(BF16), v)
            carry_ref[t] += tt
            return kb - 1

        lax.while_loop(cond, body, first_tile + t - 2)
        o_ref[t * tq:(t + 1) * tq, :] = acc_ref[t].astype(o_ref.dtype)


def _sb_attention(h, batch, seq, tq=256):
    nh = N_HEADS_SB
    rows = SB_QTILES * tq
    return pl.pallas_call(
        functools.partial(_sb_kernel, tq=tq, scale=HEAD_DIM ** -0.5),
        grid=(batch * nh, seq // rows),
        in_specs=[pl.BlockSpec((None, rows, HEAD_DIM), lambda bh, i: (bh // nh, i, bh % nh)),
                  pl.BlockSpec((None, seq, HEAD_DIM), lambda bh, i: (bh // nh, 0, nh + bh % nh)),
                  pl.BlockSpec((None, seq, HEAD_DIM), lambda bh, i: (bh // nh, 0, 2 * nh + bh % nh))],
        out_specs=pl.BlockSpec((None, rows, HEAD_DIM), lambda bh, i: (bh // nh, i, bh % nh)),
        out_shape=jax.ShapeDtypeStruct((batch, seq, nh * HEAD_DIM), BF16),
        scratch_shapes=[pltpu.VMEM((SB_QTILES, tq, HEAD_DIM), F32), pltpu.VMEM((SB_QTILES, tq, 1), F32)],
        compiler_params=_cp("parallel", "arbitrary"),
        name="sb_attention",
    )(h, h, h)


DW_UNROLL = 8


def _dw_kernel(q_ref, k_ref, v_ref, o_ref, m_ref, l_ref, acc_ref, kp_ref, vp_ref, *, scale):
    nk = DW_KEYS
    t0 = pl.program_id(1) * DW_TILE

    @pl.when(pl.program_id(1) == 0)
    def _():
        kp_ref[...] = jnp.zeros_like(kp_ref)
        vp_ref[...] = jnp.zeros_like(vp_ref)

    ii = lax.broadcasted_iota(jnp.int32, (nk, 2 * nk), 0)
    jj = lax.broadcasted_iota(jnp.int32, (nk, 2 * nk), 1)
    band = jnp.logical_and(jj >= ii, jj <= ii + nk)
    band_prev = jnp.logical_and(band, jj < nk)
    band_own = jnp.logical_and(band, jj >= nk)
    neg = jnp.float32(-jnp.inf)

    for p_idx, (window, dil) in enumerate(DW_PATTERNS):
        assert window // dil == nk
        span = nk * dil
        n_steps = DW_TILE // nk

        def group(g, _, dil=dil, span=span, p_idx=p_idx):
            subs = range(DW_UNROLL)
            blk = [(g * DW_UNROLL + u) // dil for u in subs]
            res = [(g * DW_UNROLL + u) % dil for u in subs]
            off = [b * span + r for b, r in zip(blk, res)]
            rows = [pl.ds(o, nk, stride=dil) for o in off]
            cur = [pl.ds(t0 + o, nk, stride=dil) for o in off]
            has_prev = [(t0 + b * span) > 0 for b in blk]
            prev = [pl.ds(jnp.maximum(t0 + b * span - span, 0) + r, nk, stride=dil) for b, r in zip(blk, res)]
            qs = [q_ref[rw, :].astype(BF16) for rw in rows]
            k_cur = [k_ref[cr, :].astype(BF16) for cr in cur]
            v_cur = [v_ref[cr, :].astype(BF16) for cr in cur]
            if span == DW_TILE:
                k_prev = [kp_ref[r] for r in res]
                v_prev = [vp_ref[r] for r in res]
                for r, kc, vc in zip(res, k_cur, v_cur):
                    kp_ref[r] = kc
                    vp_ref[r] = vc
            else:
                k_prev = [k_ref[pv, :].astype(BF16) for pv in prev]
                v_prev = [v_ref[pv, :].astype(BF16) for pv in prev]
            keys = [jnp.concatenate([kp, kc], axis=0) for kp, kc in zip(k_prev, k_cur)]
            sc = [_dot_nt(qq, kk) * scale for qq, kk in zip(qs, keys)]
            sc = [jnp.where(jnp.logical_or(band_own, jnp.logical_and(band_prev, hp)), s, neg)
                  for s, hp in zip(sc, has_prev)]
            m_new = [jnp.max(s, axis=1, keepdims=True) for s in sc]
            p = [jnp.exp(s - m) for s, m in zip(sc, m_new)]
            l_new = [jnp.sum(pp, axis=1, keepdims=True) for pp in p]
            vals = [jnp.concatenate([vp, vc], axis=0) for vp, vc in zip(v_prev, v_cur)]
            num = [_dot(pp.astype(BF16), vv) for pp, vv in zip(p, vals)]
            for rw, m, l, nm in zip(rows, m_new, l_new, num):
                m_ref[p_idx, rw, :] = jnp.broadcast_to(m, (nk, HEAD_DIM))
                l_ref[p_idx, rw, :] = jnp.broadcast_to(l, (nk, HEAD_DIM))
                acc_ref[p_idx, rw, :] = nm
            return 0

        lax.fori_loop(0, n_steps // DW_UNROLL, group, 0)

    m_all = m_ref[...]
    m_tot = jnp.max(m_all, axis=0)
    w = jnp.exp(m_all - m_tot[None])
    den = jnp.sum(w * l_ref[...], axis=0)
    num = jnp.sum(w * acc_ref[...], axis=0)
    o_ref[...] = (num / den).astype(o_ref.dtype)


def _dw_attention(h, batch, seq):
    nh = N_HEADS_DW
    return pl.pallas_call(
        functools.partial(_dw_kernel, scale=HEAD_DIM ** -0.5),
        grid=(batch * nh, seq // DW_TILE),
        in_specs=[pl.BlockSpec((None, DW_TILE, HEAD_DIM), lambda bh, i: (bh // nh, i, bh % nh)),
                  pl.BlockSpec((None, seq, HEAD_DIM), lambda bh, i: (bh // nh, 0, nh + bh % nh)),
                  pl.BlockSpec((None, seq, HEAD_DIM), lambda bh, i: (bh // nh, 0, 2 * nh + bh % nh))],
        out_specs=pl.BlockSpec((None, DW_TILE, HEAD_DIM), lambda bh, i: (bh // nh, i, bh % nh)),
        out_shape=jax.ShapeDtypeStruct((batch, seq, nh * HEAD_DIM), BF16),
        scratch_shapes=[pltpu.VMEM((len(DW_PATTERNS), DW_TILE, HEAD_DIM), F32)] * 3
                       + [pltpu.VMEM((DW_TILE // DW_KEYS, DW_KEYS, HEAD_DIM), BF16)] * 2,
        compiler_params=_cp("parallel", "arbitrary"),
        name="dw_attention",
    )(h, h, h)


def _s5_params(lam_re, lam_im, log_dt, b_re, b_im, c_re, c_im, n_scan):
    L = SSM_CHUNK
    gpl = LANES // SSM_GROUP
    nlb = SSM_GROUPS // gpl
    dt = jnp.exp(log_dt)[:, None]
    mag_log = lam_re * dt
    ang = lam_im * dt

    def power(n):
        n = jnp.asarray(n, F32)[..., None, None]
        mag = jnp.exp(mag_log * n)
        return mag * jnp.cos(ang * n), mag * jnp.sin(ang * n)

    lr, li = power(jnp.ones(()))
    den = lam_re * lam_re + lam_im * lam_im
    cr = ((lr - 1.0) * lam_re + li * lam_im) / den
    ci = (li * lam_re - (lr - 1.0) * lam_im) / den
    bbr = cr[..., None] * b_re - ci[..., None] * b_im
    bbi = cr[..., None] * b_im + ci[..., None] * b_re

    pr, pi = power(jnp.arange(L + 1, dtype=F32))
    mr = pr[:L, :, :, None] * bbr - pi[:L, :, :, None] * bbi
    mi = pr[:L, :, :, None] * bbi + pi[:L, :, :, None] * bbr
    kk = (jnp.einsum('ghp,tgpk->tghk', c_re, mr, precision=HIGHEST)
          - jnp.einsum('ghp,tgpk->tghk', c_im, mi, precision=HIGHEST))
    jj = jnp.arange(L)[:, None]
    ii = jnp.arange(L)[None, :]
    lag = ii - jj
    toe = jnp.where((lag >= 0)[:, :, None, None, None], kk[jnp.clip(lag, 0, L - 1)], 0.0)
    toe = toe.reshape(L, L, nlb, gpl, SSM_GROUP, SSM_GROUP)
    c_toe = toe.transpose(2, 0, 3, 5, 1, 4).reshape(nlb, L * LANES, L * SSM_GROUP)
    qr = pr[L - 1 - jnp.arange(L)]
    qi = pi[L - 1 - jnp.arange(L)]
    inr = (qr[..., None] * bbr - qi[..., None] * bbi).reshape(L, nlb, gpl, SSM_STATE, SSM_GROUP)
    ini = (qr[..., None] * bbi + qi[..., None] * bbr).reshape(L, nlb, gpl, SSM_STATE, SSM_GROUP)
    c_in = jnp.concatenate(
        [part.transpose(1, 0, 2, 4, 3).reshape(nlb, L * LANES, SSM_STATE) for part in (inr, ini)], axis=2)
    orr = c_re[None] * pr[1:, :, None, :] - c_im[None] * pi[1:, :, None, :]
    oii = c_re[None] * pi[1:, :, None, :] + c_im[None] * pr[1:, :, None, :]
    half = gpl * SSM_STATE
    c_out = jnp.concatenate(
        [part.reshape(L, nlb, gpl, SSM_GROUP, SSM_STATE).transpose(1, 2, 4, 0, 3)
         .reshape(nlb, half, L * SSM_GROUP) for part in (orr, -oii)], axis=1)
    ar, ai = power(float(L) * (2.0 ** jnp.arange(n_scan, dtype=F32)))
    ar = ar.reshape(n_scan, nlb, half).transpose(1, 0, 2)
    ai = ai.reshape(n_scan, nlb, half).transpose(1, 0, 2)
    return c_toe.astype(BF16), c_in.astype(BF16), c_out.astype(BF16), ar, ai


def _spread_matrix(inner, reps, period):
    rows = jnp.arange(inner * period)
    cols = jnp.arange(inner * reps * period)
    same_a = (rows[:, None] // period) == (cols[None, :] // (reps * period))
    same_c = (rows[:, None] % period) == (cols[None, :] % period)
    return jnp.logical_and(same_a, same_c).astype(BF16)


def _spread_groups(compact, spread, row_period, col_period):
    gpl = LANES // SSM_GROUP
    full = _dot(compact, spread)
    rg = (lax.broadcasted_iota(jnp.int32, full.shape, 0) // row_period) % gpl
    cg = (lax.broadcasted_iota(jnp.int32, full.shape, 1) // col_period) % gpl
    return jnp.where(rg == cg, full, 0.0).astype(BF16)


def _s5_kernel(x_ref, ct_ref, ci_ref, co_ref, et_ref, ei_ref, ar_ref, ai_ref, y_ref,
               wt_ref, wi_ref, wo_ref, sr_ref, si_ref, *, n_scan, m):
    L = SSM_CHUNK

    @pl.when(jnp.logical_and(pl.program_id(1) == 0, pl.program_id(2) == 0))
    def _():
        wt_ref[...] = _spread_groups(ct_ref[...], et_ref[...], SSM_GROUP, SSM_GROUP)
        wi_ref[...] = _spread_groups(ci_ref[...], ei_ref[...], SSM_GROUP, SSM_STATE)
        wo_ref[...] = _spread_groups(co_ref[...], et_ref[...], SSM_STATE, SSM_GROUP)

    @pl.when(pl.program_id(2) == 0)
    def _():
        sr_ref[...] = jnp.zeros_like(sr_ref)
        si_ref[...] = jnp.zeros_like(si_ref)

    xc = jnp.concatenate([x_ref[pl.ds(j, m, stride=L), :].astype(BF16) for j in range(L)], axis=1)
    yc = _dot(xc, wt_ref[...])
    z = _dot(xc, wi_ref[...])
    half = z.shape[1] // 2
    zr, zi = z[:, :half], z[:, half:]
    row = lax.broadcasted_iota(jnp.int32, (m, half), 0)
    pr, pi = sr_ref[0:1, :], si_ref[0:1, :]
    a1r, a1i = ar_ref[0:1, :], ai_ref[0:1, :]
    first = row == 0
    zr, zi = (zr + jnp.where(first, a1r * pr - a1i * pi, 0.0),
              zi + jnp.where(first, a1r * pi + a1i * pr, 0.0))
    for k in range(n_scan):
        s = 1 << k
        ar = ar_ref[k:k + 1, :]
        ai = ai_ref[k:k + 1, :]
        keep = row >= s
        tr = jnp.where(keep, pltpu.roll(zr, s, 0), 0.0)
        ti = jnp.where(keep, pltpu.roll(zi, s, 0), 0.0)
        zr, zi = zr + ar * tr - ai * ti, zi + ar * ti + ai * tr
    keep = row >= 1
    s_in = jnp.concatenate([jnp.where(keep, pltpu.roll(zr, 1, 0), pr),
                            jnp.where(keep, pltpu.roll(zi, 1, 0), pi)], axis=1)
    sr_ref[...] = jnp.broadcast_to(zr[m - 1:m, :], sr_ref.shape)
    si_ref[...] = jnp.broadcast_to(zi[m - 1:m, :], si_ref.shape)
    yc = yc + _dot(s_in.astype(BF16), wo_ref[...])
    for i in range(L):
        y_ref[pl.ds(i, m, stride=L), :] = yc[:, i * LANES:(i + 1) * LANES]


def _s5_scan(h1, lam_re, lam_im, log_dt, b_re, b_im, c_re, c_im, batch, seq, rows=2048):
    L = SSM_CHUNK
    rows = min(rows, seq)
    m = rows // L
    nt = seq // rows
    n_scan = max(1, math.ceil(math.log2(m)))
    c_toe, c_in, c_out, ar, ai = _s5_params(lam_re, lam_im, log_dt, b_re, b_im, c_re, c_im, n_scan)
    nlb, half = ar.shape[0], ar.shape[2]
    gpl = LANES // SSM_GROUP
    e_toe = _spread_matrix(L, gpl, SSM_GROUP)
    e_in = _spread_matrix(2, gpl, SSM_STATE)
    per_lb = lambda *blk: pl.BlockSpec((None,) + blk, lambda l, b, i: (l, 0, 0))
    whole = lambda a: pl.BlockSpec(a.shape, lambda l, b, i: (0, 0))
    tile = pl.BlockSpec((rows, LANES), lambda l, b, i: (b * nt + i, l))
    wide = L * LANES
    return pl.pallas_call(
        functools.partial(_s5_kernel, n_scan=n_scan, m=m),
        grid=(nlb, batch, nt),
        in_specs=[tile, per_lb(wide, c_toe.shape[2]), per_lb(wide, c_in.shape[2]),
                  per_lb(2 * half, c_out.shape[2]), whole(e_toe), whole(e_in),
                  per_lb(n_scan, half), per_lb(n_scan, half)],
        out_specs=tile,
        out_shape=jax.ShapeDtypeStruct((batch * seq, SSM_WIDTH), F32),
        scratch_shapes=[pltpu.VMEM((wide, wide), BF16), pltpu.VMEM((wide, 2 * half), BF16),
                        pltpu.VMEM((2 * half, wide), BF16)] + [pltpu.VMEM((SUBLANES, half), F32)] * 2,
        compiler_params=_cp("parallel", "arbitrary", "arbitrary"),
        name="s5_scan",
    )(h1, c_toe, c_in, c_out, e_toe, e_in, ar, ai)


def _s5_glu_kernel(y_ref, u_ref, d_ref, w_ref, b_ref, o_ref):
    y = y_ref[...] + d_ref[...] * u_ref[...]
    c = math.sqrt(2.0 / math.pi)
    z = 0.5 * y * (1.0 + jnp.tanh(c * (y + 0.044715 * (y * y * y))))
    gate = jax.nn.sigmoid(_dot(z.astype(BF16), w_ref[...]) + b_ref[...])
    o_ref[...] = (z * gate).astype(o_ref.dtype)


def _s5_glu(y, h1, d_skip, glu_w, glu_b, tm=512):
    t, w = y.shape
    row = lambda i: (i, 0)
    fixed = lambda i: (0, 0)
    return pl.pallas_call(
        _s5_glu_kernel,
        grid=(t // tm,),
        in_specs=[pl.BlockSpec((tm, w), row), pl.BlockSpec((tm, w), row),
                  pl.BlockSpec((1, w), fixed), pl.BlockSpec((w, w), fixed), pl.BlockSpec((1, w), fixed)],
        out_specs=pl.BlockSpec((tm, w), row),
        out_shape=jax.ShapeDtypeStruct((t, w), BF16),
        compiler_params=_cp("parallel"),
        name="s5_glu",
    )(y, h1, d_skip.reshape(1, w), glu_w, glu_b.reshape(1, w))


def _bdot(a, b):
    return _dot(a.astype(BF16), b.astype(BF16))


def _bdot_nt(a, b):
    return _dot_nt(a.astype(BF16), b.astype(BF16))


def _gdn_kernel(q_ref, k_ref, v_ref, gate_ref, ba_ref, wq_ref, wk_ref, wv_ref, alog_ref, dtb_ref,
                nw_ref, o_ref, state_ref, hq_ref, hk_ref, hv_ref, gct_ref, *, lt, nh, hp):
    d = GDN_D
    c = lt
    head0 = (pl.program_id(0) % (nh // hp)) * hp

    @pl.when(pl.program_id(1) == 0)
    def _():
        state_ref[...] = jnp.zeros_like(state_ref)
        for halo_ref in (hq_ref, hk_ref, hv_ref):
            halo_ref[0:SUBLANES, :] = jnp.zeros((SUBLANES, hp * d), F32)

    def conv_silu(x_ref, w_ref, xe_ref):
        w = w_ref[...]
        xe_ref[SUBLANES:, :] = x_ref[...].astype(F32)
        y = xe_ref[SUBLANES:, :] * w[GDN_CONV - 1:GDN_CONV, :]
        for back in range(1, GDN_CONV):
            y = y + xe_ref[pl.ds(SUBLANES - back, lt), :] * w[GDN_CONV - 1 - back:GDN_CONV - back, :]
        xe_ref[0:SUBLANES, :] = xe_ref[lt:, :]
        return y * jax.nn.sigmoid(y)

    def l2norm(t):
        return t * lax.rsqrt(jnp.sum(t * t, axis=-1, keepdims=True) + RMS_EPS)

    q_cs = conv_silu(q_ref, wq_ref, hq_ref)
    k_cs = conv_silu(k_ref, wk_ref, hk_ref)
    v_cs = conv_silu(v_ref, wv_ref, hv_ref)

    ba = ba_ref[...]
    lane = lax.broadcasted_iota(jnp.int32, ba.shape, 1)
    sig_ba = jax.nn.sigmoid(ba)
    g_full = -jnp.exp(alog_ref[...]) * _softplus(ba + dtb_ref[...])
    gc_full = g_full
    rows_i = lax.broadcasted_iota(jnp.int32, gc_full.shape, 0)
    step = 1
    while step < c:
        gc_full = gc_full + jnp.where(rows_i >= step, pltpu.roll(gc_full, step, 0), 0.0)
        step *= 2
    gct_ref[...] = gc_full.T

    ri = lax.broadcasted_iota(jnp.int32, (c, c), 0)
    ci = lax.broadcasted_iota(jnp.int32, (c, c), 1)
    tri = ri >= ci
    strict = ri > ci
    eye = (ri == ci).astype(F32)
    in16 = (ri // 16) == (ci // 16)
    merges = []
    width = 16
    while width < c:
        inner = (ri // width) == (ci // width)
        outer = (ri // (2 * width)) == (ci // (2 * width))
        merges.append(jnp.logical_and(outer, jnp.logical_not(inner)))
        width *= 2

    heads = range(hp)
    cols = [slice(hh * d, (hh + 1) * d) for hh in heads]
    q = [l2norm(q_cs[:, cols[hh]]) * (d ** -0.5) for hh in heads]
    k = [l2norm(k_cs[:, cols[hh]]) for hh in heads]
    v = [v_cs[:, cols[hh]] for hh in heads]
    beta = [jnp.sum(jnp.where(lane == head0 + hh, sig_ba, 0.0), axis=1, keepdims=True) for hh in heads]
    gc = [jnp.sum(jnp.where(lane == nh + head0 + hh, gc_full, 0.0), axis=1, keepdims=True)
          for hh in heads]
    gc_row = [gct_ref[pl.ds(nh + head0 + hh, 1), :] for hh in heads]
    gc_last = [gc_row[hh][:, c - 1:c] for hh in heads]
    decay = [jnp.exp(jnp.where(tri, gc[hh] - gc_row[hh], -jnp.inf)) for hh in heads]
    kb = [k[hh] * beta[hh] for hh in heads]
    kq_kt = [_bdot_nt(jnp.concatenate([kb[hh], q[hh]], axis=0), k[hh]) for hh in heads]
    lower = [jnp.where(strict, kq_kt[hh][:c] * decay[hh], 0.0) for hh in heads]
    pw = [jnp.where(in16, -lower[hh], 0.0) for hh in heads]
    inv = [eye + pw[hh] for hh in heads]
    for _ in range(3):
        pw = [_bdot(pw[hh], pw[hh]) for hh in heads]
        inv = [inv[hh] + _bdot(inv[hh], pw[hh]) for hh in heads]
    for off in merges:
        part = [_bdot(inv[hh], jnp.where(off, lower[hh], 0.0)) for hh in heads]
        inv = [inv[hh] - _bdot(part[hh], inv[hh]) for hh in heads]
    e_gc = [jnp.exp(gc[hh]) for hh in heads]
    uw = [_bdot(inv[hh], jnp.concatenate([v[hh] * beta[hh], kb[hh] * e_gc[hh]], axis=1)) for hh in heads]
    state = [state_ref[hh] for hh in heads]
    ws_qs = [_bdot(jnp.concatenate([uw[hh][:, d:], q[hh] * e_gc[hh]], axis=0), state[hh]) for hh in heads]
    v_new = [uw[hh][:, :d] - ws_qs[hh][:c] for hh in heads]
    o = [ws_qs[hh][c:] + _bdot(kq_kt[hh][c:] * decay[hh], v_new[hh]) for hh in heads]
    for hh in heads:
        k_dec = k[hh] * jnp.exp(gc_last[hh] - gc[hh])
        state_ref[hh] = (state[hh] * jnp.exp(gc_last[hh])
                         + _dot_tn(k_dec.astype(BF16), v_new[hh].astype(BF16)))
        on = o[hh] * lax.rsqrt(jnp.mean(o[hh] * o[hh], axis=-1, keepdims=True) + RMS_EPS) * nw_ref[...]
        gt = gate_ref[:, cols[hh]].astype(F32)
        o_ref[:, cols[hh]] = (on * (gt * jax.nn.sigmoid(gt))).astype(o_ref.dtype)


def _gated_deltanet(h1, ba, conv_w, a_log_pad, dt_bias_pad, norm_w, batch, seq, lt=GDN_TILE, hp=8):
    nh = N_HEADS_GDN
    d = GDN_D
    t = batch * seq
    nt = seq // lt
    ng = nh // hp
    wide = hp * d
    base = 0
    rows = lambda off: (lambda bg, i: ((bg // ng) * nt + i, off + bg % ng))
    cw = lambda off: (lambda bg, i: (0, off + bg % ng))
    fixed = lambda bg, i: (0, 0)
    return pl.pallas_call(
        functools.partial(_gdn_kernel, lt=lt, nh=nh, hp=hp),
        grid=(batch * ng, nt),
        in_specs=[pl.BlockSpec((lt, wide), rows(base)), pl.BlockSpec((lt, wide), rows(base + ng)),
                  pl.BlockSpec((lt, wide), rows(base + 2 * ng)), pl.BlockSpec((lt, wide), rows(base + 3 * ng)),
                  pl.BlockSpec((lt, LANES), lambda bg, i: ((bg // ng) * nt + i, 0)),
                  pl.BlockSpec((GDN_CONV, wide), cw(0)), pl.BlockSpec((GDN_CONV, wide), cw(ng)),
                  pl.BlockSpec((GDN_CONV, wide), cw(2 * ng)),
                  pl.BlockSpec((1, LANES), fixed), pl.BlockSpec((1, LANES), fixed),
                  pl.BlockSpec((1, d), fixed)],
        out_specs=pl.BlockSpec((lt, wide), rows(0)),
        out_shape=jax.ShapeDtypeStruct((t, nh * d), BF16),
        scratch_shapes=[pltpu.VMEM((hp, d, d), F32)] + [pltpu.VMEM((SUBLANES + lt, wide), F32)] * 3
                       + [pltpu.VMEM((LANES, lt), F32)],
        compiler_params=_cp("parallel", "arbitrary"),
        name="gated_deltanet",
    )(h1, h1, h1, h1, ba, conv_w, conv_w, conv_w, a_log_pad, dt_bias_pad, norm_w.reshape(1, d))


DMA_ISSUE_UNROLL = 8


def _moe_kernel(te_ref, nv_ref, rows_ref, src_ref, x_hbm, w1_ref, w3_ref, w2_ref, gate_ref, o_ref,
                rows_buf, xb_ref, sem, *, nj, tm, per_step):
    i = pl.program_id(0)
    j = pl.program_id(1)
    n_used = nv_ref[0]
    valid = i < n_used

    def start_rows(tile, first, count):
        for k in range(count):
            r = first + k
            tok = src_ref[tile * tm + jnp.minimum(r, tm - 1)]
            pltpu.make_async_copy(x_hbm.at[pl.ds(tok, 1), :], rows_buf.at[pl.ds(r, 1), :], sem).start()

    @pl.when(jnp.logical_and(i == 0, j == 0))
    def _():
        def body(s, _):
            start_rows(0, s * per_step, per_step)
            return 0
        lax.fori_loop(0, nj, body, 0)

    @pl.when(j == 0)
    def _():
        o_ref[...] = jnp.zeros_like(o_ref)

    @pl.when(jnp.logical_and(j == 0, i <= n_used))
    def _():
        pltpu.make_async_copy(x_hbm.at[pl.ds(0, nj * per_step), :], rows_buf, sem).wait()
        xb_ref[...] = rows_buf[0:tm, :].astype(BF16)

    half = tm // 2
    lower_only = rows_ref[i] <= half

    @pl.when(jnp.logical_and(valid, jnp.logical_not(lower_only)))
    def _():
        start_rows(i + 1, j * per_step, per_step)
        _swiglu_acc(xb_ref[...], w1_ref, w3_ref, w2_ref, o_ref)

    @pl.when(jnp.logical_and(valid, lower_only))
    def _():
        start_rows(i + 1, j * per_step, per_step)
        _swiglu_acc(xb_ref[0:half, :], w1_ref, w3_ref, w2_ref, o_ref.at[0:half, :])

    @pl.when(j == nj - 1)
    def _():
        o_ref[...] = o_ref[...] * gate_ref[...]


def _moe_ffn(x, src, gates, tile_expert, n_valid, tile_rows, w1, w3, w2, tm, tf):
    p = src.shape[0]
    d, dff = w1.shape[1], w1.shape[2]
    nj = dff // tf
    per_step = -(-tm // nj)
    while (nj * per_step) % SUBLANES:
        per_step += 1
    jsel = lambda i, j, nv: jnp.where(i < nv[0], j, nj - 1)
    return pl.pallas_call(
        functools.partial(_moe_kernel, nj=nj, tm=tm, per_step=per_step),
        grid_spec=pltpu.PrefetchScalarGridSpec(
            num_scalar_prefetch=4, grid=(p // tm, nj),
            in_specs=[pl.BlockSpec(memory_space=pl.ANY),
                      pl.BlockSpec((None, d, tf), lambda i, j, te, nv, rows, src: (te[i], 0, jsel(i, j, nv))),
                      pl.BlockSpec((None, d, tf), lambda i, j, te, nv, rows, src: (te[i], 0, jsel(i, j, nv))),
                      pl.BlockSpec((None, tf, d), lambda i, j, te, nv, rows, src: (te[i], jsel(i, j, nv), 0)),
                      pl.BlockSpec((tm, 1), lambda i, j, te, nv, rows, src: (i, 0))],
            out_specs=pl.BlockSpec((tm, d), lambda i, j, te, nv, rows, src: (i, 0),
                                   pipeline_mode=pl.Buffered(1)),
            scratch_shapes=[pltpu.VMEM((nj * per_step, d), F32), pltpu.VMEM((tm, d), BF16),
                            pltpu.SemaphoreType.DMA(())]),
        out_shape=jax.ShapeDtypeStruct((p, d), F32),
        compiler_params=_cp("arbitrary", "arbitrary"),
        name="moe_ffn",
    )(tile_expert, n_valid, tile_rows, src, x, w1, w3, w2, gates)


def _combine_kernel(pos_ref, ys_hbm, res_ref, g_ref, b_ref, o_ref, buf_ref, sem, *, tm, t):
    i = pl.program_id(0)

    def issue(step, slot):
        def body(r, _):
            for choice in range(2):
                p = pos_ref[choice * t + step * tm + r]
                pltpu.make_async_copy(ys_hbm.at[pl.ds(p, 1), :], buf_ref.at[slot, choice, pl.ds(r, 1), :],
                                      sem.at[slot, choice]).start()
            return 0
        lax.fori_loop(0, tm, body, 0, unroll=DMA_ISSUE_UNROLL)

    @pl.when(i == 0)
    def _():
        issue(0, 0)

    @pl.when(i + 1 < pl.num_programs(0))
    def _():
        issue(i + 1, (i + 1) % 2)

    slot = i % 2
    for choice in range(2):
        pltpu.make_async_copy(ys_hbm.at[pl.ds(0, tm), :], buf_ref.at[slot, choice],
                              sem.at[slot, choice]).wait()
    y = buf_ref[slot, 0] + buf_ref[slot, 1]
    o_ref[...] = _layer_norm(DEEPNORM_ALPHA * res_ref[...] + y, g_ref[...], b_ref[...])


def _moe_combine_ln(ys, pos, res, g, b, tm=512):
    t, d = res.shape
    return pl.pallas_call(
        functools.partial(_combine_kernel, tm=tm, t=t),
        grid_spec=pltpu.PrefetchScalarGridSpec(
            num_scalar_prefetch=1, grid=(t // tm,),
            in_specs=[pl.BlockSpec(memory_space=pl.ANY),
                      pl.BlockSpec((tm, d), lambda i, pos: (i, 0)),
                      pl.BlockSpec((1, d), lambda i, pos: (0, 0)),
                      pl.BlockSpec((1, d), lambda i, pos: (0, 0))],
            out_specs=pl.BlockSpec((tm, d), lambda i, pos: (i, 0)),
            scratch_shapes=[pltpu.VMEM((2, 2, tm, d), F32), pltpu.SemaphoreType.DMA((2, 2))]),
        out_shape=jax.ShapeDtypeStruct((t, d), F32),
        compiler_params=_cp("arbitrary"),
        name="moe_combine_ln",
    )(pos, ys, res, g.reshape(1, d), b.reshape(1, d))


def _moe_routing(route, tm):
    t = route.shape[0]
    e = N_EXPERTS
    idx = route[:, 0:2].astype(jnp.int32)
    wts = route[:, 2:4]
    flat_e = idx.T.reshape(-1)
    onehot = (flat_e[:, None] == jnp.arange(e, dtype=jnp.int32)[None, :]).astype(jnp.int32)
    rank = jnp.cumsum(onehot, axis=0) - onehot
    counts = jnp.sum(onehot, axis=0)
    tiles = (counts + tm - 1) // tm
    tile_end = jnp.cumsum(tiles)
    start = (tile_end - tiles) * tm
    pos = jnp.sum(onehot * (start[None, :] + rank), axis=1)
    n_slots = 2 * t + e * tm
    n_tiles = n_slots // tm
    owner = jnp.full((n_slots,), -1, jnp.int32).at[pos].set(jnp.arange(2 * t, dtype=jnp.int32))
    used = owner >= 0
    src = jnp.where(used, owner % t, jnp.arange(n_slots, dtype=jnp.int32) % t)
    gates = jnp.where(used, wts.T.reshape(-1)[jnp.maximum(owner, 0)], 0.0)
    n_valid = tile_end[-1]
    tile_ids = jnp.arange(n_tiles, dtype=jnp.int32)
    tile_expert = jnp.sum((tile_ids[:, None] >= tile_end[None, :]).astype(jnp.int32), axis=1)
    last_expert = jnp.sum((n_valid - 1 >= tile_end).astype(jnp.int32))
    tile_expert = jnp.where(tile_ids < n_valid, tile_expert, last_expert).astype(jnp.int32)
    tile_rows = jnp.clip((start + counts)[tile_expert] - tile_ids * tm, 0, tm)
    tile_rows = jnp.where(tile_ids < n_valid, tile_rows, 0).astype(jnp.int32)
    return (src, gates.reshape(n_slots, 1), pos.astype(jnp.int32), tile_expert,
            n_valid.reshape(1).astype(jnp.int32), tile_rows)


def _even_layer(x, batch, seq, w_in, w_out, ln_mix_g, ln_mix_b, w1, w3, w2, ln_ffn_g, ln_ffn_b):
    t, d = x.shape
    w_sb = N_HEADS_SB * HEAD_DIM
    w_in_b = w_in.astype(BF16)
    ha = _matmul(x, w_in_b[:, :3 * w_sb], BF16, *MM_TILE, "in_proj_sb")
    hb = _matmul(x, w_in_b[:, 3 * w_sb:], F32, *MM_TILE, "in_proj_dw")
    oa = _sb_attention(ha.reshape(batch, seq, -1), batch, seq).reshape(t, -1)
    ob = _dw_attention(hb.reshape(batch, seq, -1), batch, seq).reshape(t, -1)
    w_out_b = w_out.astype(BF16)
    x, xb = _proj_ln(oa, ob, w_out_b[:w_sb], w_out_b[w_sb:], x, ln_mix_g, ln_mix_b)
    return _ffn_ln(xb, x, w1.astype(BF16), w3.astype(BF16), w2.astype(BF16), ln_ffn_g, ln_ffn_b)


def _odd_layer(x, xb, batch, seq, w_in, lam_re, lam_im, log_dt, b_re, b_im, c_re, c_im, d_skip,
               glu_w, glu_b, conv_w, a_log, dt_bias, norm_w, w_out, ln_mix_g, ln_mix_b,
               router_w, w1, w3, w2, ln_ffn_g, ln_ffn_b):
    t, d = x.shape
    nh = N_HEADS_GDN
    wide = SSM_WIDTH + 4 * nh * GDN_D
    w_in_b = w_in[:, :wide].astype(BF16)
    u = _matmul(xb, w_in_b[:, :SSM_WIDTH], F32, *MM_TILE, "in_proj_ssm")
    h1 = _matmul(xb, w_in_b[:, SSM_WIDTH:], BF16, *MM_TILE, "in_proj_gdn")
    w_small = jnp.pad(w_in[:, wide:], ((0, 0), (0, LANES - 2 * nh)))
    ba = _matmul(x, _split_hi_lo(w_small), F32, 512, LANES, "in_proj_gates")

    y = _s5_scan(u, lam_re, lam_im, log_dt, b_re, b_im, c_re, c_im, batch, seq)
    oc = _s5_glu(y, u, d_skip, glu_w.astype(BF16), glu_b)

    pad_hi = LANES - 2 * nh
    a_log_pad = jnp.pad(a_log, (nh, pad_hi)).reshape(1, LANES)
    dt_bias_pad = jnp.pad(dt_bias, (nh, pad_hi)).reshape(1, LANES)
    od = _gated_deltanet(h1, ba, conv_w, a_log_pad, dt_bias_pad, norm_w, batch, seq)

    w_out_b = w_out.astype(BF16)
    rw = _split_hi_lo(jnp.pad(router_w, ((0, 0), (0, LANES - N_EXPERTS))))
    x, route = _proj_ln(oc, od, w_out_b[:SSM_WIDTH], w_out_b[SSM_WIDTH:], x, ln_mix_g, ln_mix_b,
                        router_w=rw)

    src, gates, pos, tile_expert, n_valid, tile_rows = _moe_routing(route, MOE_ROWS)
    ys = _moe_ffn(x, src, gates, tile_expert, n_valid, tile_rows, w1, w3, w2, MOE_ROWS, MOE_FF_CHUNK)
    return _moe_combine_ln(ys, pos, x, ln_ffn_g, ln_ffn_b)


def kernel(x, even_w_in, even_w_out, even_ln_mix_g, even_ln_mix_b, even_ffn_w1, even_ffn_w3, even_ffn_w2, even_ln_ffn_g, even_ln_ffn_b, odd_w_in, odd_ssm_lam_re, odd_ssm_lam_im, odd_ssm_log_dt, odd_ssm_b_re, odd_ssm_b_im, odd_ssm_c_re, odd_ssm_c_im, odd_ssm_d, odd_glu_w, odd_glu_b, odd_gdn_conv_w, odd_gdn_a_log, odd_gdn_dt_bias, odd_gdn_norm_w, odd_w_out, odd_ln_mix_g, odd_ln_mix_b, odd_router_w, odd_moe_w1, odd_moe_w3, odd_moe_w2, odd_ln_ffn_g, odd_ln_ffn_b):
    batch, seq, d = x.shape
    xf = x.reshape(batch * seq, d)
    xf, xb = _even_layer(xf, batch, seq, even_w_in[0], even_w_out[0],
                         even_ln_mix_g[0], even_ln_mix_b[0], even_ffn_w1[0], even_ffn_w3[0],
                         even_ffn_w2[0], even_ln_ffn_g[0], even_ln_ffn_b[0])
    out = _odd_layer(xf, xb, batch, seq, odd_w_in[0], odd_ssm_lam_re[0], odd_ssm_lam_im[0],
                     odd_ssm_log_dt[0], odd_ssm_b_re[0], odd_ssm_b_im[0], odd_ssm_c_re[0],
                     odd_ssm_c_im[0], odd_ssm_d[0], odd_glu_w[0], odd_glu_b[0], odd_gdn_conv_w[0],
                     odd_gdn_a_log[0], odd_gdn_dt_bias[0], odd_gdn_norm_w[0], odd_w_out[0],
                     odd_ln_mix_g[0], odd_ln_mix_b[0], odd_router_w[0], odd_moe_w1[0],
                     odd_moe_w3[0], odd_moe_w2[0], odd_ln_ffn_g[0], odd_ln_ffn_b[0])
    return out.reshape(batch, seq, d)
```
